```python
import math
import jax, jax.numpy as jnp
from jax import lax
import numpy as np

D_MODEL = 1024
BATCH = 8
SEQ = 2048
DEPTH = 4

N_MIXERS = 4
HEAD_DIM = 64
ROPE_THETA = 500000.0
ROT_DIM = HEAD_DIM // 4
NEG_INF = -1e30
RMS_EPS = 1e-6
Q_BLOCK = 128

DIL_PAIRS = ((128, 1), (512, 4), (2048, 16))
A_HEADS_PER_GROUP = 5
A_HEADS = A_HEADS_PER_GROUP * len(DIL_PAIRS)
BAND_BLOCK = 64

B_HEADS = 16
B_Q_RANK = 256
B_KV_RANK = 128
B_NOPE = 64
B_ROPE = 32
B_V = 64

C_HEADS = 16
GRID_W = 64
NA_ROWS = 8
NA_COLS = 16

D_HEADS = 8
D_HEAD = 64

MLP_HIDDEN = 4 * D_MODEL
PLE_DIM = 256

kernel_name = "hybrid_interleaved_bidir_encoder"


def rmsnorm(x, g):
    xf = x.astype(jnp.float32)
    y = xf * lax.rsqrt(jnp.mean(xf * xf, axis=-1, keepdims=True) + RMS_EPS)
    return (y * g.astype(jnp.float32)).astype(x.dtype)


def rope_tables(seq_len, rot_dim):
    inv = ROPE_THETA ** (-jnp.arange(0, rot_dim, 2, dtype=jnp.float32) / rot_dim)
    ang = jnp.arange(seq_len, dtype=jnp.float32)[:, None] * inv[None, :]
    return jnp.cos(ang), jnp.sin(ang)


def apply_rope(x, cos, sin):
    r = cos.shape[-1]
    c = cos.astype(x.dtype)
    s = sin.astype(x.dtype)
    x1, x2, rest = x[..., :r], x[..., r:2 * r], x[..., 2 * r:]
    return jnp.concatenate([x1 * c - x2 * s, x2 * c + x1 * s, rest], axis=-1)


def banded_attention(q, k, v, half):
    Z, L, dh = q.shape
    qb = math.gcd(L, BAND_BLOCK)
    nb = L // qb
    kw = qb + 2 * half
    kp = jnp.pad(k, ((0, 0), (half, half), (0, 0)))
    vp = jnp.pad(v, ((0, 0), (half, half), (0, 0)))
    idx = np.arange(nb)[:, None] * qb + np.arange(kw)[None, :]
    kb = kp[:, idx]
    vb = vp[:, idx]
    rel = np.arange(kw)[None, :] - half - np.arange(qb)[:, None]
    kpos = idx - half
    valid = (np.abs(rel) <= half)[None] & ((kpos >= 0) & (kpos < L))[:, None, :]
    s = jnp.einsum('znqd,znkd->znqk', q.reshape(Z, nb, qb, dh), kb).astype(jnp.float32) * (dh ** -0.5)
    s = jnp.where(valid, s, NEG_INF)
    m = jnp.max(s, axis=-1, keepdims=True)
    e = jnp.exp(s - m)
    l = jnp.sum(e, axis=-1, keepdims=True)
    o = jnp.einsum('znqk,znkd->znqd', (e / l).astype(v.dtype), vb)
    lse = (m + jnp.log(l))[..., 0]
    return o.reshape(Z, L, dh), lse.reshape(Z, L)


def dense_block_attention(q, k, v, scale):
    B, H, S, dq = q.shape
    nb = S // Q_BLOCK
    qb = q.reshape(B, H, nb, Q_BLOCK, dq).transpose(2, 0, 1, 3, 4)

    def one(qi):
        s = jnp.einsum('bhqd,bhkd->bhqk', qi, k).astype(jnp.float32) * scale
        pr = jax.nn.softmax(s, axis=-1)
        return jnp.einsum('bhqk,bhkd->bhqd', pr.astype(v.dtype), v)

    o = lax.map(one, qb)
    return o.transpose(1, 2, 0, 3, 4).reshape(B, H, S, v.shape[-1])


def diff_block_attention(q1, q2, k1, k2, v, lam, scale):
    B, H, S, d = q1.shape
    nb = S // Q_BLOCK
    blk = lambda t: t.reshape(B, H, nb, Q_BLOCK, d).transpose(2, 0, 1, 3, 4)

    def one(qs):
        a, b = qs
        p1 = jax.nn.softmax(jnp.einsum('bhqd,bhkd->bhqk', a, k1).astype(jnp.float32) * scale, axis=-1)
        p2 = jax.nn.softmax(jnp.einsum('bhqd,bhkd->bhqk', b, k2).astype(jnp.float32) * scale, axis=-1)
        return jnp.einsum('bhqk,bhkd->bhqd', (p1 - lam * p2).astype(v.dtype), v)

    o = lax.map(one, (blk(q1), blk(q2)))
    return o.transpose(1, 2, 0, 3, 4).reshape(B, H, S, v.shape[-1])


def neighborhood_attention(q, k, v, rpb):
    B, H, S, dh = q.shape
    rows = S // GRID_W
    kh = min(NA_ROWS, rows)
    cb = NA_COLS
    kcw = 2 * NA_COLS
    ncb = GRID_W // cb
    starts = np.clip(np.arange(ncb) * cb - NA_COLS // 2, 0, GRID_W - kcw)
    key_cols = starts[:, None] + np.arange(kcw)[None, :]
    q_cols = np.arange(GRID_W).reshape(ncb, cb)
    win0 = np.clip(q_cols - NA_COLS // 2, 0, GRID_W - NA_COLS)[..., None]
    kc = key_cols[:, None, :]
    col_mask = (kc >= win0) & (kc < win0 + NA_COLS)
    col_off = np.clip(kc - q_cols[..., None] + NA_COLS - 1, 0, 2 * NA_COLS - 2)
    rpb_cols = rpb[:, :, col_off]
    grid = lambda t: t.reshape(B, H, rows, GRID_W, dh).transpose(2, 0, 1, 3, 4)
    qg, kg, vg = grid(q), grid(k), grid(v)
    scale = dh ** -0.5

    def one_row(r):
        rs = jnp.clip(r - kh // 2, 0, rows - kh)
        q_r = lax.dynamic_index_in_dim(qg, r, 0, keepdims=False).reshape(B, H, ncb, cb, dh)
        k_r = lax.dynamic_slice_in_dim(kg, rs, kh, 0)[:, :, :, key_cols]
        v_r = lax.dynamic_slice_in_dim(vg, rs, kh, 0)[:, :, :, key_cols]
        s = jnp.einsum('bhnqd,rbhnkd->bhnqrk', q_r, k_r).astype(jnp.float32) * scale
        row_off = rs + jnp.arange(kh) - r + NA_ROWS - 1
        bias = rpb_cols[:, row_off].astype(jnp.float32).transpose(0, 2, 3, 1, 4)
        s = jnp.where(col_mask[:, :, None, :], s + bias[None], NEG_INF)
        pr = jax.nn.softmax(s.reshape(B, H, ncb, cb, kh * kcw), axis=-1).reshape(s.shape)
        o = jnp.einsum('bhnqrk,rbhnkd->bhnqd', pr.astype(v.dtype), v_r)
        return o.reshape(B, H, GRID_W, dh)

    o = lax.map(one_row, jnp.arange(rows))
    return o.transpose(1, 2, 0, 3, 4).reshape(B, H, S, dh)


def _by_stride(t, dil):
    B, G, S, dh = t.shape
    return t.reshape(B, G, S // dil, dil, dh).transpose(0, 1, 3, 2, 4).reshape(B * G * dil, S // dil, dh)


def dilated_window_mixer(h, w_qkv, w_o, cos, sin):
    B, S, _ = h.shape
    G = A_HEADS_PER_GROUP
    qkv = (h @ w_qkv).reshape(B, S, 3, A_HEADS, HEAD_DIM).transpose(2, 0, 3, 1, 4)
    q = apply_rope(qkv[0], cos, sin)
    k = apply_rope(qkv[1], cos, sin)
    v = qkv[2]
    outs, lses = [], []
    for g, (window, dil) in enumerate(DIL_PAIRS):
        sl = slice(g * G, (g + 1) * G)
        half = window // (2 * dil)
        o, lse = banded_attention(_by_stride(q[:, sl], dil), _by_stride(k[:, sl], dil),
                                  _by_stride(v[:, sl], dil), half)
        outs.append(o.reshape(B, G, dil, S // dil, HEAD_DIM).transpose(0, 1, 3, 2, 4).reshape(B, G, S, HEAD_DIM))
        lses.append(lse.reshape(B, G, dil, S // dil).transpose(0, 1, 3, 2).reshape(B, G, S))
    alpha = jax.nn.softmax(jnp.stack(lses, axis=0), axis=0)
    o = jnp.concatenate([outs[g] * alpha[g][..., None].astype(h.dtype) for g in range(len(DIL_PAIRS))], axis=1)
    return o.transpose(0, 2, 1, 3).reshape(B, S, A_HEADS * HEAD_DIM) @ w_o


def latent_attention_mixer(h, w_in, q_norm, w_uq, kv_norm, w_ukv, w_o, cos, sin):
    B, S, _ = h.shape
    z = h @ w_in
    c_q = z[..., :B_Q_RANK]
    c_kv = z[..., B_Q_RANK:B_Q_RANK + B_KV_RANK]
    k_rope = z[..., B_Q_RANK + B_KV_RANK:]
    q = (rmsnorm(c_q, q_norm) @ w_uq).reshape(B, S, B_HEADS, B_NOPE + B_ROPE).transpose(0, 2, 1, 3)
    kv = (rmsnorm(c_kv, kv_norm) @ w_ukv).reshape(B, S, B_HEADS, B_NOPE + B_V).transpose(0, 2, 1, 3)
    q = jnp.concatenate([q[..., :B_NOPE], apply_rope(q[..., B_NOPE:], cos, sin)], axis=-1)
    k_rope = jnp.broadcast_to(apply_rope(k_rope, cos, sin)[:, None], (B, B_HEADS, S, B_ROPE))
    k = jnp.concatenate([kv[..., :B_NOPE], k_rope], axis=-1)
    v = kv[..., B_NOPE:]
    o = dense_block_attention(q, k, v, (B_NOPE + B_ROPE) ** -0.5)
    return o.transpose(0, 2, 1, 3).reshape(B, S, B_HEADS * B_V) @ w_o


def neighborhood_mixer(h, w_qkv, rpb, w_o):
    B, S, _ = h.shape
    qkv = (h @ w_qkv).reshape(B, S, 3, C_HEADS, HEAD_DIM).transpose(2, 0, 3, 1, 4)
    o = neighborhood_attention(qkv[0], qkv[1], qkv[2], rpb)
    return o.transpose(0, 2, 1, 3).reshape(B, S, C_HEADS * HEAD_DIM) @ w_o


def differential_mixer(h, w_qkv, lq1, lk1, lq2, lk2, subln, w_o, cos, sin, lambda_init):
    B, S, _ = h.shape
    q, k, v = jnp.split(h @ w_qkv, 3, axis=-1)
    q = apply_rope(q.reshape(B, S, 2 * D_HEADS, D_HEAD).transpose(0, 2, 1, 3), cos, sin).reshape(B, D_HEADS, 2, S, D_HEAD)
    k = apply_rope(k.reshape(B, S, 2 * D_HEADS, D_HEAD).transpose(0, 2, 1, 3), cos, sin).reshape(B, D_HEADS, 2, S, D_HEAD)
    v = v.reshape(B, S, D_HEADS, 2 * D_HEAD).transpose(0, 2, 1, 3)
    f32 = jnp.float32
    lam = (jnp.exp(jnp.sum(lq1.astype(f32) * lk1.astype(f32)))
           - jnp.exp(jnp.sum(lq2.astype(f32) * lk2.astype(f32))) + lambda_init)
    o = diff_block_attention(q[:, :, 0], q[:, :, 1], k[:, :, 0], k[:, :, 1], v, lam, D_HEAD ** -0.5)
    o = rmsnorm(o, subln) * (1.0 - lambda_init)
    return o.transpose(0, 2, 1, 3).reshape(B, S, 2 * D_HEADS * D_HEAD) @ w_o


def sq_relu_mlp(h, w_up, w_down):
    return jnp.square(jax.nn.relu(h @ w_up)) @ w_down


def setup_inputs(seed: int = 0) -> dict:
    key = jax.random.key(seed)
    keys = iter(jax.random.split(key, 40))

    def nrm(shape, scale):
        return jax.random.normal(next(keys), shape, jnp.float32) * scale

    def gain(shape):
        return 1.0 + nrm(shape, 0.02)

    nA, nB, nC, nD = (len(range(m, DEPTH, N_MIXERS)) for m in range(N_MIXERS))
    D = D_MODEL
    return {
        "x": nrm((BATCH, SEQ, D), 1.0),
        "p": nrm((DEPTH, BATCH, SEQ, PLE_DIM), 1.0),
        "a_norm": gain((nA, D)),
        "a_w_qkv": nrm((nA, D, 3 * A_HEADS * HEAD_DIM), D ** -0.5),
        "a_w_o": nrm((nA, A_HEADS * HEAD_DIM, D), (A_HEADS * HEAD_DIM) ** -0.5),
        "b_norm": gain((nB, D)),
        "b_w_in": nrm((nB, D, B_Q_RANK + B_KV_RANK + B_ROPE), D ** -0.5),
        "b_q_norm": gain((nB, B_Q_RANK)),
        "b_w_uq": nrm((nB, B_Q_RANK, B_HEADS * (B_NOPE + B_ROPE)), B_Q_RANK ** -0.5),
        "b_kv_norm": gain((nB, B_KV_RANK)),
        "b_w_ukv": nrm((nB, B_KV_RANK, B_HEADS * (B_NOPE + B_V)), B_KV_RANK ** -0.5),
        "b_w_o": nrm((nB, B_HEADS * B_V, D), (B_HEADS * B_V) ** -0.5),
        "c_norm": gain((nC, D)),
        "c_w_qkv": nrm((nC, D, 3 * C_HEADS * HEAD_DIM), D ** -0.5),
        "c_rpb": nrm((nC, C_HEADS, 2 * NA_ROWS - 1, 2 * NA_COLS - 1), 0.02),
        "c_w_o": nrm((nC, C_HEADS * HEAD_DIM, D), (C_HEADS * HEAD_DIM) ** -0.5),
        "d_norm": gain((nD, D)),
        "d_w_qkv": nrm((nD, D, 3 * 2 * D_HEADS * D_HEAD), D ** -0.5),
        "d_lambda_q1": nrm((nD, D_HEAD), 0.1),
        "d_lambda_k1": nrm((nD, D_HEAD), 0.1),
        "d_lambda_q2": nrm((nD, D_HEAD), 0.1),
        "d_lambda_k2": nrm((nD, D_HEAD), 0.1),
        "d_subln": gain((nD, 2 * D_HEAD)),
        "d_w_o": nrm((nD, 2 * D_HEADS * D_HEAD, D), (2 * D_HEADS * D_HEAD) ** -0.5),
        "mlp_norm": gain((DEPTH, D)),
        "w_up": nrm((DEPTH, D, MLP_HIDDEN), D ** -0.5),
        "w_down": nrm((DEPTH, MLP_HIDDEN, D), MLP_HIDDEN ** -0.5),
        "ple_norm": gain((DEPTH, D)),
        "w_ple_gate": nrm((DEPTH, D, D), D ** -0.5),
        "w_ple_proj": nrm((DEPTH, PLE_DIM, D), PLE_DIM ** -0.5),
        "final_norm": gain((D,)),
    }


def reference(x, p, a_norm, a_w_qkv, a_w_o,
              b_norm, b_w_in, b_q_norm, b_w_uq, b_kv_norm, b_w_ukv, b_w_o,
              c_norm, c_w_qkv, c_rpb, c_w_o,
              d_norm, d_w_qkv, d_lambda_q1, d_lambda_k1, d_lambda_q2, d_lambda_k2, d_subln, d_w_o,
              mlp_norm, w_up, w_down, ple_norm, w_ple_gate, w_ple_proj, final_norm):
    S = x.shape[1]
    cos_p, sin_p = rope_tables(S, ROT_DIM)
    cos_l, sin_l = rope_tables(S, B_ROPE)
    for i in range(DEPTH):
        mix, j = i % N_MIXERS, i // N_MIXERS
        if mix == 0:
            x = x + dilated_window_mixer(rmsnorm(x, a_norm[j]), a_w_qkv[j], a_w_o[j], cos_p, sin_p)
        elif mix == 1:
            x = x + latent_attention_mixer(rmsnorm(x, b_norm[j]), b_w_in[j], b_q_norm[j], b_w_uq[j],
                                           b_kv_norm[j], b_w_ukv[j], b_w_o[j], cos_l, sin_l)
        elif mix == 2:
            x = x + neighborhood_mixer(rmsnorm(x, c_norm[j]), c_w_qkv[j], c_rpb[j], c_w_o[j])
        else:
            lambda_init = 0.8 - 0.6 * math.exp(-0.3 * i)
            x = x + differential_mixer(rmsnorm(x, d_norm[j]), d_w_qkv[j], d_lambda_q1[j], d_lambda_k1[j],
                                       d_lambda_q2[j], d_lambda_k2[j], d_subln[j], d_w_o[j],
                                       cos_p, sin_p, lambda_init)
        x = x + sq_relu_mlp(rmsnorm(x, mlp_norm[i]), w_up[i], w_down[i])
        gate = jax.nn.sigmoid(rmsnorm(x, ple_norm[i]) @ w_ple_gate[i])
        x = x + gate * (p[i] @ w_ple_proj[i])
    return rmsnorm(x, final_norm)
```

```python
import functools
import math

import numpy as np
import jax
import jax.numpy as jnp
from jax import lax
from jax.experimental import pallas as pl
from jax.experimental.pallas import tpu as pltpu

F32 = jnp.float32
BF16 = jnp.bfloat16

D_MODEL = 1024
BATCH = 8
SEQ = 2048
DEPTH = 4
TOKENS = BATCH * SEQ
HEAD_DIM = 64
ROPE_THETA = 500000.0
ROT_DIM = HEAD_DIM // 4
NEG_INF = -1e30
RMS_EPS = 1e-6

DIL_PAIRS = ((128, 1), (512, 4), (2048, 16))
A_GROUP_HEADS = 5
A_HEADS = A_GROUP_HEADS * len(DIL_PAIRS)
A_BAND_HALF = 64
A_GROUP_LANES = 384
A_GROUP_CHUNKS = A_GROUP_LANES // 128

B_HEADS = 16
B_Q_RANK = 256
B_KV_RANK = 128
B_NOPE = 64
B_ROPE = 32
B_V = 64

C_HEADS = 16
GRID_W = 64
GRID_ROWS = SEQ // GRID_W
NA_ROWS = 8
NA_COLS = 16
C_ROW_GROUP = 4
C_WIN_ROWS = 12

D_HEADS = 8
D_HEAD = 64

MLP_HIDDEN = 4 * D_MODEL
PLE_DIM = 256

LANES = 128
VMEM_LIMIT = 48 * 1024 * 1024

PROJ_TM = 512
MLP_TM = 1024
MLP_TH = 512
ATTN_TQ = 256
BAND_TQ = 128


def _params(*sem):
    return pltpu.CompilerParams(dimension_semantics=sem, vmem_limit_bytes=VMEM_LIMIT)


def _rms_f32(x, g):
    ms = jnp.mean(x * x, axis=-1, keepdims=True)
    return x * lax.rsqrt(ms + RMS_EPS) * g


def _rope_chunk(y, c, sa, sb, shift):
    return (y * c + pltpu.roll(y, LANES - shift, 1) * sa + pltpu.roll(y, shift, 1) * sb)


def _softmax_parts(s):
    m = jnp.max(s, axis=-1, keepdims=True)
    e = jnp.exp(s - m)
    l = jnp.sum(e, axis=-1, keepdims=True)
    return m, e, l


def _dot(a, b):
    return jnp.dot(a, b, preferred_element_type=F32)


def _dot_nt(a, b):
    return lax.dot_general(a, b, (((1,), (1,)), ((), ())), preferred_element_type=F32)


def _lane_iota():
    return lax.broadcasted_iota(jnp.int32, (1, LANES), 1)


def _norm_proj_kernel(x_ref, g_ref, w_ref, c_ref, sa_ref, sb_ref, o_ref, *, n_rope, tn):
    xn = _rms_f32(x_ref[...], g_ref[...]).astype(BF16)
    n = o_ref.shape[1]
    c, sa, sb = c_ref[...], sa_ref[...], sb_ref[...]
    for j in range(n // tn):
        y = _dot(xn, w_ref[:, j * tn:(j + 1) * tn])
        for i in range(tn // LANES):
            col = j * tn + i * LANES
            chunk = y[:, i * LANES:(i + 1) * LANES]
            if col < n_rope:
                chunk = _rope_chunk(chunk, c, sa, sb, ROT_DIM // 2)
            o_ref[:, col:col + LANES] = chunk.astype(BF16)


def _norm_proj(x, g, w, tabs, n_rope, name):
    n = w.shape[1]
    tm = PROJ_TM
    nb = SEQ // tm
    tab_spec = pl.BlockSpec((tm, LANES), lambda i: (i % nb, 0))
    return pl.pallas_call(
        functools.partial(_norm_proj_kernel, n_rope=n_rope, tn=512),
        grid=(TOKENS // tm,),
        in_specs=[
            pl.BlockSpec((tm, D_MODEL), lambda i: (i, 0)),
            pl.BlockSpec((1, D_MODEL), lambda i: (0, 0)),
            pl.BlockSpec((D_MODEL, n), lambda i: (0, 0)),
            tab_spec, tab_spec, tab_spec,
        ],
        out_specs=pl.BlockSpec((tm, n), lambda i: (i, 0)),
        out_shape=jax.ShapeDtypeStruct((TOKENS, n), BF16),
        compiler_params=_params("parallel"),
        name=name,
    )(x, g, w, *tabs)


def _proj_res_kernel(x_ref, o_ref, w_ref, out_ref):
    out_ref[...] = x_ref[...] + _dot(o_ref[...], w_ref[...])


def _proj_res(x, o, w, name):
    tm = PROJ_TM
    k = o.shape[1]
    return pl.pallas_call(
        _proj_res_kernel,
        grid=(TOKENS // tm,),
        in_specs=[
            pl.BlockSpec((tm, D_MODEL), lambda i: (i, 0)),
            pl.BlockSpec((tm, k), lambda i: (i, 0)),
            pl.BlockSpec((k, D_MODEL), lambda i: (0, 0)),
        ],
        out_specs=pl.BlockSpec((tm, D_MODEL), lambda i: (i, 0)),
        out_shape=jax.ShapeDtypeStruct((TOKENS, D_MODEL), F32),
        compiler_params=_params("parallel"),
        name=name,
    )(x, o, w)


def _mlp_kernel(x_ref, g_ref, wu_ref, wd_ref, o_ref, xn_ref):
    j = pl.program_id(1)

    @pl.when(j == 0)
    def _():
        x = x_ref[...]
        xn_ref[...] = _rms_f32(x, g_ref[...]).astype(BF16)
        o_ref[...] = x

    h = _dot(xn_ref[...], wu_ref[...])
    h = jnp.square(jnp.maximum(h, 0.0)).astype(BF16)
    o_ref[...] += _dot(h, wd_ref[...])


def _mlp(x, g, wu, wd):
    tm, th = MLP_TM, MLP_TH
    return pl.pallas_call(
        _mlp_kernel,
        grid=(TOKENS // tm, MLP_HIDDEN // th),
        in_specs=[
            pl.BlockSpec((tm, D_MODEL), lambda i, j: (i, 0)),
            pl.BlockSpec((1, D_MODEL), lambda i, j: (0, 0)),
            pl.BlockSpec((D_MODEL, th), lambda i, j: (0, j)),
            pl.BlockSpec((th, D_MODEL), lambda i, j: (j, 0)),
        ],
        out_specs=pl.BlockSpec((tm, D_MODEL), lambda i, j: (i, 0)),
        out_shape=jax.ShapeDtypeStruct((TOKENS, D_MODEL), F32),
        scratch_shapes=[pltpu.VMEM((tm, D_MODEL), BF16)],
        compiler_params=_params("parallel", "arbitrary"),
        name="mlp",
    )(x, g, wu, wd)


def _ple_kernel(x_ref, g_ref, wg_ref, p_ref, wp_ref, fg_ref, o_ref, *, final):
    x = x_ref[...]
    xn = _rms_f32(x, g_ref[...]).astype(BF16)
    gate = jax.nn.sigmoid(_dot(xn, wg_ref[...]))
    proj = _dot(p_ref[...].astype(BF16), wp_ref[...])
    y = x + gate * proj
    if final:
        y = _rms_f32(y, fg_ref[...])
    o_ref[...] = y


def _ple(x, g, wg, p, wp, fg, final):
    tm = PROJ_TM
    return pl.pallas_call(
        functools.partial(_ple_kernel, final=final),
        grid=(TOKENS // tm,),
        in_specs=[
            pl.BlockSpec((tm, D_MODEL), lambda i: (i, 0)),
            pl.BlockSpec((1, D_MODEL), lambda i: (0, 0)),
            pl.BlockSpec((D_MODEL, D_MODEL), lambda i: (0, 0)),
            pl.BlockSpec((tm, PLE_DIM), lambda i: (i, 0)),
            pl.BlockSpec((PLE_DIM, D_MODEL), lambda i: (0, 0)),
            pl.BlockSpec((1, D_MODEL), lambda i: (0, 0)),
        ],
        out_specs=pl.BlockSpec((tm, D_MODEL), lambda i: (i, 0)),
        out_shape=jax.ShapeDtypeStruct((TOKENS, D_MODEL), F32),
        compiler_params=_params("parallel"),
        name="ple_final" if final else "ple",
    )(x, g, wg, p, wp, fg)


def _a_qkv_kernel(x_ref, g_ref, w_ref, c_ref, sa_ref, sb_ref, *refs):
    outs, stage = refs[:9], refs[9]
    tm = x_ref.shape[0]
    gl = A_GROUP_LANES
    nc = A_GROUP_CHUNKS
    xn = _rms_f32(x_ref[...], g_ref[...]).astype(BF16)
    c, sa, sb = c_ref[...], sa_ref[...], sb_ref[...]

    def emit(dil, dsts):
        n = tm // dil
        for k, dst in enumerate(dsts):
            for ci in range(nc):
                for r in range(dil):
                    if dil == 1:
                        rows = stage[k * nc + ci]
                    else:
                        rows = stage[k * nc + ci, pl.ds(r, n, stride=dil), :]
                    dst[0, r, ci] = rows.astype(BF16)

    for g, (_, dil) in enumerate(DIL_PAIRS):
        y = _dot(xn, w_ref[:, 2 * gl * g:2 * gl * (g + 1)])
        for i in range(2 * nc):
            stage[i] = _rope_chunk(y[:, i * LANES:(i + 1) * LANES], c, sa, sb, ROT_DIM // 2)
        emit(dil, (outs[3 * g], outs[3 * g + 1]))
    for g, (_, dil) in enumerate(DIL_PAIRS):
        y = _dot(xn, w_ref[:, 6 * gl + gl * g:6 * gl + gl * (g + 1)])
        for i in range(nc):
            stage[i] = y[:, i * LANES:(i + 1) * LANES]
        emit(dil, (outs[3 * g + 2],))


def _a_qkv(x, g, w, tabs):
    tm = PROJ_TM
    nb = SEQ // tm
    gl = A_GROUP_LANES
    tab_spec = pl.BlockSpec((tm, LANES), lambda i: (i % nb, 0))
    out_shapes, out_specs = [], []
    nc = A_GROUP_CHUNKS
    for _, dil in DIL_PAIRS:
        for _ in range(3):
            out_shapes.append(jax.ShapeDtypeStruct((BATCH, dil, nc, SEQ // dil, LANES), BF16))
            out_specs.append(pl.BlockSpec((1, dil, nc, tm // dil, LANES),
                                          lambda i: (i // nb, 0, 0, i % nb, 0)))
    return pl.pallas_call(
        _a_qkv_kernel,
        grid=(TOKENS // tm,),
        in_specs=[
            pl.BlockSpec((tm, D_MODEL), lambda i: (i, 0)),
            pl.BlockSpec((1, D_MODEL), lambda i: (0, 0)),
            pl.BlockSpec((D_MODEL, 9 * gl), lambda i: (0, 0)),
            tab_spec, tab_spec, tab_spec,
        ],
        out_specs=out_specs,
        out_shape=out_shapes,
        scratch_shapes=[pltpu.VMEM((2 * nc, tm, LANES), F32)],
        compiler_params=_params("parallel"),
        name="a_qkv",
    )(x, g, w, *tabs)


def _a_attn_kernel(q_ref, k_ref, v_ref, o_ref, lse_ref, *, seg, dil):
    r = pl.program_id(1)
    tq = BAND_TQ
    kw = min(seg, tq + 2 * A_BAND_HALF)
    lane = _lane_iota()
    low = lane < HEAD_DIM

    def tile(t, carry):
        q0 = pl.multiple_of(t * tq, tq)
        ws = pl.multiple_of(jnp.clip(q0 - A_BAND_HALF, 0, seg - kw), A_BAND_HALF)
        rows = q0 + lax.broadcasted_iota(jnp.int32, (tq, 1), 0)
        cols = ws + lax.broadcasted_iota(jnp.int32, (1, kw), 1)
        valid = jnp.abs(cols - rows) <= A_BAND_HALF
        for c in range(A_GROUP_CHUNKS):
            qc = q_ref[0, 0, c, pl.ds(q0, tq), :] * jnp.asarray(HEAD_DIM ** -0.5, BF16)
            kc = k_ref[0, 0, c, pl.ds(ws, kw), :]
            vc = v_ref[0, 0, c, pl.ds(ws, kw), :]
            outs, lses = [], []
            n_heads = 2 if 2 * c + 1 < A_GROUP_HEADS else 1
            for hh in range(n_heads):
                qm = jnp.where(low if hh == 0 else jnp.logical_not(low), qc, jnp.zeros_like(qc))
                s = jnp.where(valid, _dot_nt(qm, kc), NEG_INF)
                m, e, l = _softmax_parts(s)
                outs.append(_dot(e.astype(BF16), vc) * (1.0 / l))
                lses.append(m + jnp.log(l))
            if n_heads == 2:
                o = jnp.where(low, outs[0], outs[1])
                ls = jnp.where(low, lses[0], lses[1])
            else:
                o = jnp.where(low, outs[0], 0.0)
                ls = jnp.where(low, lses[0], 0.0)
            if dil == 1:
                dst = pl.ds(q0, tq)
            else:
                dst = pl.ds(r + dil * q0, tq, stride=dil)
            o_ref[0, c, dst, :] = o
            lse_ref[0, c, dst, :] = jnp.broadcast_to(ls, o.shape)
        return carry

    lax.fori_loop(0, seg // tq, tile, 0)


def _a_attn(q, k, v, dil):
    seg = SEQ // dil
    nc = A_GROUP_CHUNKS
    in_spec = pl.BlockSpec((1, 1, nc, seg, LANES), lambda b, r: (b, r, 0, 0, 0))
    out_spec = pl.BlockSpec((1, nc, SEQ, LANES), lambda b, r: (b, 0, 0, 0))
    out_shape = jax.ShapeDtypeStruct((BATCH, nc, SEQ, LANES), F32)
    return pl.pallas_call(
        functools.partial(_a_attn_kernel, seg=seg, dil=dil),
        grid=(BATCH, dil),
        in_specs=[in_spec, in_spec, in_spec],
        out_specs=[out_spec, out_spec],
        out_shape=[out_shape, out_shape],
        compiler_params=_params("parallel", "arbitrary"),
        name=f"a_attn_d{dil}",
    )(q, k, v)


def _a_out_kernel(x_ref, o0, o1, o2, l0, l1, l2, w_ref, out_ref):
    cat = lambda ref: jnp.concatenate([ref[0, c] for c in range(A_GROUP_CHUNKS)], axis=1)
    ls = [cat(l0), cat(l1), cat(l2)]
    m = jnp.maximum(jnp.maximum(ls[0], ls[1]), ls[2])
    es = [jnp.exp(l - m) for l in ls]
    inv = 1.0 / (es[0] + es[1] + es[2])
    acc = x_ref[...]
    for g, o in enumerate((o0, o1, o2)):
        acc = acc + _dot((cat(o) * (es[g] * inv)).astype(BF16), w_ref[g])
    out_ref[...] = acc


def _a_out(x, os_, ls_, w):
    tm = PROJ_TM
    nb = SEQ // tm
    gl = A_GROUP_LANES
    part = pl.BlockSpec((1, A_GROUP_CHUNKS, tm, LANES), lambda i: (i // nb, 0, i % nb, 0))
    return pl.pallas_call(
        _a_out_kernel,
        grid=(TOKENS // tm,),
        in_specs=[pl.BlockSpec((tm, D_MODEL), lambda i: (i, 0))] + [part] * 6 + [
            pl.BlockSpec((3, gl, D_MODEL), lambda i: (0, 0, 0))],
        out_specs=pl.BlockSpec((tm, D_MODEL), lambda i: (i, 0)),
        out_shape=jax.ShapeDtypeStruct((TOKENS, D_MODEL), F32),
        compiler_params=_params("parallel"),
        name="a_out",
    )(x, *os_, *ls_, w)


def _b_proj_kernel(x_ref, g_ref, win_ref, qn_ref, kvn_ref, wuq_ref, wuk_ref, wuv_ref,
                   c_ref, sa_ref, sb_ref, q_out, k_out, v_out):
    c, sa, sb = c_ref[...], sa_ref[...], sb_ref[...]
    xn = _rms_f32(x_ref[...], g_ref[...]).astype(BF16)
    z = _dot(xn, win_ref[...])
    cq = _rms_f32(z[:, :B_Q_RANK], qn_ref[...]).astype(BF16)
    ckv = _rms_f32(z[:, B_Q_RANK:B_Q_RANK + B_KV_RANK], kvn_ref[...]).astype(BF16)
    k_rope = _rope_chunk(z[:, B_Q_RANK + B_KV_RANK:], c, sa, sb, B_ROPE // 2)
    q = _dot(cq, wuq_ref[...])
    k = _dot(ckv, wuk_ref[...])
    for h in range(B_HEADS):
        sl = slice(h * LANES, (h + 1) * LANES)
        q_out[:, sl] = _rope_chunk(q[:, sl], c, sa, sb, B_ROPE // 2).astype(BF16)
        k_out[:, sl] = (k[:, sl] + k_rope).astype(BF16)
    v_out[...] = _dot(ckv, wuv_ref[...]).astype(BF16)


def _b_proj(x, g, win, qn, kvn, wuq, wuk, wuv, tabs):
    tm = PROJ_TM
    nb = SEQ // tm
    tab_spec = pl.BlockSpec((tm, LANES), lambda i: (i % nb, 0))

    def full(a):
        return pl.BlockSpec(a.shape, lambda i: (0,) * a.ndim)

    widths = (B_HEADS * LANES, B_HEADS * LANES, B_HEADS * B_V)
    return pl.pallas_call(
        _b_proj_kernel,
        grid=(TOKENS // tm,),
        in_specs=[pl.BlockSpec((tm, D_MODEL), lambda i: (i, 0)), full(g), full(win), full(qn),
                  full(kvn), full(wuq), full(wuk), full(wuv), tab_spec, tab_spec, tab_spec],
        out_specs=[pl.BlockSpec((tm, n), lambda i: (i, 0)) for n in widths],
        out_shape=[jax.ShapeDtypeStruct((TOKENS, n), BF16) for n in widths],
        compiler_params=_params("parallel"),
        name="b_proj",
    )(x, g, win, qn, kvn, wuq, wuk, wuv, *tabs)


def _b_attn_kernel(q_ref, k_ref, v_ref, o_ref, *, scale):
    low = _lane_iota() < HEAD_DIM
    v = v_ref[...]
    outs = []
    for hh in range(2):
        sl = slice(hh * LANES, (hh + 1) * LANES)
        s = _dot_nt(q_ref[:, sl], k_ref[:, sl]) * scale
        _, e, l = _softmax_parts(s)
        outs.append(_dot(e.astype(BF16), v) * (1.0 / l))
    o_ref[...] = jnp.where(low, outs[0], outs[1]).astype(BF16)


def _b_attn(q, k, v):
    tq = ATTN_TQ
    nq = SEQ // tq
    return pl.pallas_call(
        functools.partial(_b_attn_kernel, scale=(B_NOPE + B_ROPE) ** -0.5),
        grid=(BATCH, B_HEADS // 2, nq),
        in_specs=[
            pl.BlockSpec((tq, 2 * LANES), lambda b, h, i: (b * nq + i, h)),
            pl.BlockSpec((SEQ, 2 * LANES), lambda b, h, i: (b, h)),
            pl.BlockSpec((SEQ, LANES), lambda b, h, i: (b, h)),
        ],
        out_specs=pl.BlockSpec((tq, LANES), lambda b, h, i: (b * nq + i, h)),
        out_shape=jax.ShapeDtypeStruct((TOKENS, B_HEADS * B_V), BF16),
        compiler_params=_params("parallel", "parallel", "arbitrary"),
        name="b_attn",
    )(q, k, v)


def _c_attn_kernel(q_ref, k_ref, v_ref, bias_ref, o_ref):
    low = _lane_iota() < HEAD_DIM
    nq = C_ROW_GROUP * GRID_W
    nk = C_WIN_ROWS * GRID_W
    n_groups = GRID_ROWS // C_ROW_GROUP

    def group(gi, carry):
        q0 = pl.multiple_of(gi * nq, nq)
        wrow = jnp.clip(gi * C_ROW_GROUP - NA_ROWS // 2, 0, GRID_ROWS - C_WIN_ROWS)
        ws = pl.multiple_of(wrow * GRID_W, GRID_W)
        case = jnp.where(gi == 0, 0, jnp.where(gi == n_groups - 1, 2, 1))
        qc = q_ref[pl.ds(q0, nq), :] * jnp.asarray(HEAD_DIM ** -0.5, BF16)
        kc = k_ref[pl.ds(ws, nk), :]
        vc = v_ref[pl.ds(ws, nk), :]
        outs = []
        for hh in range(2):
            qm = jnp.where(low if hh == 0 else jnp.logical_not(low), qc, jnp.zeros_like(qc))
            s = _dot_nt(qm, kc) + bias_ref[hh, case]
            _, e, l = _softmax_parts(s)
            outs.append(_dot(e.astype(BF16), vc) * (1.0 / l))
        o_ref[pl.ds(q0, nq), :] = jnp.where(low, outs[0], outs[1]).astype(BF16)
        return carry

    lax.fori_loop(0, n_groups, group, 0)


def _c_attn(qkv, bias):
    nq = C_ROW_GROUP * GRID_W
    nk = C_WIN_ROWS * GRID_W
    npair = C_HEADS // 2
    return pl.pallas_call(
        _c_attn_kernel,
        grid=(npair, BATCH),
        in_specs=[
            pl.BlockSpec((SEQ, LANES), lambda h, b: (b, h)),
            pl.BlockSpec((SEQ, LANES), lambda h, b: (b, npair + h)),
            pl.BlockSpec((SEQ, LANES), lambda h, b: (b, 2 * npair + h)),
            pl.BlockSpec((2, 3, nq, nk), lambda h, b: (h, 0, 0, 0)),
        ],
        out_specs=pl.BlockSpec((SEQ, LANES), lambda h, b: (b, h)),
        out_shape=jax.ShapeDtypeStruct((TOKENS, C_HEADS * HEAD_DIM), BF16),
        compiler_params=_params("parallel", "arbitrary"),
        name="c_attn",
    )(qkv, qkv, qkv, bias)


def _c_bias_table(rpb):
    nq = C_ROW_GROUP * GRID_W
    nk = C_WIN_ROWS * GRID_W
    qi, qc = np.divmod(np.arange(nq), GRID_W)
    kr, kc = np.divmod(np.arange(nk), GRID_W)
    win0 = np.clip(qc - NA_COLS // 2, 0, GRID_W - NA_COLS)
    col_ok = (kc[None, :] >= win0[:, None]) & (kc[None, :] < win0[:, None] + NA_COLS)
    col_off = np.clip(kc[None, :] - qc[:, None] + NA_COLS - 1, 0, 2 * NA_COLS - 2)
    tables = []
    for first_row in (0, C_ROW_GROUP, GRID_ROWS - C_ROW_GROUP):
        wrow = int(np.clip(first_row - NA_ROWS // 2, 0, GRID_ROWS - C_WIN_ROWS))
        qrow = first_row + qi
        rs = np.clip(qrow - NA_ROWS // 2, 0, GRID_ROWS - NA_ROWS)
        krow = wrow + kr
        row_ok = (krow[None, :] >= rs[:, None]) & (krow[None, :] < rs[:, None] + NA_ROWS)
        row_off = np.clip(krow[None, :] - qrow[:, None] + NA_ROWS - 1, 0, 2 * NA_ROWS - 2)
        vals = rpb[:, row_off, col_off]
        tables.append(jnp.where(row_ok & col_ok, vals, NEG_INF))
    return jnp.stack(tables, axis=1)


def _d_attn_kernel(lq1, lk1, lq2, lk2, q_ref, k_ref, v_ref, sub_ref, o_ref, *, scale, lambda_init):
    lam = (jnp.exp(jnp.sum(lq1[...] * lk1[...], axis=-1, keepdims=True))
           - jnp.exp(jnp.sum(lq2[...] * lk2[...], axis=-1, keepdims=True)) + lambda_init)
    low = _lane_iota() < D_HEAD
    q = q_ref[...]
    k = k_ref[...]
    zero = jnp.zeros_like(q)
    _, e1, l1 = _softmax_parts(_dot_nt(jnp.where(low, q, zero), k) * scale)
    _, e2, l2 = _softmax_parts(_dot_nt(jnp.where(low, zero, q), k) * scale)
    w = e1 * (1.0 / l1) - lam * (e2 * (1.0 / l2))
    o = _dot(w.astype(BF16), v_ref[...])
    o_ref[...] = (_rms_f32(o, sub_ref[...]) * (1.0 - lambda_init)).astype(BF16)


def _d_attn(qkv, lq1, lk1, lq2, lk2, subln, lambda_init):
    tq = ATTN_TQ
    nq = SEQ // tq
    vec = pl.BlockSpec((1, D_HEAD), lambda b, h, i: (0, 0))
    return pl.pallas_call(
        functools.partial(_d_attn_kernel, scale=D_HEAD ** -0.5, lambda_init=lambda_init),
        grid=(BATCH, D_HEADS, nq),
        in_specs=[
            vec, vec, vec, vec,
            pl.BlockSpec((tq, LANES), lambda b, h, i: (b * nq + i, h)),
            pl.BlockSpec((SEQ, LANES), lambda b, h, i: (b, D_HEADS + h)),
            pl.BlockSpec((SEQ, LANES), lambda b, h, i: (b, 2 * D_HEADS + h)),
            pl.BlockSpec((1, 2 * D_HEAD), lambda b, h, i: (0, 0)),
        ],
        out_specs=pl.BlockSpec((tq, LANES), lambda b, h, i: (b * nq + i, h)),
        out_shape=jax.ShapeDtypeStruct((TOKENS, 2 * D_HEADS * D_HEAD), BF16),
        compiler_params=_params("parallel", "parallel", "arbitrary"),
        name="d_attn",
    )(lq1, lk1, lq2, lk2, qkv, qkv, qkv, subln)


def _rope_tables(rot_dim, period, lane0):
    r = rot_dim // 2
    inv = ROPE_THETA ** (-jnp.arange(0, rot_dim, 2, dtype=F32) / rot_dim)
    ang = jnp.arange(SEQ, dtype=F32)[:, None] * inv[None, :]
    cos, sin = jnp.cos(ang), jnp.sin(ang)
    lane = np.arange(LANES) % period - lane0
    first = (lane >= 0) & (lane < r)
    second = (lane >= r) & (lane < 2 * r)
    idx = np.where(first, lane, np.where(second, lane - r, 0))
    cg, sg = cos[:, idx], sin[:, idx]
    c = jnp.where(first | second, cg, 1.0)
    sa = jnp.where(first, -sg, 0.0)
    sb = jnp.where(second, sg, 0.0)
    return c, sa, sb


def _a_weights(w_qkv, w_o):
    nh = A_GROUP_HEADS * HEAD_DIM
    pad = jnp.zeros((D_MODEL, A_GROUP_LANES - nh), w_qkv.dtype)

    def cols(part, g):
        base = part * A_HEADS * HEAD_DIM + g * nh
        return [w_qkv[:, base:base + nh], pad]

    pieces = []
    for g in range(len(DIL_PAIRS)):
        pieces += cols(0, g) + cols(1, g)
    for g in range(len(DIL_PAIRS)):
        pieces += cols(2, g)
    w = jnp.concatenate(pieces, axis=1).astype(BF16)
    wo = w_o.reshape(len(DIL_PAIRS), nh, D_MODEL)
    wo = jnp.pad(wo, ((0, 0), (0, A_GROUP_LANES - nh), (0, 0))).astype(BF16)
    return w, wo


def _b_weights(w_in, w_uq, w_ukv):
    split = B_Q_RANK + B_KV_RANK
    win = jnp.concatenate([
        w_in[:, :split], jnp.zeros((D_MODEL, B_NOPE), w_in.dtype), w_in[:, split:],
        jnp.zeros((D_MODEL, LANES - B_NOPE - B_ROPE), w_in.dtype)], axis=1).astype(BF16)
    wuq = w_uq.reshape(B_Q_RANK, B_HEADS, B_NOPE + B_ROPE)
    wuq = jnp.pad(wuq, ((0, 0), (0, 0), (0, LANES - B_NOPE - B_ROPE)))
    wuq = wuq.reshape(B_Q_RANK, B_HEADS * LANES).astype(BF16)
    wukv = w_ukv.reshape(B_KV_RANK, B_HEADS, B_NOPE + B_V)
    wuk = jnp.pad(wukv[:, :, :B_NOPE], ((0, 0), (0, 0), (0, LANES - B_NOPE)))
    wuk = wuk.reshape(B_KV_RANK, B_HEADS * LANES).astype(BF16)
    wuv = wukv[:, :, B_NOPE:].reshape(B_KV_RANK, B_HEADS * B_V).astype(BF16)
    return win, wuq, wuk, wuv


def kernel(x, p, a_norm, a_w_qkv, a_w_o, b_norm, b_w_in, b_q_norm, b_w_uq, b_kv_norm, b_w_ukv, b_w_o, c_norm, c_w_qkv, c_rpb, c_w_o, d_norm, d_w_qkv, d_lambda_q1, d_lambda_k1, d_lambda_q2, d_lambda_k2, d_subln, d_w_o, mlp_norm, w_up, w_down, ple_norm, w_ple_gate, w_ple_proj, final_norm):
    tabs_p = _rope_tables(ROT_DIM, HEAD_DIM, 0)
    tabs_l = _rope_tables(B_ROPE, LANES, B_NOPE)
    h = x.reshape(TOKENS, D_MODEL)
    row = lambda v: v.reshape(1, -1)

    for i in range(DEPTH):
        if i == 0:
            w, wo = _a_weights(a_w_qkv[0], a_w_o[0])
            parts = _a_qkv(h, row(a_norm[0]), w, tabs_p)
            os_, ls_ = [], []
            for g, (_, dil) in enumerate(DIL_PAIRS):
                o, lse = _a_attn(parts[3 * g], parts[3 * g + 1], parts[3 * g + 2], dil)
                os_.append(o)
                ls_.append(lse)
            h = _a_out(h, os_, ls_, wo)
        elif i == 1:
            win, wuq, wuk, wuv = _b_weights(b_w_in[0], b_w_uq[0], b_w_ukv[0])
            q, k, v = _b_proj(h, row(b_norm[0]), win, row(b_q_norm[0]), row(b_kv_norm[0]),
                              wuq, wuk, wuv, tabs_l)
            h = _proj_res(h, _b_attn(q, k, v), b_w_o[0].astype(BF16), "b_out")
        elif i == 2:
            qkv = _norm_proj(h, row(c_norm[0]), c_w_qkv[0].astype(BF16), tabs_p, 0, "c_qkv")
            o = _c_attn(qkv, _c_bias_table(c_rpb[0]))
            h = _proj_res(h, o, c_w_o[0].astype(BF16), "c_out")
        else:
            lambda_init = 0.8 - 0.6 * math.exp(-0.3 * i)
            qkv = _norm_proj(h, row(d_norm[0]), d_w_qkv[0].astype(BF16), tabs_p,
                             2 * 2 * D_HEADS * D_HEAD, "d_qkv")
            o = _d_attn(qkv, row(d_lambda_q1[0]), row(d_lambda_k1[0]), row(d_lambda_q2[0]),
                        row(d_lambda_k2[0]), row(d_subln[0]), lambda_init)
            h = _proj_res(h, o, d_w_o[0].astype(BF16), "d_out")
        h = _mlp(h, row(mlp_norm[i]), w_up[i].astype(BF16), w_down[i].astype(BF16))
        h = _ple(h, row(ple_norm[i]), w_ple_gate[i].astype(BF16), p[i].reshape(TOKENS, PLE_DIM),
                 w_ple_proj[i].astype(BF16), row(final_norm), i == DEPTH - 1)
    return h.reshape(BATCH, SEQ, D_MODEL)
```

```python
import functools
import math

import numpy as np
import jax
import jax.numpy as jnp
from jax import lax
from jax.experimental import pallas as pl
from jax.experimental.pallas import tpu as pltpu

F32 = jnp.float32
BF16 = jnp.bfloat16

D_MODEL = 1024
BATCH = 8
SEQ = 2048
DEPTH = 4
TOKENS = BATCH * SEQ
HEAD_DIM = 64
ROPE_THETA = 500000.0
ROT_DIM = HEAD_DIM // 4
NEG_INF = -1e30
RMS_EPS = 1e-6
LOG2_E = math.log2(math.e)

DIL_PAIRS = ((128, 1), (512, 4), (2048, 16))
A_GROUP_HEADS = 5
A_HEADS = A_GROUP_HEADS * len(DIL_PAIRS)
A_BAND_HALF = 64
A_GROUP_LANES = 384
A_GROUP_CHUNKS = A_GROUP_LANES // 128

B_HEADS = 16
B_Q_RANK = 256
B_KV_RANK = 128
B_NOPE = 64
B_ROPE = 32
B_V = 64

C_HEADS = 16
GRID_W = 64
GRID_ROWS = SEQ // GRID_W
NA_ROWS = 8
NA_COLS = 16
C_ROW_GROUP = 4
C_WIN_ROWS = 12

D_HEADS = 8
D_HEAD = 64

MLP_HIDDEN = 4 * D_MODEL
PLE_DIM = 256

LANES = 128
VMEM_LIMIT = 48 * 1024 * 1024

PROJ_TM = 512
MLP_TM = 1024
MLP_TH = 512
ATTN_TQ = 512
ATTN_KC = 512
BAND_TQ = 128


def _params(*sem):
    return pltpu.CompilerParams(dimension_semantics=sem, vmem_limit_bytes=VMEM_LIMIT)


def _rms_f32(x, g):
    ms = jnp.mean(x * x, axis=-1, keepdims=True)
    return x * lax.rsqrt(ms + RMS_EPS) * g


def _rope_chunk(y, c, sa, sb, shift):
    return (y * c + pltpu.roll(y, LANES - shift, 1) * sa + pltpu.roll(y, shift, 1) * sb)


def _softmax_parts(s):
    m = jnp.max(s, axis=-1, keepdims=True)
    e = jnp.exp(s - m)
    l = jnp.sum(e, axis=-1, keepdims=True)
    return m, e, l


def _dot(a, b):
    return jnp.dot(a, b, preferred_element_type=F32)


def _dot_nt(a, b):
    return lax.dot_general(a, b, (((1,), (1,)), ((), ())), preferred_element_type=F32)


def _lane_iota():
    return lax.broadcasted_iota(jnp.int32, (1, LANES), 1)


def _norm_proj_kernel(x_ref, g_ref, w_ref, wvt_ref, c_ref, sa_ref, sb_ref, qk_ref, vt_ref,
                      *, rope, tn):
    xn = _rms_f32(x_ref[...], g_ref[...]).astype(BF16)
    n = qk_ref.shape[1]
    c, sa, sb = c_ref[...], sa_ref[...], sb_ref[...]
    for j in range(n // tn):
        y = _dot(xn, w_ref[:, j * tn:(j + 1) * tn])
        for i in range(tn // LANES):
            col = j * tn + i * LANES
            chunk = y[:, i * LANES:(i + 1) * LANES]
            if rope:
                chunk = _rope_chunk(chunk, c, sa, sb, ROT_DIM // 2)
            qk_ref[:, col:col + LANES] = chunk.astype(BF16)
    vt_ref[...] = _dot_nt(wvt_ref[...], xn).astype(BF16)


def _norm_proj(x, g, w, wvt, tabs, rope, name):
    n = w.shape[1]
    nv = wvt.shape[0]
    tm = PROJ_TM
    nb = SEQ // tm
    tab_spec = pl.BlockSpec((tm, LANES), lambda i: (i % nb, 0))
    return pl.pallas_call(
        functools.partial(_norm_proj_kernel, rope=rope, tn=512),
        grid=(TOKENS // tm,),
        in_specs=[
            pl.BlockSpec((tm, D_MODEL), lambda i: (i, 0)),
            pl.BlockSpec((1, D_MODEL), lambda i: (0, 0)),
            pl.BlockSpec((D_MODEL, n), lambda i: (0, 0)),
            pl.BlockSpec((nv, D_MODEL), lambda i: (0, 0)),
            tab_spec, tab_spec, tab_spec,
        ],
        out_specs=[pl.BlockSpec((tm, n), lambda i: (i, 0)),
                   pl.BlockSpec((nv, tm), lambda i: (0, i))],
        out_shape=[jax.ShapeDtypeStruct((TOKENS, n), BF16),
                   jax.ShapeDtypeStruct((nv, TOKENS), BF16)],
        compiler_params=_params("parallel"),
        name=name,
    )(x, g, w, wvt, *tabs)


def _proj_res_kernel(x_ref, o_ref, w_ref, out_ref):
    out_ref[...] = x_ref[...] + _dot(o_ref[...], w_ref[...])


def _proj_res(x, o, w, name):
    tm = PROJ_TM
    k = o.shape[1]
    return pl.pallas_call(
        _proj_res_kernel,
        grid=(TOKENS // tm,),
        in_specs=[
            pl.BlockSpec((tm, D_MODEL), lambda i: (i, 0)),
            pl.BlockSpec((tm, k), lambda i: (i, 0)),
            pl.BlockSpec((k, D_MODEL), lambda i: (0, 0)),
        ],
        out_specs=pl.BlockSpec((tm, D_MODEL), lambda i: (i, 0)),
        out_shape=jax.ShapeDtypeStruct((TOKENS, D_MODEL), F32),
        compiler_params=_params("parallel"),
        name=name,
    )(x, o, w)


def _mlp_kernel(x_ref, g_ref, wu_ref, wd_ref, o_ref, xn_ref):
    j = pl.program_id(1)

    @pl.when(j == 0)
    def _():
        x = x_ref[...]
        xn_ref[...] = _rms_f32(x, g_ref[...]).astype(BF16)
        o_ref[...] = x

    h = _dot(xn_ref[...], wu_ref[...])
    h = jnp.square(jnp.maximum(h, 0.0)).astype(BF16)
    o_ref[...] += _dot(h, wd_ref[...])


def _mlp(x, g, wu, wd):
    tm, th = MLP_TM, MLP_TH
    return pl.pallas_call(
        _mlp_kernel,
        grid=(TOKENS // tm, MLP_HIDDEN // th),
        in_specs=[
            pl.BlockSpec((tm, D_MODEL), lambda i, j: (i, 0)),
            pl.BlockSpec((1, D_MODEL), lambda i, j: (0, 0)),
            pl.BlockSpec((D_MODEL, th), lambda i, j: (0, j)),
            pl.BlockSpec((th, D_MODEL), lambda i, j: (j, 0)),
        ],
        out_specs=pl.BlockSpec((tm, D_MODEL), lambda i, j: (i, 0)),
        out_shape=jax.ShapeDtypeStruct((TOKENS, D_MODEL), F32),
        scratch_shapes=[pltpu.VMEM((tm, D_MODEL), BF16)],
        compiler_params=_params("parallel", "arbitrary"),
        name="mlp",
    )(x, g, wu, wd)


def _ple_kernel(x_ref, g_ref, wg_ref, p_ref, wp_ref, fg_ref, o_ref, *, final):
    x = x_ref[...]
    xn = _rms_f32(x, g_ref[...]).astype(BF16)
    gate = jax.nn.sigmoid(_dot(xn, wg_ref[...]))
    proj = _dot(p_ref[...].astype(BF16), wp_ref[...])
    y = x + gate * proj
    if final:
        y = _rms_f32(y, fg_ref[...])
    o_ref[...] = y


def _ple(x, g, wg, p, wp, fg, final):
    tm = PROJ_TM
    return pl.pallas_call(
        functools.partial(_ple_kernel, final=final),
        grid=(TOKENS // tm,),
        in_specs=[
            pl.BlockSpec((tm, D_MODEL), lambda i: (i, 0)),
            pl.BlockSpec((1, D_MODEL), lambda i: (0, 0)),
            pl.BlockSpec((D_MODEL, D_MODEL), lambda i: (0, 0)),
            pl.BlockSpec((tm, PLE_DIM), lambda i: (i, 0)),
            pl.BlockSpec((PLE_DIM, D_MODEL), lambda i: (0, 0)),
            pl.BlockSpec((1, D_MODEL), lambda i: (0, 0)),
        ],
        out_specs=pl.BlockSpec((tm, D_MODEL), lambda i: (i, 0)),
        out_shape=jax.ShapeDtypeStruct((TOKENS, D_MODEL), F32),
        compiler_params=_params("parallel"),
        name="ple_final" if final else "ple",
    )(x, g, wg, p, wp, fg)


def _a_qkv_kernel(x_ref, g_ref, w_ref, c_ref, sa_ref, sb_ref, *refs):
    outs, stage = refs[:9], refs[9]
    tm = x_ref.shape[0]
    gl = A_GROUP_LANES
    nc = A_GROUP_CHUNKS
    xn = _rms_f32(x_ref[...], g_ref[...]).astype(BF16)
    c, sa, sb = c_ref[...], sa_ref[...], sb_ref[...]

    def emit(dil, dsts):
        n = tm // dil
        for k, dst in enumerate(dsts):
            for ci in range(nc):
                for r in range(dil):
                    if dil == 1:
                        rows = stage[k * nc + ci]
                    else:
                        rows = stage[k * nc + ci, pl.ds(r, n, stride=dil), :]
                    dst[0, r, ci] = rows.astype(BF16)

    for g, (_, dil) in enumerate(DIL_PAIRS):
        y = _dot(xn, w_ref[:, 2 * gl * g:2 * gl * (g + 1)])
        for i in range(2 * nc):
            stage[i] = _rope_chunk(y[:, i * LANES:(i + 1) * LANES], c, sa, sb, ROT_DIM // 2)
        emit(dil, (outs[3 * g], outs[3 * g + 1]))
    for g, (_, dil) in enumerate(DIL_PAIRS):
        y = _dot(xn, w_ref[:, 6 * gl + gl * g:6 * gl + gl * (g + 1)])
        for i in range(nc):
            stage[i] = y[:, i * LANES:(i + 1) * LANES]
        emit(dil, (outs[3 * g + 2],))


def _a_qkv(x, g, w, tabs):
    tm = PROJ_TM
    nb = SEQ // tm
    gl = A_GROUP_LANES
    tab_spec = pl.BlockSpec((tm, LANES), lambda i: (i % nb, 0))
    out_shapes, out_specs = [], []
    nc = A_GROUP_CHUNKS
    for _, dil in DIL_PAIRS:
        for _ in range(3):
            out_shapes.append(jax.ShapeDtypeStruct((BATCH, dil, nc, SEQ // dil, LANES), BF16))
            out_specs.append(pl.BlockSpec((1, dil, nc, tm // dil, LANES),
                                          lambda i: (i // nb, 0, 0, i % nb, 0)))
    return pl.pallas_call(
        _a_qkv_kernel,
        grid=(TOKENS // tm,),
        in_specs=[
            pl.BlockSpec((tm, D_MODEL), lambda i: (i, 0)),
            pl.BlockSpec((1, D_MODEL), lambda i: (0, 0)),
            pl.BlockSpec((D_MODEL, 9 * gl), lambda i: (0, 0)),
            tab_spec, tab_spec, tab_spec,
        ],
        out_specs=out_specs,
        out_shape=out_shapes,
        scratch_shapes=[pltpu.VMEM((2 * nc, tm, LANES), F32)],
        compiler_params=_params("parallel"),
        name="a_qkv",
    )(x, g, w, *tabs)


def _a_attn_kernel(q_ref, k_ref, v_ref, o_ref, lse_ref, *, seg, dil):
    r = pl.program_id(1)
    tq = BAND_TQ
    kw = min(seg, tq + 2 * A_BAND_HALF)
    lane = _lane_iota()
    low = lane < HEAD_DIM

    def tile(t, carry):
        q0 = pl.multiple_of(t * tq, tq)
        ws = pl.multiple_of(jnp.clip(q0 - A_BAND_HALF, 0, seg - kw), A_BAND_HALF)
        rows = q0 + lax.broadcasted_iota(jnp.int32, (tq, 1), 0)
        cols = ws + lax.broadcasted_iota(jnp.int32, (1, kw), 1)
        valid = jnp.abs(cols - rows) <= A_BAND_HALF
        for c in range(A_GROUP_CHUNKS):
            qc = q_ref[0, 0, c, pl.ds(q0, tq), :] * jnp.asarray(HEAD_DIM ** -0.5, BF16)
            kc = k_ref[0, 0, c, pl.ds(ws, kw), :]
            vc = v_ref[0, 0, c, pl.ds(ws, kw), :]
            outs, lses = [], []
            n_heads = 2 if 2 * c + 1 < A_GROUP_HEADS else 1
            for hh in range(n_heads):
                qm = jnp.where(low if hh == 0 else jnp.logical_not(low), qc, jnp.zeros_like(qc))
                s = jnp.where(valid, _dot_nt(qm, kc), NEG_INF)
                m, e, l = _softmax_parts(s)
                outs.append(_dot(e.astype(BF16), vc) * (1.0 / l))
                lses.append(m + jnp.log(l))
            if n_heads == 2:
                o = jnp.where(low, outs[0], outs[1])
                ls = jnp.where(low, lses[0], lses[1])
            else:
                o = jnp.where(low, outs[0], 0.0)
                ls = jnp.where(low, lses[0], 0.0)
            if dil == 1:
                dst = pl.ds(q0, tq)
            else:
                dst = pl.ds(r + dil * q0, tq, stride=dil)
            o_ref[0, c, dst, :] = o
            lse_ref[0, c, dst, :] = jnp.broadcast_to(ls, o.shape)
        return carry

    lax.fori_loop(0, seg // tq, tile, 0)


def _a_attn(q, k, v, dil):
    seg = SEQ // dil
    nc = A_GROUP_CHUNKS
    in_spec = pl.BlockSpec((1, 1, nc, seg, LANES), lambda b, r: (b, r, 0, 0, 0))
    out_spec = pl.BlockSpec((1, nc, SEQ, LANES), lambda b, r: (b, 0, 0, 0))
    out_shape = jax.ShapeDtypeStruct((BATCH, nc, SEQ, LANES), F32)
    return pl.pallas_call(
        functools.partial(_a_attn_kernel, seg=seg, dil=dil),
        grid=(BATCH, dil),
        in_specs=[in_spec, in_spec, in_spec],
        out_specs=[out_spec, out_spec],
        out_shape=[out_shape, out_shape],
        compiler_params=_params("parallel", "arbitrary"),
        name=f"a_attn_d{dil}",
    )(q, k, v)


def _a_out_kernel(x_ref, o0, o1, o2, l0, l1, l2, w_ref, out_ref):
    cat = lambda ref: jnp.concatenate([ref[0, c] for c in range(A_GROUP_CHUNKS)], axis=1)
    ls = [cat(l0), cat(l1), cat(l2)]
    m = jnp.maximum(jnp.maximum(ls[0], ls[1]), ls[2])
    es = [jnp.exp(l - m) for l in ls]
    inv = 1.0 / (es[0] + es[1] + es[2])
    acc = x_ref[...]
    for g, o in enumerate((o0, o1, o2)):
        acc = acc + _dot((cat(o) * (es[g] * inv)).astype(BF16), w_ref[g])
    out_ref[...] = acc


def _a_out(x, os_, ls_, w):
    tm = PROJ_TM
    nb = SEQ // tm
    gl = A_GROUP_LANES
    part = pl.BlockSpec((1, A_GROUP_CHUNKS, tm, LANES), lambda i: (i // nb, 0, i % nb, 0))
    return pl.pallas_call(
        _a_out_kernel,
        grid=(TOKENS // tm,),
        in_specs=[pl.BlockSpec((tm, D_MODEL), lambda i: (i, 0))] + [part] * 6 + [
            pl.BlockSpec((3, gl, D_MODEL), lambda i: (0, 0, 0))],
        out_specs=pl.BlockSpec((tm, D_MODEL), lambda i: (i, 0)),
        out_shape=jax.ShapeDtypeStruct((TOKENS, D_MODEL), F32),
        compiler_params=_params("parallel"),
        name="a_out",
    )(x, *os_, *ls_, w)


def _b_proj_kernel(x_ref, g_ref, win_ref, qn_ref, kvn_ref, wuq_ref, wuk_ref, wuvt_ref,
                   c_ref, sa_ref, sb_ref, q_out, k_out, vt_out):
    c, sa, sb = c_ref[...], sa_ref[...], sb_ref[...]
    xn = _rms_f32(x_ref[...], g_ref[...]).astype(BF16)
    z = _dot(xn, win_ref[...])
    cq = _rms_f32(z[:, :B_Q_RANK], qn_ref[...]).astype(BF16)
    ckv = _rms_f32(z[:, B_Q_RANK:B_Q_RANK + B_KV_RANK], kvn_ref[...]).astype(BF16)
    k_rope = _rope_chunk(z[:, B_Q_RANK + B_KV_RANK:], c, sa, sb, B_ROPE // 2)
    q = _dot(cq, wuq_ref[...])
    k = _dot(ckv, wuk_ref[...])
    for h in range(B_HEADS):
        sl = slice(h * LANES, (h + 1) * LANES)
        q_out[:, sl] = _rope_chunk(q[:, sl], c, sa, sb, B_ROPE // 2).astype(BF16)
        k_out[:, sl] = (k[:, sl] + k_rope).astype(BF16)
    vt_out[...] = _dot_nt(wuvt_ref[...], ckv).astype(BF16)


def _b_proj(x, g, win, qn, kvn, wuq, wuk, wuvt, tabs):
    tm = PROJ_TM
    nb = SEQ // tm
    tab_spec = pl.BlockSpec((tm, LANES), lambda i: (i % nb, 0))

    def full(a):
        return pl.BlockSpec(a.shape, lambda i: (0,) * a.ndim)

    nqk = B_HEADS * LANES
    nv = B_HEADS * B_V
    return pl.pallas_call(
        _b_proj_kernel,
        grid=(TOKENS // tm,),
        in_specs=[pl.BlockSpec((tm, D_MODEL), lambda i: (i, 0)), full(g), full(win), full(qn),
                  full(kvn), full(wuq), full(wuk), full(wuvt), tab_spec, tab_spec, tab_spec],
        out_specs=[pl.BlockSpec((tm, nqk), lambda i: (i, 0)),
                   pl.BlockSpec((tm, nqk), lambda i: (i, 0)),
                   pl.BlockSpec((nv, tm), lambda i: (0, i))],
        out_shape=[jax.ShapeDtypeStruct((TOKENS, nqk), BF16),
                   jax.ShapeDtypeStruct((TOKENS, nqk), BF16),
                   jax.ShapeDtypeStruct((nv, TOKENS), BF16)],
        compiler_params=_params("parallel"),
        name="b_proj",
    )(x, g, win, qn, kvn, wuq, wuk, wuvt, *tabs)


def _softmax_keys_major(st, c):
    m = jnp.max(st, axis=0, keepdims=True)
    e = jnp.exp2((st - m) * c)
    return e, jnp.sum(e, axis=0, keepdims=True)


def _attend_keys_major(streams, c, n_chunks):
    items = [(j, s) for j in range(n_chunks) for s in range(len(streams))]
    state = [None] * len(streams)

    def scores(item):
        j, s = item
        k_chunk, _, qw = streams[s]
        return _dot_nt(k_chunk(j), qw)

    def absorb(item, st):
        j, s = item
        vt_chunk = streams[s][1]
        mj = jnp.max(st, axis=0, keepdims=True)
        if j == 0:
            e = jnp.exp2((st - mj) * c)
            state[s] = (mj, jnp.sum(e, axis=0, keepdims=True), _dot(vt_chunk(j), e.astype(BF16)))
        else:
            m, l, acc = state[s]
            m_new = jnp.maximum(m, mj)
            a = jnp.exp2((m - m_new) * c)
            e = jnp.exp2((st - m_new) * c)
            state[s] = (m_new, l * a + jnp.sum(e, axis=0, keepdims=True),
                        acc * a + _dot(vt_chunk(j), e.astype(BF16)))

    st = scores(items[0])
    for idx, item in enumerate(items):
        st_next = scores(items[idx + 1]) if idx + 1 < len(items) else None
        absorb(item, st)
        st = st_next
    return [(acc, l) for _, l, acc in state]


def _b_attn_kernel(q_ref, k_ref, vt_ref, o_ref, *, c):
    kc = ATTN_KC

    def stream(hh):
        sl = slice(hh * LANES, (hh + 1) * LANES)
        rows = slice(hh * B_V, (hh + 1) * B_V)
        return (lambda j: k_ref[j * kc:(j + 1) * kc, sl],
                lambda j: vt_ref[rows, j * kc:(j + 1) * kc],
                q_ref[:, sl])

    outs = [acc * (1.0 / l) for acc, l in _attend_keys_major([stream(0), stream(1)], c, SEQ // kc)]
    o_ref[...] = jnp.concatenate(outs, axis=0).T.astype(BF16)


def _b_attn(q, k, vt):
    tq = ATTN_TQ
    nq = SEQ // tq
    return pl.pallas_call(
        functools.partial(_b_attn_kernel, c=(B_NOPE + B_ROPE) ** -0.5 * LOG2_E),
        grid=(BATCH, B_HEADS // 2, nq),
        in_specs=[
            pl.BlockSpec((tq, 2 * LANES), lambda b, h, i: (b * nq + i, h)),
            pl.BlockSpec((SEQ, 2 * LANES), lambda b, h, i: (b, h)),
            pl.BlockSpec((2 * B_V, SEQ), lambda b, h, i: (h, b)),
        ],
        out_specs=pl.BlockSpec((tq, LANES), lambda b, h, i: (b * nq + i, h)),
        out_shape=jax.ShapeDtypeStruct((TOKENS, B_HEADS * B_V), BF16),
        compiler_params=_params("parallel", "parallel", "arbitrary"),
        name="b_attn",
    )(q, k, vt)


def _c_group_geometry(gi):
    first_row = gi * C_ROW_GROUP
    wrow = min(max(first_row - NA_ROWS // 2, 0), GRID_ROWS - C_WIN_ROWS)
    n_groups = GRID_ROWS // C_ROW_GROUP
    case = 0 if gi == 0 else (2 if gi == n_groups - 1 else 1)
    return first_row, wrow, case


def _c_attn_kernel(q_ref, k_ref, vt_ref, bias_ref, o_ref):
    low = _lane_iota() < HEAD_DIM
    nq = C_ROW_GROUP * GRID_W
    nk = C_WIN_ROWS * GRID_W
    for gi in range(GRID_ROWS // C_ROW_GROUP):
        first_row, wrow, case = _c_group_geometry(gi)
        q0, ws = first_row * GRID_W, wrow * GRID_W
        qc = q_ref[q0:q0 + nq, :] * jnp.asarray(HEAD_DIM ** -0.5, BF16)
        zero = jnp.zeros_like(qc)
        qq = jnp.concatenate([jnp.where(low, qc, zero), jnp.where(low, zero, qc)], axis=0)
        st = _dot_nt(k_ref[ws:ws + nk, :], qq) + bias_ref[0, case]
        e, l = _softmax_keys_major(st, LOG2_E)
        ot = _dot(vt_ref[:, ws:ws + nk], e.astype(BF16)) * (1.0 / l)
        o = jnp.concatenate([ot[:HEAD_DIM, :nq], ot[HEAD_DIM:, nq:]], axis=0)
        o_ref[q0:q0 + nq, :] = o.T.astype(BF16)


def _c_attn(qk, vt, bias):
    nq = C_ROW_GROUP * GRID_W
    nk = C_WIN_ROWS * GRID_W
    npair = C_HEADS // 2
    return pl.pallas_call(
        _c_attn_kernel,
        grid=(npair, BATCH),
        in_specs=[
            pl.BlockSpec((SEQ, LANES), lambda h, b: (b, h)),
            pl.BlockSpec((SEQ, LANES), lambda h, b: (b, npair + h)),
            pl.BlockSpec((LANES, SEQ), lambda h, b: (h, b)),
            pl.BlockSpec((1, 3, nk, 2 * nq), lambda h, b: (h, 0, 0, 0)),
        ],
        out_specs=pl.BlockSpec((SEQ, LANES), lambda h, b: (b, h)),
        out_shape=jax.ShapeDtypeStruct((TOKENS, C_HEADS * HEAD_DIM), BF16),
        compiler_params=_params("parallel", "arbitrary"),
        name="c_attn",
    )(qk, qk, vt, bias)


def _c_bias_table(rpb):
    hi = lax.Precision.HIGHEST
    kc, qc = np.arange(GRID_W)[:, None], np.arange(GRID_W)[None, :]
    win0 = np.clip(qc - NA_COLS // 2, 0, GRID_W - NA_COLS)
    col_ok = (kc >= win0) & (kc < win0 + NA_COLS)
    col_off = np.clip(kc - qc + NA_COLS - 1, 0, 2 * NA_COLS - 2)
    n_co, n_ro = 2 * NA_COLS - 1, 2 * NA_ROWS - 1
    col_sel = (col_off[None] == np.arange(n_co)[:, None, None]).astype(np.float32)
    cols = jnp.einsum("hab,bkq->hakq", rpb, col_sel, precision=hi)
    cols = jnp.where(col_ok, cols, NEG_INF)
    cols = jnp.concatenate([cols, jnp.full((C_HEADS, 1, GRID_W, GRID_W), NEG_INF, F32)], axis=1)
    n_groups = GRID_ROWS // C_ROW_GROUP
    row_sel = np.zeros((3, C_WIN_ROWS, C_ROW_GROUP, n_ro + 1), np.float32)
    for gi in (0, 1, n_groups - 1):
        first_row, wrow, case = _c_group_geometry(gi)
        for kr in range(C_WIN_ROWS):
            for i in range(C_ROW_GROUP):
                qrow, krow = first_row + i, wrow + kr
                rs = min(max(qrow - NA_ROWS // 2, 0), GRID_ROWS - NA_ROWS)
                ok = rs <= krow < rs + NA_ROWS
                row_sel[case, kr, i, krow - qrow + NA_ROWS - 1 if ok else n_ro] = 1.0
    cols = cols.reshape(C_HEADS // 2, 2, n_ro + 1, GRID_W, GRID_W)
    table = jnp.einsum("ekia,pjanc->peknjic", row_sel, cols, precision=hi)
    return table.reshape(C_HEADS // 2, 3, C_WIN_ROWS * GRID_W, 2 * C_ROW_GROUP * GRID_W)


def _d_attn_kernel(lq1, lk1, lq2, lk2, q_ref, k_ref, vt_ref, sub_ref, o_ref, *, c, lambda_init):
    lam = (jnp.exp(jnp.sum(lq1[...] * lk1[...], axis=-1, keepdims=True))
           - jnp.exp(jnp.sum(lq2[...] * lk2[...], axis=-1, keepdims=True)) + lambda_init)
    low = _lane_iota() < D_HEAD
    kc = ATTN_KC
    q = q_ref[...]
    zero = jnp.zeros_like(q)
    k_chunk = lambda j: k_ref[j * kc:(j + 1) * kc, :]
    vt_chunk = lambda j: vt_ref[:, j * kc:(j + 1) * kc]
    (acc1, l1), (acc2, l2) = _attend_keys_major(
        [(k_chunk, vt_chunk, jnp.where(low, q, zero)),
         (k_chunk, vt_chunk, jnp.where(low, zero, q))], c, SEQ // kc)
    o = (acc1 * (1.0 / l1) - lam * (acc2 * (1.0 / l2))).T
    o_ref[...] = (_rms_f32(o, sub_ref[...]) * (1.0 - lambda_init)).astype(BF16)


def _d_attn(qk, vt, lq1, lk1, lq2, lk2, subln, lambda_init):
    tq = ATTN_TQ
    nq = SEQ // tq
    vec = pl.BlockSpec((1, D_HEAD), lambda b, h, i: (0, 0))
    return pl.pallas_call(
        functools.partial(_d_attn_kernel, c=D_HEAD ** -0.5 * LOG2_E, lambda_init=lambda_init),
        grid=(BATCH, D_HEADS, nq),
        in_specs=[
            vec, vec, vec, vec,
            pl.BlockSpec((tq, LANES), lambda b, h, i: (b * nq + i, h)),
            pl.BlockSpec((SEQ, LANES), lambda b, h, i: (b, D_HEADS + h)),
            pl.BlockSpec((LANES, SEQ), lambda b, h, i: (h, b)),
            pl.BlockSpec((1, 2 * D_HEAD), lambda b, h, i: (0, 0)),
        ],
        out_specs=pl.BlockSpec((tq, LANES), lambda b, h, i: (b * nq + i, h)),
        out_shape=jax.ShapeDtypeStruct((TOKENS, 2 * D_HEADS * D_HEAD), BF16),
        compiler_params=_params("parallel", "parallel", "arbitrary"),
        name="d_attn",
    )(lq1, lk1, lq2, lk2, qk, qk, vt, subln)


def _rope_tables(rot_dim, period, lane0):
    r = rot_dim // 2
    inv = ROPE_THETA ** (-jnp.arange(0, rot_dim, 2, dtype=F32) / rot_dim)
    ang = jnp.arange(SEQ, dtype=F32)[:, None] * inv[None, :]
    cos, sin = jnp.cos(ang), jnp.sin(ang)
    lane = np.arange(LANES) % period - lane0
    first = (lane >= 0) & (lane < r)
    second = (lane >= r) & (lane < 2 * r)
    idx = np.where(first, lane, np.where(second, lane - r, 0))
    cg, sg = cos[:, idx], sin[:, idx]
    c = jnp.where(first | second, cg, 1.0)
    sa = jnp.where(first, -sg, 0.0)
    sb = jnp.where(second, sg, 0.0)
    return c, sa, sb


def _a_weights(w_qkv, w_o):
    nh = A_GROUP_HEADS * HEAD_DIM
    pad = jnp.zeros((D_MODEL, A_GROUP_LANES - nh), w_qkv.dtype)

    def cols(part, g):
        base = part * A_HEADS * HEAD_DIM + g * nh
        return [w_qkv[:, base:base + nh], pad]

    pieces = []
    for g in range(len(DIL_PAIRS)):
        pieces += cols(0, g) + cols(1, g)
    for g in range(len(DIL_PAIRS)):
        pieces += cols(2, g)
    w = jnp.concatenate(pieces, axis=1).astype(BF16)
    wo = w_o.reshape(len(DIL_PAIRS), nh, D_MODEL)
    wo = jnp.pad(wo, ((0, 0), (0, A_GROUP_LANES - nh), (0, 0))).astype(BF16)
    return w, wo


def _b_weights(w_in, w_uq, w_ukv):
    split = B_Q_RANK + B_KV_RANK
    win = jnp.concatenate([
        w_in[:, :split], jnp.zeros((D_MODEL, B_NOPE), w_in.dtype), w_in[:, split:],
        jnp.zeros((D_MODEL, LANES - B_NOPE - B_ROPE), w_in.dtype)], axis=1).astype(BF16)
    wuq = w_uq.reshape(B_Q_RANK, B_HEADS, B_NOPE + B_ROPE)
    wuq = jnp.pad(wuq, ((0, 0), (0, 0), (0, LANES - B_NOPE - B_ROPE)))
    wuq = wuq.reshape(B_Q_RANK, B_HEADS * LANES).astype(BF16)
    wukv = w_ukv.reshape(B_KV_RANK, B_HEADS, B_NOPE + B_V)
    wuk = jnp.pad(wukv[:, :, :B_NOPE], ((0, 0), (0, 0), (0, LANES - B_NOPE)))
    wuk = wuk.reshape(B_KV_RANK, B_HEADS * LANES).astype(BF16)
    wuvt = wukv[:, :, B_NOPE:].reshape(B_KV_RANK, B_HEADS * B_V).T.astype(BF16)
    return win, wuq, wuk, wuvt


def _split_qk_vt(w_qkv, n_qk):
    return w_qkv[:, :n_qk].astype(BF16), w_qkv[:, n_qk:].T.astype(BF16)


def kernel(x, p, a_norm, a_w_qkv, a_w_o, b_norm, b_w_in, b_q_norm, b_w_uq, b_kv_norm, b_w_ukv, b_w_o, c_norm, c_w_qkv, c_rpb, c_w_o, d_norm, d_w_qkv, d_lambda_q1, d_lambda_k1, d_lambda_q2, d_lambda_k2, d_subln, d_w_o, mlp_norm, w_up, w_down, ple_norm, w_ple_gate, w_ple_proj, final_norm):
    tabs_p = _rope_tables(ROT_DIM, HEAD_DIM, 0)
    tabs_l = _rope_tables(B_ROPE, LANES, B_NOPE)
    h = x.reshape(TOKENS, D_MODEL)
    row = lambda v: v.reshape(1, -1)

    for i in range(DEPTH):
        if i == 0:
            w, wo = _a_weights(a_w_qkv[0], a_w_o[0])
            parts = _a_qkv(h, row(a_norm[0]), w, tabs_p)
            os_, ls_ = [], []
            for g, (_, dil) in enumerate(DIL_PAIRS):
                o, lse = _a_attn(parts[3 * g], parts[3 * g + 1], parts[3 * g + 2], dil)
                os_.append(o)
                ls_.append(lse)
            h = _a_out(h, os_, ls_, wo)
        elif i == 1:
            win, wuq, wuk, wuvt = _b_weights(b_w_in[0], b_w_uq[0], b_w_ukv[0])
            q, k, vt = _b_proj(h, row(b_norm[0]), win, row(b_q_norm[0]), row(b_kv_norm[0]),
                               wuq, wuk, wuvt, tabs_l)
            h = _proj_res(h, _b_attn(q, k, vt), b_w_o[0].astype(BF16), "b_out")
        elif i == 2:
            w, wvt = _split_qk_vt(c_w_qkv[0], 2 * C_HEADS * HEAD_DIM)
            qk, vt = _norm_proj(h, row(c_norm[0]), w, wvt, tabs_p, False, "c_qkv")
            o = _c_attn(qk, vt, _c_bias_table(c_rpb[0]))
            h = _proj_res(h, o, c_w_o[0].astype(BF16), "c_out")
        else:
            lambda_init = 0.8 - 0.6 * math.exp(-0.3 * i)
            w, wvt = _split_qk_vt(d_w_qkv[0], 2 * 2 * D_HEADS * D_HEAD)
            qk, vt = _norm_proj(h, row(d_norm[0]), w, wvt, tabs_p, True, "d_qkv")
            o = _d_attn(qk, vt, row(d_lambda_q1[0]), row(d_lambda_k1[0]), row(d_lambda_q2[0]),
                        row(d_lambda_k2[0]), row(d_subln[0]), lambda_init)
            h = _proj_res(h, o, d_w_o[0].astype(BF16), "d_out")
        h = _mlp(h, row(mlp_norm[i]), w_up[i].astype(BF16), w_down[i].astype(BF16))
        h = _ple(h, row(ple_norm[i]), w_ple_gate[i].astype(BF16), p[i].reshape(TOKENS, PLE_DIM),
                 w_ple_proj[i].astype(BF16), row(final_norm), i == DEPTH - 1)
    return h.reshape(BATCH, SEQ, D_MODEL)
```

```python
import functools
import math

import numpy as np
import jax
import jax.numpy as jnp
from jax import lax
from jax.experimental import pallas as pl
from jax.experimental.pallas import tpu as pltpu

F32 = jnp.float32
BF16 = jnp.bfloat16

D_MODEL = 1024
BATCH = 8
SEQ = 2048
DEPTH = 4
TOKENS = BATCH * SEQ
HEAD_DIM = 64
ROPE_THETA = 500000.0
ROT_DIM = HEAD_DIM // 4
NEG_INF = -1e30
RMS_EPS = 1e-6
LOG2_E = math.log2(math.e)

DIL_PAIRS = ((128, 1), (512, 4), (2048, 16))
A_GROUP_HEADS = 5
A_HEADS = A_GROUP_HEADS * len(DIL_PAIRS)
A_BAND_HALF = 64
A_GROUP_LANES = 384
A_GROUP_CHUNKS = A_GROUP_LANES // 128

B_HEADS = 16
B_Q_RANK = 256
B_KV_RANK = 128
B_NOPE = 64
B_ROPE = 32
B_V = 64

C_HEADS = 16
GRID_W = 64
GRID_ROWS = SEQ // GRID_W
NA_ROWS = 8
NA_COLS = 16
C_ROW_GROUP = 4
C_WIN_ROWS = 12

D_HEADS = 8
D_HEAD = 64

MLP_HIDDEN = 4 * D_MODEL
PLE_DIM = 256

LANES = 128
VMEM_LIMIT = 48 * 1024 * 1024

PROJ_TM = 512
MLP_TH = 1024
ATTN_TQ = 512
ATTN_KC = 512
BAND_TQ = 128


def _params(*sem):
    return pltpu.CompilerParams(dimension_semantics=sem, vmem_limit_bytes=VMEM_LIMIT)


def _rms_f32(x, g):
    ms = jnp.mean(x * x, axis=-1, keepdims=True)
    return x * lax.rsqrt(ms + RMS_EPS) * g


def _rope_chunk(y, c, sa, sb, shift):
    return (y * c + pltpu.roll(y, LANES - shift, 1) * sa + pltpu.roll(y, shift, 1) * sb)


def _softmax_parts(s):
    m = jnp.max(s, axis=-1, keepdims=True)
    e = jnp.exp(s - m)
    l = jnp.sum(e, axis=-1, keepdims=True)
    return m, e, l


def _dot(a, b):
    return jnp.dot(a, b, preferred_element_type=F32)


def _dot_nt(a, b):
    return lax.dot_general(a, b, (((1,), (1,)), ((), ())), preferred_element_type=F32)


def _lane_iota():
    return lax.broadcasted_iota(jnp.int32, (1, LANES), 1)


def _norm_proj_kernel(x_ref, g_ref, w_ref, wvt_ref, c_ref, sa_ref, sb_ref, qk_ref, vt_ref,
                      *, rope, tn):
    xn = _rms_f32(x_ref[...], g_ref[...]).astype(BF16)
    n = qk_ref.shape[1]
    c, sa, sb = c_ref[...], sa_ref[...], sb_ref[...]
    for j in range(n // tn):
        y = _dot(xn, w_ref[:, j * tn:(j + 1) * tn])
        for i in range(tn // LANES):
            col = j * tn + i * LANES
            chunk = y[:, i * LANES:(i + 1) * LANES]
            if rope:
                chunk = _rope_chunk(chunk, c, sa, sb, ROT_DIM // 2)
            qk_ref[:, col:col + LANES] = chunk.astype(BF16)
    vt_ref[...] = _dot_nt(wvt_ref[...], xn).astype(BF16)


def _norm_proj(x, g, w, wvt, tabs, rope, name):
    n = w.shape[1]
    nv = wvt.shape[0]
    tm = PROJ_TM
    nb = SEQ // tm
    tab_spec = pl.BlockSpec((tm, LANES), lambda i: (i % nb, 0))
    return pl.pallas_call(
        functools.partial(_norm_proj_kernel, rope=rope, tn=512),
        grid=(TOKENS // tm,),
        in_specs=[
            pl.BlockSpec((tm, D_MODEL), lambda i: (i, 0)),
            pl.BlockSpec((1, D_MODEL), lambda i: (0, 0)),
            pl.BlockSpec((D_MODEL, n), lambda i: (0, 0)),
            pl.BlockSpec((nv, D_MODEL), lambda i: (0, 0)),
            tab_spec, tab_spec, tab_spec,
        ],
        out_specs=[pl.BlockSpec((tm, n), lambda i: (i, 0)),
                   pl.BlockSpec((nv, tm), lambda i: (0, i))],
        out_shape=[jax.ShapeDtypeStruct((TOKENS, n), BF16),
                   jax.ShapeDtypeStruct((nv, TOKENS), BF16)],
        compiler_params=_params("parallel"),
        name=name,
    )(x, g, w, wvt, *tabs)


def _mixer_residual(x_ref, o_ref, w_ref):
    return x_ref[...] + _dot(o_ref[...], w_ref[...])


def _tail_kernel(*refs, n_front, front, final):
    front_refs = refs[:n_front]
    gm_ref, wu_ref, wd_ref, gp_ref, wg_ref, p_ref, wp_ref, fg_ref, out_ref = refs[n_front:]
    x1 = front(*front_refs)
    xn = _rms_f32(x1, gm_ref[...]).astype(BF16)
    acc = x1
    for j in range(MLP_HIDDEN // MLP_TH):
        h = _dot(xn, wu_ref[:, j * MLP_TH:(j + 1) * MLP_TH])
        h = jnp.square(jnp.maximum(h, 0.0)).astype(BF16)
        acc = acc + _dot(h, wd_ref[j * MLP_TH:(j + 1) * MLP_TH, :])
    gate = jax.nn.sigmoid(_dot(_rms_f32(acc, gp_ref[...]).astype(BF16), wg_ref[...]))
    y = acc + gate * _dot(p_ref[...].astype(BF16), wp_ref[...])
    if final:
        y = _rms_f32(y, fg_ref[...])
    out_ref[...] = y


def _resident(a):
    return pl.BlockSpec(a.shape, lambda i: (0,) * a.ndim, pipeline_mode=pl.Buffered(1))


def _tail(front, front_args, front_specs, gm, wu, wd, gp, wg, p, wp, fg, final, name):
    tm = PROJ_TM
    shared = (gm, wu, wd, gp, wg)
    return pl.pallas_call(
        functools.partial(_tail_kernel, n_front=len(front_args), front=front, final=final),
        grid=(TOKENS // tm,),
        in_specs=list(front_specs) + [_resident(a) for a in shared] + [
            pl.BlockSpec((tm, PLE_DIM), lambda i: (i, 0)), _resident(wp), _resident(fg)],
        out_specs=pl.BlockSpec((tm, D_MODEL), lambda i: (i, 0)),
        out_shape=jax.ShapeDtypeStruct((TOKENS, D_MODEL), F32),
        compiler_params=_params("parallel"),
        name=name,
    )(*front_args, *shared, p, wp, fg)


def _tail_plain(x, o, w_o, *rest, name):
    tm = PROJ_TM
    specs = [pl.BlockSpec((tm, D_MODEL), lambda i: (i, 0)),
             pl.BlockSpec((tm, o.shape[1]), lambda i: (i, 0)), _resident(w_o)]
    return _tail(_mixer_residual, (x, o, w_o), specs, *rest, name=name)


def _a_qkv_kernel(x_ref, g_ref, w_ref, c_ref, sa_ref, sb_ref, *refs):
    outs, stage = refs[:9], refs[9]
    tm = x_ref.shape[0]
    gl = A_GROUP_LANES
    nc = A_GROUP_CHUNKS
    xn = _rms_f32(x_ref[...], g_ref[...]).astype(BF16)
    c, sa, sb = c_ref[...], sa_ref[...], sb_ref[...]

    def emit(dil, dsts):
        n = tm // dil
        for k, dst in enumerate(dsts):
            for ci in range(nc):
                for r in range(dil):
                    if dil == 1:
                        rows = stage[k * nc + ci]
                    else:
                        rows = stage[k * nc + ci, pl.ds(r, n, stride=dil), :]
                    dst[0, r, ci] = rows.astype(BF16)

    for g, (_, dil) in enumerate(DIL_PAIRS):
        y = _dot(xn, w_ref[:, 2 * gl * g:2 * gl * (g + 1)])
        for i in range(2 * nc):
            stage[i] = _rope_chunk(y[:, i * LANES:(i + 1) * LANES], c, sa, sb, ROT_DIM // 2)
        emit(dil, (outs[3 * g], outs[3 * g + 1]))
    for g, (_, dil) in enumerate(DIL_PAIRS):
        y = _dot(xn, w_ref[:, 6 * gl + gl * g:6 * gl + gl * (g + 1)])
        for i in range(nc):
            stage[i] = y[:, i * LANES:(i + 1) * LANES]
        emit(dil, (outs[3 * g + 2],))


def _a_qkv(x, g, w, tabs):
    tm = PROJ_TM
    nb = SEQ // tm
    gl = A_GROUP_LANES
    tab_spec = pl.BlockSpec((tm, LANES), lambda i: (i % nb, 0))
    out_shapes, out_specs = [], []
    nc = A_GROUP_CHUNKS
    for _, dil in DIL_PAIRS:
        for _ in range(3):
            out_shapes.append(jax.ShapeDtypeStruct((BATCH, dil, nc, SEQ // dil, LANES), BF16))
            out_specs.append(pl.BlockSpec((1, dil, nc, tm // dil, LANES),
                                          lambda i: (i // nb, 0, 0, i % nb, 0)))
    return pl.pallas_call(
        _a_qkv_kernel,
        grid=(TOKENS // tm,),
        in_specs=[
            pl.BlockSpec((tm, D_MODEL), lambda i: (i, 0)),
            pl.BlockSpec((1, D_MODEL), lambda i: (0, 0)),
            pl.BlockSpec((D_MODEL, 9 * gl), lambda i: (0, 0)),
            tab_spec, tab_spec, tab_spec,
        ],
        out_specs=out_specs,
        out_shape=out_shapes,
        scratch_shapes=[pltpu.VMEM((2 * nc, tm, LANES), F32)],
        compiler_params=_params("parallel"),
        name="a_qkv",
    )(x, g, w, *tabs)


def _a_attn_kernel(q_ref, k_ref, v_ref, o_ref, lse_ref, *, seg, dil):
    r = pl.program_id(1)
    tq = BAND_TQ
    kw = min(seg, tq + 2 * A_BAND_HALF)
    lane = _lane_iota()
    low = lane < HEAD_DIM

    def tile(t, carry):
        q0 = pl.multiple_of(t * tq, tq)
        ws = pl.multiple_of(jnp.clip(q0 - A_BAND_HALF, 0, seg - kw), A_BAND_HALF)
        rows = q0 + lax.broadcasted_iota(jnp.int32, (tq, 1), 0)
        cols = ws + lax.broadcasted_iota(jnp.int32, (1, kw), 1)
        valid = jnp.abs(cols - rows) <= A_BAND_HALF
        for c in range(A_GROUP_CHUNKS):
            qc = q_ref[0, 0, c, pl.ds(q0, tq), :] * jnp.asarray(HEAD_DIM ** -0.5, BF16)
            kc = k_ref[0, 0, c, pl.ds(ws, kw), :]
            vc = v_ref[0, 0, c, pl.ds(ws, kw), :]
            outs, lses = [], []
            n_heads = 2 if 2 * c + 1 < A_GROUP_HEADS else 1
            for hh in range(n_heads):
                qm = jnp.where(low if hh == 0 else jnp.logical_not(low), qc, jnp.zeros_like(qc))
                s = jnp.where(valid, _dot_nt(qm, kc), NEG_INF)
                m, e, l = _softmax_parts(s)
                outs.append(_dot(e.astype(BF16), vc) * (1.0 / l))
                lses.append(m + jnp.log(l))
            if n_heads == 2:
                o = jnp.where(low, outs[0], outs[1])
                ls = jnp.where(low, lses[0], lses[1])
            else:
                o = jnp.where(low, outs[0], 0.0)
                ls = jnp.where(low, lses[0], 0.0)
            if dil == 1:
                dst = pl.ds(q0, tq)
            else:
                dst = pl.ds(r + dil * q0, tq, stride=dil)
            o_ref[0, c, dst, :] = o
            lse_ref[0, c, dst, :] = jnp.broadcast_to(ls, o.shape)
        return carry

    lax.fori_loop(0, seg // tq, tile, 0)


def _a_attn(q, k, v, dil):
    seg = SEQ // dil
    nc = A_GROUP_CHUNKS
    in_spec = pl.BlockSpec((1, 1, nc, seg, LANES), lambda b, r: (b, r, 0, 0, 0))
    out_spec = pl.BlockSpec((1, nc, SEQ, LANES), lambda b, r: (b, 0, 0, 0))
    out_shape = jax.ShapeDtypeStruct((BATCH, nc, SEQ, LANES), F32)
    return pl.pallas_call(
        functools.partial(_a_attn_kernel, seg=seg, dil=dil),
        grid=(BATCH, dil),
        in_specs=[in_spec, in_spec, in_spec],
        out_specs=[out_spec, out_spec],
        out_shape=[out_shape, out_shape],
        compiler_params=_params("parallel", "arbitrary"),
        name=f"a_attn_d{dil}",
    )(q, k, v)


def _a_combine_kernel(o0, o1, o2, l0, l1, l2, out_ref):
    for c in range(A_GROUP_CHUNKS):
        ls = [l0[0, c], l1[0, c], l2[0, c]]
        m = jnp.maximum(jnp.maximum(ls[0], ls[1]), ls[2])
        es = [jnp.exp(l - m) for l in ls]
        inv = 1.0 / (es[0] + es[1] + es[2])
        for g, o in enumerate((o0, o1, o2)):
            col = g * A_GROUP_LANES + c * LANES
            out_ref[:, col:col + LANES] = (o[0, c] * (es[g] * inv)).astype(BF16)


def _a_combine(os_, ls_):
    tm = PROJ_TM
    nb = SEQ // tm
    n = len(DIL_PAIRS) * A_GROUP_LANES
    part = pl.BlockSpec((1, A_GROUP_CHUNKS, tm, LANES), lambda i: (i // nb, 0, i % nb, 0))
    return pl.pallas_call(
        _a_combine_kernel,
        grid=(TOKENS // tm,),
        in_specs=[part] * 6,
        out_specs=pl.BlockSpec((tm, n), lambda i: (i, 0)),
        out_shape=jax.ShapeDtypeStruct((TOKENS, n), BF16),
        compiler_params=_params("parallel"),
        name="a_combine",
    )(*os_, *ls_)


def _b_proj_kernel(x_ref, g_ref, win_ref, qn_ref, kvn_ref, wuq_ref, wuk_ref, wuvt_ref,
                   c_ref, sa_ref, sb_ref, q_out, k_out, vt_out):
    c, sa, sb = c_ref[...], sa_ref[...], sb_ref[...]
    xn = _rms_f32(x_ref[...], g_ref[...]).astype(BF16)
    z = _dot(xn, win_ref[...])
    cq = _rms_f32(z[:, :B_Q_RANK], qn_ref[...]).astype(BF16)
    ckv = _rms_f32(z[:, B_Q_RANK:B_Q_RANK + B_KV_RANK], kvn_ref[...]).astype(BF16)
    k_rope = _rope_chunk(z[:, B_Q_RANK + B_KV_RANK:], c, sa, sb, B_ROPE // 2)
    q = _dot(cq, wuq_ref[...])
    k = _dot(ckv, wuk_ref[...])
    for h in range(B_HEADS):
        sl = slice(h * LANES, (h + 1) * LANES)
        q_out[:, sl] = _rope_chunk(q[:, sl], c, sa, sb, B_ROPE // 2).astype(BF16)
        k_out[:, sl] = (k[:, sl] + k_rope).astype(BF16)
    vt_out[...] = _dot_nt(wuvt_ref[...], ckv).astype(BF16)


def _b_proj(x, g, win, qn, kvn, wuq, wuk, wuvt, tabs):
    tm = PROJ_TM
    nb = SEQ // tm
    tab_spec = pl.BlockSpec((tm, LANES), lambda i: (i % nb, 0))

    def full(a):
        return pl.BlockSpec(a.shape, lambda i: (0,) * a.ndim)

    nqk = B_HEADS * LANES
    nv = B_HEADS * B_V
    return pl.pallas_call(
        _b_proj_kernel,
        grid=(TOKENS // tm,),
        in_specs=[pl.BlockSpec((tm, D_MODEL), lambda i: (i, 0)), full(g), full(win), full(qn),
                  full(kvn), full(wuq), full(wuk), full(wuvt), tab_spec, tab_spec, tab_spec],
        out_specs=[pl.BlockSpec((tm, nqk), lambda i: (i, 0)),
                   pl.BlockSpec((tm, nqk), lambda i: (i, 0)),
                   pl.BlockSpec((nv, tm), lambda i: (0, i))],
        out_shape=[jax.ShapeDtypeStruct((TOKENS, nqk), BF16),
                   jax.ShapeDtypeStruct((TOKENS, nqk), BF16),
                   jax.ShapeDtypeStruct((nv, TOKENS), BF16)],
        compiler_params=_params("parallel"),
        name="b_proj",
    )(x, g, win, qn, kvn, wuq, wuk, wuvt, *tabs)


def _softmax_keys_major(st, c):
    m = jnp.max(st, axis=0, keepdims=True)
    e = jnp.exp2((st - m) * c)
    return e, jnp.sum(e, axis=0, keepdims=True)


def _attend_keys_major(streams, c, n_chunks):
    items = [(j, s) for j in range(n_chunks) for s in range(len(streams))]
    state = [None] * len(streams)

    def scores(item):
        j, s = item
        k_chunk, _, qw = streams[s]
        return _dot_nt(k_chunk(j), qw)

    def absorb(item, st):
        j, s = item
        vt_chunk = streams[s][1]
        mj = jnp.max(st, axis=0, keepdims=True)
        if j == 0:
            e = jnp.exp2((st - mj) * c)
            state[s] = (mj, jnp.sum(e, axis=0, keepdims=True), _dot(vt_chunk(j), e.astype(BF16)))
        else:
            m, l, acc = state[s]
            m_new = jnp.maximum(m, mj)
            a = jnp.exp2((m - m_new) * c)
            e = jnp.exp2((st - m_new) * c)
            state[s] = (m_new, l * a + jnp.sum(e, axis=0, keepdims=True),
                        acc * a + _dot(vt_chunk(j), e.astype(BF16)))

    st = scores(items[0])
    for idx, item in enumerate(items):
        st_next = scores(items[idx + 1]) if idx + 1 < len(items) else None
        absorb(item, st)
        st = st_next
    return [(acc, l) for _, l, acc in state]


def _b_attn_kernel(q_ref, k_ref, vt_ref, o_ref, *, c):
    kc = ATTN_KC

    def stream(hh):
        sl = slice(hh * LANES, (hh + 1) * LANES)
        rows = slice(hh * B_V, (hh + 1) * B_V)
        return (lambda j: k_ref[j * kc:(j + 1) * kc, sl],
                lambda j: vt_ref[rows, j * kc:(j + 1) * kc],
                q_ref[:, sl])

    outs = [acc * (1.0 / l) for acc, l in _attend_keys_major([stream(0), stream(1)], c, SEQ // kc)]
    o_ref[...] = jnp.concatenate(outs, axis=0).T.astype(BF16)


def _b_attn(q, k, vt):
    tq = ATTN_TQ
    nq = SEQ // tq
    return pl.pallas_call(
        functools.partial(_b_attn_kernel, c=(B_NOPE + B_ROPE) ** -0.5 * LOG2_E),
        grid=(BATCH, B_HEADS // 2, nq),
        in_specs=[
            pl.BlockSpec((tq, 2 * LANES), lambda b, h, i: (b * nq + i, h)),
            pl.BlockSpec((SEQ, 2 * LANES), lambda b, h, i: (b, h)),
            pl.BlockSpec((2 * B_V, SEQ), lambda b, h, i: (h, b)),
        ],
        out_specs=pl.BlockSpec((tq, LANES), lambda b, h, i: (b * nq + i, h)),
        out_shape=jax.ShapeDtypeStruct((TOKENS, B_HEADS * B_V), BF16),
        compiler_params=_params("parallel", "parallel", "arbitrary"),
        name="b_attn",
    )(q, k, vt)


def _c_group_geometry(gi):
    first_row = gi * C_ROW_GROUP
    wrow = min(max(first_row - NA_ROWS // 2, 0), GRID_ROWS - C_WIN_ROWS)
    n_groups = GRID_ROWS // C_ROW_GROUP
    case = 0 if gi == 0 else (2 if gi == n_groups - 1 else 1)
    return first_row, wrow, case


def _c_row_offset_slot(gi, kr, i):
    first_row, wrow, _ = _c_group_geometry(gi)
    qrow, krow = first_row + i, wrow + kr
    rs = min(max(qrow - NA_ROWS // 2, 0), GRID_ROWS - NA_ROWS)
    if rs <= krow < rs + NA_ROWS:
        return krow - qrow + NA_ROWS - 1
    return 2 * NA_ROWS - 1


def _c_attn_kernel(q_ref, k_ref, vt_ref, left_ref, right_ref, o_ref, bias_ref):
    low = _lane_iota() < HEAD_DIM
    nq = C_ROW_GROUP * GRID_W
    nk = C_WIN_ROWS * GRID_W
    n_groups = GRID_ROWS // C_ROW_GROUP

    @pl.when(pl.program_id(1) == 0)
    def _():
        for case, gi in enumerate((0, 1, n_groups - 1)):
            for kr in range(C_WIN_ROWS):
                for hh in range(2):
                    for ip in range(C_ROW_GROUP // 2):
                        a0 = _c_row_offset_slot(gi, kr, 2 * ip)
                        a1 = _c_row_offset_slot(gi, kr, 2 * ip + 1)
                        col = (hh * C_ROW_GROUP // 2 + ip) * LANES
                        bias_ref[case, kr * GRID_W:(kr + 1) * GRID_W, col:col + LANES] = (
                            left_ref[0, hh, a0] + right_ref[0, hh, a1])

    for gi in range(n_groups):
        first_row, wrow, case = _c_group_geometry(gi)
        q0, ws = first_row * GRID_W, wrow * GRID_W
        qc = q_ref[q0:q0 + nq, :] * jnp.asarray(HEAD_DIM ** -0.5, BF16)
        zero = jnp.zeros_like(qc)
        qq = jnp.concatenate([jnp.where(low, qc, zero), jnp.where(low, zero, qc)], axis=0)
        st = _dot_nt(k_ref[ws:ws + nk, :], qq) + bias_ref[case]
        e, l = _softmax_keys_major(st, LOG2_E)
        ot = _dot(vt_ref[:, ws:ws + nk], e.astype(BF16)) * (1.0 / l)
        o = jnp.concatenate([ot[:HEAD_DIM, :nq], ot[HEAD_DIM:, nq:]], axis=0)
        o_ref[q0:q0 + nq, :] = o.T.astype(BF16)


def _c_attn(qk, vt, left, right):
    nq = C_ROW_GROUP * GRID_W
    nk = C_WIN_ROWS * GRID_W
    npair = C_HEADS // 2
    tab_spec = pl.BlockSpec((1, 2, 2 * NA_ROWS, GRID_W, LANES), lambda h, b: (h, 0, 0, 0, 0))
    return pl.pallas_call(
        _c_attn_kernel,
        grid=(npair, BATCH),
        in_specs=[
            pl.BlockSpec((SEQ, LANES), lambda h, b: (b, h)),
            pl.BlockSpec((SEQ, LANES), lambda h, b: (b, npair + h)),
            pl.BlockSpec((LANES, SEQ), lambda h, b: (h, b)),
            tab_spec, tab_spec,
        ],
        out_specs=pl.BlockSpec((SEQ, LANES), lambda h, b: (b, h)),
        out_shape=jax.ShapeDtypeStruct((TOKENS, C_HEADS * HEAD_DIM), BF16),
        scratch_shapes=[pltpu.VMEM((3, nk, 2 * nq), F32)],
        compiler_params=_params("arbitrary", "arbitrary"),
        name="c_attn",
    )(qk, qk, vt, left, right)


def _c_column_tables(rpb):
    kc, qc = np.arange(GRID_W)[:, None], np.arange(GRID_W)[None, :]
    win0 = np.clip(qc - NA_COLS // 2, 0, GRID_W - NA_COLS)
    col_ok = (kc >= win0) & (kc < win0 + NA_COLS)
    col_off = np.clip(kc - qc + NA_COLS - 1, 0, 2 * NA_COLS - 2)
    col_sel = (col_off[None] == np.arange(2 * NA_COLS - 1)[:, None, None]).astype(np.float32)
    cols = jnp.einsum("hab,bkq->hakq", rpb, col_sel, precision=lax.Precision.HIGHEST)
    cols = jnp.where(col_ok, cols, NEG_INF)
    cols = jnp.concatenate([cols, jnp.full((C_HEADS, 1, GRID_W, GRID_W), NEG_INF, F32)], axis=1)
    cols = cols.reshape(C_HEADS // 2, 2, 2 * NA_ROWS, GRID_W, GRID_W)
    zeros = jnp.zeros_like(cols)
    return jnp.concatenate([cols, zeros], axis=-1), jnp.concatenate([zeros, cols], axis=-1)


def _d_attn_kernel(lq1, lk1, lq2, lk2, q_ref, k_ref, vt_ref, sub_ref, o_ref, *, c, lambda_init):
    lam = (jnp.exp(jnp.sum(lq1[...] * lk1[...], axis=-1, keepdims=True))
           - jnp.exp(jnp.sum(lq2[...] * lk2[...], axis=-1, keepdims=True)) + lambda_init)
    low = _lane_iota() < D_HEAD
    kc = ATTN_KC
    q = q_ref[...]
    zero = jnp.zeros_like(q)
    k_chunk = lambda j: k_ref[j * kc:(j + 1) * kc, :]
    vt_chunk = lambda j: vt_ref[:, j * kc:(j + 1) * kc]
    (acc1, l1), (acc2, l2) = _attend_keys_major(
        [(k_chunk, vt_chunk, jnp.where(low, q, zero)),
         (k_chunk, vt_chunk, jnp.where(low, zero, q))], c, SEQ // kc)
    o = (acc1 * (1.0 / l1) - lam * (acc2 * (1.0 / l2))).T
    o_ref[...] = (_rms_f32(o, sub_ref[...]) * (1.0 - lambda_init)).astype(BF16)


def _d_attn(qk, vt, lq1, lk1, lq2, lk2, subln, lambda_init):
    tq = ATTN_TQ
    nq = SEQ // tq
    vec = pl.BlockSpec((1, D_HEAD), lambda b, h, i: (0, 0))
    return pl.pallas_call(
        functools.partial(_d_attn_kernel, c=D_HEAD ** -0.5 * LOG2_E, lambda_init=lambda_init),
        grid=(BATCH, D_HEADS, nq),
        in_specs=[
            vec, vec, vec, vec,
            pl.BlockSpec((tq, LANES), lambda b, h, i: (b * nq + i, h)),
            pl.BlockSpec((SEQ, LANES), lambda b, h, i: (b, D_HEADS + h)),
            pl.BlockSpec((LANES, SEQ), lambda b, h, i: (h, b)),
            pl.BlockSpec((1, 2 * D_HEAD), lambda b, h, i: (0, 0)),
        ],
        out_specs=pl.BlockSpec((tq, LANES), lambda b, h, i: (b * nq + i, h)),
        out_shape=jax.ShapeDtypeStruct((TOKENS, 2 * D_HEADS * D_HEAD), BF16),
        compiler_params=_params("parallel", "parallel", "arbitrary"),
        name="d_attn",
    )(lq1, lk1, lq2, lk2, qk, qk, vt, subln)


def _rope_tables(rot_dim, period, lane0):
    r = rot_dim // 2
    inv = ROPE_THETA ** (-jnp.arange(0, rot_dim, 2, dtype=F32) / rot_dim)
    ang = jnp.arange(SEQ, dtype=F32)[:, None] * inv[None, :]
    cos, sin = jnp.cos(ang), jnp.sin(ang)
    lane = np.arange(LANES) % period - lane0
    first = (lane >= 0) & (lane < r)
    second = (lane >= r) & (lane < 2 * r)
    idx = np.where(first, lane, np.where(second, lane - r, 0))
    cg, sg = cos[:, idx], sin[:, idx]
    c = jnp.where(first | second, cg, 1.0)
    sa = jnp.where(first, -sg, 0.0)
    sb = jnp.where(second, sg, 0.0)
    return c, sa, sb


def _a_weights(w_qkv, w_o):
    nh = A_GROUP_HEADS * HEAD_DIM
    pad = jnp.zeros((D_MODEL, A_GROUP_LANES - nh), w_qkv.dtype)

    def cols(part, g):
        base = part * A_HEADS * HEAD_DIM + g * nh
        return [w_qkv[:, base:base + nh], pad]

    pieces = []
    for g in range(len(DIL_PAIRS)):
        pieces += cols(0, g) + cols(1, g)
    for g in range(len(DIL_PAIRS)):
        pieces += cols(2, g)
    w = jnp.concatenate(pieces, axis=1).astype(BF16)
    wo = w_o.reshape(len(DIL_PAIRS), nh, D_MODEL)
    wo = jnp.pad(wo, ((0, 0), (0, A_GROUP_LANES - nh), (0, 0))).astype(BF16)
    return w, wo.reshape(len(DIL_PAIRS) * A_GROUP_LANES, D_MODEL)


def _b_weights(w_in, w_uq, w_ukv):
    split = B_Q_RANK + B_KV_RANK
    win = jnp.concatenate([
        w_in[:, :split], jnp.zeros((D_MODEL, B_NOPE), w_in.dtype), w_in[:, split:],
        jnp.zeros((D_MODEL, LANES - B_NOPE - B_ROPE), w_in.dtype)], axis=1).astype(BF16)
    wuq = w_uq.reshape(B_Q_RANK, B_HEADS, B_NOPE + B_ROPE)
    wuq = jnp.pad(wuq, ((0, 0), (0, 0), (0, LANES - B_NOPE - B_ROPE)))
    wuq = wuq.reshape(B_Q_RANK, B_HEADS * LANES).astype(BF16)
    wukv = w_ukv.reshape(B_KV_RANK, B_HEADS, B_NOPE + B_V)
    wuk = jnp.pad(wukv[:, :, :B_NOPE], ((0, 0), (0, 0), (0, LANES - B_NOPE)))
    wuk = wuk.reshape(B_KV_RANK, B_HEADS * LANES).astype(BF16)
    wuvt = wukv[:, :, B_NOPE:].reshape(B_KV_RANK, B_HEADS * B_V).T.astype(BF16)
    return win, wuq, wuk, wuvt


def _split_qk_vt(w_qkv, n_qk):
    return w_qkv[:, :n_qk].astype(BF16), w_qkv[:, n_qk:].T.astype(BF16)


def kernel(x, p, a_norm, a_w_qkv, a_w_o, b_norm, b_w_in, b_q_norm, b_w_uq, b_kv_norm, b_w_ukv, b_w_o, c_norm, c_w_qkv, c_rpb, c_w_o, d_norm, d_w_qkv, d_lambda_q1, d_lambda_k1, d_lambda_q2, d_lambda_k2, d_subln, d_w_o, mlp_norm, w_up, w_down, ple_norm, w_ple_gate, w_ple_proj, final_norm):
    tabs_p = _rope_tables(ROT_DIM, HEAD_DIM, 0)
    tabs_l = _rope_tables(B_ROPE, LANES, B_NOPE)
    h = x.reshape(TOKENS, D_MODEL)
    row = lambda v: v.reshape(1, -1)

    for i in range(DEPTH):
        rest = (row(mlp_norm[i]), w_up[i].astype(BF16), w_down[i].astype(BF16),
                row(ple_norm[i]), w_ple_gate[i].astype(BF16), p[i].reshape(TOKENS, PLE_DIM),
                w_ple_proj[i].astype(BF16), row(final_norm), i == DEPTH - 1)
        if i == 0:
            w, wo = _a_weights(a_w_qkv[0], a_w_o[0])
            parts = _a_qkv(h, row(a_norm[0]), w, tabs_p)
            os_, ls_ = [], []
            for g, (_, dil) in enumerate(DIL_PAIRS):
                o, lse = _a_attn(parts[3 * g], parts[3 * g + 1], parts[3 * g + 2], dil)
                os_.append(o)
                ls_.append(lse)
            h = _tail_plain(h, _a_combine(os_, ls_), wo, *rest, name="a_tail")
        elif i == 1:
            win, wuq, wuk, wuvt = _b_weights(b_w_in[0], b_w_uq[0], b_w_ukv[0])
            q, k, vt = _b_proj(h, row(b_norm[0]), win, row(b_q_norm[0]), row(b_kv_norm[0]),
                               wuq, wuk, wuvt, tabs_l)
            h = _tail_plain(h, _b_attn(q, k, vt), b_w_o[0].astype(BF16), *rest, name="b_tail")
        elif i == 2:
            w, wvt = _split_qk_vt(c_w_qkv[0], 2 * C_HEADS * HEAD_DIM)
            qk, vt = _norm_proj(h, row(c_norm[0]), w, wvt, tabs_p, False, "c_qkv")
            o = _c_attn(qk, vt, *_c_column_tables(c_rpb[0]))
            h = _tail_plain(h, o, c_w_o[0].astype(BF16), *rest, name="c_tail")
        else:
            lambda_init = 0.8 - 0.6 * math.exp(-0.3 * i)
            w, wvt = _split_qk_vt(d_w_qkv[0], 2 * 2 * D_HEADS * D_HEAD)
            qk, vt = _norm_proj(h, row(d_norm[0]), w, wvt, tabs_p, True, "d_qkv")
            o = _d_attn(qk, vt, row(d_lambda_q1[0]), row(d_lambda_k1[0]), row(d_lambda_q2[0]),
                        row(d_lambda_k2[0]), row(d_subln[0]), lambda_init)
            h = _tail_plain(h, o, d_w_o[0].astype(BF16), *rest, name="d_tail")
    return h.reshape(BATCH, SEQ, D_MODEL)
```

```python
import functools
import math

import numpy as np
import jax
import jax.numpy as jnp
from jax import lax
from jax.experimental import pallas as pl
from jax.experimental.pallas import tpu as pltpu

F32 = jnp.float32
BF16 = jnp.bfloat16

D_MODEL = 1024
BATCH = 8
SEQ = 2048
DEPTH = 4
TOKENS = BATCH * SEQ
HEAD_DIM = 64
ROPE_THETA = 500000.0
ROT_DIM = HEAD_DIM // 4
NEG_INF = -1e30
RMS_EPS = 1e-6
LOG2_E = math.log2(math.e)

DIL_PAIRS = ((128, 1), (512, 4), (2048, 16))
A_GROUP_HEADS = 5
A_HEADS = A_GROUP_HEADS * len(DIL_PAIRS)
A_BAND_HALF = 64
A_GROUP_LANES = 384
A_GROUP_CHUNKS = A_GROUP_LANES // 128

B_HEADS = 16
B_Q_RANK = 256
B_KV_RANK = 128
B_NOPE = 64
B_ROPE = 32
B_V = 64

C_HEADS = 16
GRID_W = 64
GRID_ROWS = SEQ // GRID_W
NA_ROWS = 8
NA_COLS = 16
C_ROW_GROUP = 4
C_WIN_ROWS = 12

D_HEADS = 8
D_HEAD = 64

MLP_HIDDEN = 4 * D_MODEL
PLE_DIM = 256

LANES = 128
VMEM_LIMIT = 48 * 1024 * 1024

PROJ_TM = 512
MLP_TH = 1024
ATTN_TQ = 512
ATTN_KC = 512
ATTN_TILES = 2

B_Q_SCALE = (B_NOPE + B_ROPE) ** -0.5 * LOG2_E
CD_Q_SCALE = HEAD_DIM ** -0.5 * LOG2_E
BAND_TQ = 128


def _params(*sem):
    return pltpu.CompilerParams(dimension_semantics=sem, vmem_limit_bytes=VMEM_LIMIT)


def _rms_f32(x, g):
    ms = jnp.mean(x * x, axis=-1, keepdims=True)
    return x * lax.rsqrt(ms + RMS_EPS) * g


def _rope_chunk(y, c, sa, sb, shift):
    return (y * c + pltpu.roll(y, LANES - shift, 1) * sa + pltpu.roll(y, shift, 1) * sb)


def _softmax_parts(s):
    m = jnp.max(s, axis=-1, keepdims=True)
    e = jnp.exp(s - m)
    l = jnp.sum(e, axis=-1, keepdims=True)
    return m, e, l


def _dot(a, b):
    return jnp.dot(a, b, preferred_element_type=F32)


def _dot_nt(a, b):
    return lax.dot_general(a, b, (((1,), (1,)), ((), ())), preferred_element_type=F32)


def _lane_iota():
    return lax.broadcasted_iota(jnp.int32, (1, LANES), 1)


def _norm_proj_kernel(x_ref, g_ref, w_ref, wvt_ref, c_ref, sa_ref, sb_ref, qk_ref, vt_ref,
                      *, rope, tn):
    xn = _rms_f32(x_ref[...], g_ref[...]).astype(BF16)
    n = qk_ref.shape[1]
    c, sa, sb = c_ref[...], sa_ref[...], sb_ref[...]
    for j in range(n // tn):
        y = _dot(xn, w_ref[:, j * tn:(j + 1) * tn])
        for i in range(tn // LANES):
            col = j * tn + i * LANES
            chunk = y[:, i * LANES:(i + 1) * LANES]
            if rope:
                chunk = _rope_chunk(chunk, c, sa, sb, ROT_DIM // 2)
            if col < n // 2:
                chunk = chunk * CD_Q_SCALE
            qk_ref[:, col:col + LANES] = chunk.astype(BF16)
    vt_ref[...] = _dot_nt(wvt_ref[...], xn).astype(BF16)


def _norm_proj(x, g, w, wvt, tabs, rope, name):
    n = w.shape[1]
    nv = wvt.shape[0]
    tm = PROJ_TM
    nb = SEQ // tm
    tab_spec = pl.BlockSpec((tm, LANES), lambda i: (i % nb, 0))
    return pl.pallas_call(
        functools.partial(_norm_proj_kernel, rope=rope, tn=512),
        grid=(TOKENS // tm,),
        in_specs=[
            pl.BlockSpec((tm, D_MODEL), lambda i: (i, 0)),
            pl.BlockSpec((1, D_MODEL), lambda i: (0, 0)),
            pl.BlockSpec((D_MODEL, n), lambda i: (0, 0)),
            pl.BlockSpec((nv, D_MODEL), lambda i: (0, 0)),
            tab_spec, tab_spec, tab_spec,
        ],
        out_specs=[pl.BlockSpec((tm, n), lambda i: (i, 0)),
                   pl.BlockSpec((nv, tm), lambda i: (0, i))],
        out_shape=[jax.ShapeDtypeStruct((TOKENS, n), BF16),
                   jax.ShapeDtypeStruct((nv, TOKENS), BF16)],
        compiler_params=_params("parallel"),
        name=name,
    )(x, g, w, wvt, *tabs)


def _mixer_residual(x_ref, o_ref, w_ref):
    return x_ref[...] + _dot(o_ref[...], w_ref[...])


def _tail_kernel(*refs, n_front, front, final):
    front_refs = refs[:n_front]
    gm_ref, wu_ref, wd_ref, gp_ref, wg_ref, p_ref, wp_ref, fg_ref, out_ref = refs[n_front:]
    x1 = front(*front_refs)
    xn = _rms_f32(x1, gm_ref[...]).astype(BF16)
    acc = x1
    for j in range(MLP_HIDDEN // MLP_TH):
        h = _dot(xn, wu_ref[:, j * MLP_TH:(j + 1) * MLP_TH])
        h = jnp.square(jnp.maximum(h, 0.0)).astype(BF16)
        acc = acc + _dot(h, wd_ref[j * MLP_TH:(j + 1) * MLP_TH, :])
    gate = jax.nn.sigmoid(_dot(_rms_f32(acc, gp_ref[...]).astype(BF16), wg_ref[...]))
    y = acc + gate * _dot(p_ref[...].astype(BF16), wp_ref[...])
    if final:
        y = _rms_f32(y, fg_ref[...])
    out_ref[...] = y


def _resident(a):
    return pl.BlockSpec(a.shape, lambda i: (0,) * a.ndim, pipeline_mode=pl.Buffered(1))


def _tail(front, front_args, front_specs, gm, wu, wd, gp, wg, p, wp, fg, final, name):
    tm = PROJ_TM
    shared = (gm, wu, wd, gp, wg)
    return pl.pallas_call(
        functools.partial(_tail_kernel, n_front=len(front_args), front=front, final=final),
        grid=(TOKENS // tm,),
        in_specs=list(front_specs) + [_resident(a) for a in shared] + [
            pl.BlockSpec((tm, PLE_DIM), lambda i: (i, 0)), _resident(wp), _resident(fg)],
        out_specs=pl.BlockSpec((tm, D_MODEL), lambda i: (i, 0)),
        out_shape=jax.ShapeDtypeStruct((TOKENS, D_MODEL), F32),
        compiler_params=_params("parallel"),
        name=name,
    )(*front_args, *shared, p, wp, fg)


def _tail_plain(x, o, w_o, *rest, name):
    tm = PROJ_TM
    specs = [pl.BlockSpec((tm, D_MODEL), lambda i: (i, 0)),
             pl.BlockSpec((tm, o.shape[1]), lambda i: (i, 0)), _resident(w_o)]
    return _tail(_mixer_residual, (x, o, w_o), specs, *rest, name=name)


def _a_qkv_kernel(x_ref, g_ref, w_ref, c_ref, sa_ref, sb_ref, *refs):
    outs, stage = refs[:9], refs[9]
    tm = x_ref.shape[0]
    gl = A_GROUP_LANES
    nc = A_GROUP_CHUNKS
    xn = _rms_f32(x_ref[...], g_ref[...]).astype(BF16)
    c, sa, sb = c_ref[...], sa_ref[...], sb_ref[...]

    def emit(dil, dsts):
        n = tm // dil
        for k, dst in enumerate(dsts):
            for ci in range(nc):
                for r in range(dil):
                    if dil == 1:
                        rows = stage[k * nc + ci]
                    else:
                        rows = stage[k * nc + ci, pl.ds(r, n, stride=dil), :]
                    dst[0, r, ci] = rows.astype(BF16)

    for g, (_, dil) in enumerate(DIL_PAIRS):
        y = _dot(xn, w_ref[:, 2 * gl * g:2 * gl * (g + 1)])
        for i in range(2 * nc):
            stage[i] = _rope_chunk(y[:, i * LANES:(i + 1) * LANES], c, sa, sb, ROT_DIM // 2)
        emit(dil, (outs[3 * g], outs[3 * g + 1]))
    for g, (_, dil) in enumerate(DIL_PAIRS):
        y = _dot(xn, w_ref[:, 6 * gl + gl * g:6 * gl + gl * (g + 1)])
        for i in range(nc):
            stage[i] = y[:, i * LANES:(i + 1) * LANES]
        emit(dil, (outs[3 * g + 2],))


def _a_qkv(x, g, w, tabs):
    tm = PROJ_TM
    nb = SEQ // tm
    gl = A_GROUP_LANES
    tab_spec = pl.BlockSpec((tm, LANES), lambda i: (i % nb, 0))
    out_shapes, out_specs = [], []
    nc = A_GROUP_CHUNKS
    for _, dil in DIL_PAIRS:
        for _ in range(3):
            out_shapes.append(jax.ShapeDtypeStruct((BATCH, dil, nc, SEQ // dil, LANES), BF16))
            out_specs.append(pl.BlockSpec((1, dil, nc, tm // dil, LANES),
                                          lambda i: (i // nb, 0, 0, i % nb, 0)))
    return pl.pallas_call(
        _a_qkv_kernel,
        grid=(TOKENS // tm,),
        in_specs=[
            pl.BlockSpec((tm, D_MODEL), lambda i: (i, 0)),
            pl.BlockSpec((1, D_MODEL), lambda i: (0, 0)),
            pl.BlockSpec((D_MODEL, 9 * gl), lambda i: (0, 0)),
            tab_spec, tab_spec, tab_spec,
        ],
        out_specs=out_specs,
        out_shape=out_shapes,
        scratch_shapes=[pltpu.VMEM((2 * nc, tm, LANES), F32)],
        compiler_params=_params("parallel"),
        name="a_qkv",
    )(x, g, w, *tabs)


def _a_attn_kernel(q_ref, k_ref, v_ref, o_ref, lse_ref, *, seg, dil):
    r = pl.program_id(1)
    tq = BAND_TQ
    kw = min(seg, tq + 2 * A_BAND_HALF)
    lane = _lane_iota()
    low = lane < HEAD_DIM

    def tile(t, carry):
        q0 = pl.multiple_of(t * tq, tq)
        ws = pl.multiple_of(jnp.clip(q0 - A_BAND_HALF, 0, seg - kw), A_BAND_HALF)
        rows = q0 + lax.broadcasted_iota(jnp.int32, (tq, 1), 0)
        cols = ws + lax.broadcasted_iota(jnp.int32, (1, kw), 1)
        valid = jnp.abs(cols - rows) <= A_BAND_HALF
        for c in range(A_GROUP_CHUNKS):
            qc = q_ref[0, 0, c, pl.ds(q0, tq), :] * jnp.asarray(HEAD_DIM ** -0.5, BF16)
            kc = k_ref[0, 0, c, pl.ds(ws, kw), :]
            vc = v_ref[0, 0, c, pl.ds(ws, kw), :]
            outs, lses = [], []
            n_heads = 2 if 2 * c + 1 < A_GROUP_HEADS else 1
            for hh in range(n_heads):
                qm = jnp.where(low if hh == 0 else jnp.logical_not(low), qc, jnp.zeros_like(qc))
                s = jnp.where(valid, _dot_nt(qm, kc), NEG_INF)
                m, e, l = _softmax_parts(s)
                outs.append(_dot(e.astype(BF16), vc) * (1.0 / l))
                lses.append(m + jnp.log(l))
            if n_heads == 2:
                o = jnp.where(low, outs[0], outs[1])
                ls = jnp.where(low, lses[0], lses[1])
            else:
                o = jnp.where(low, outs[0], 0.0)
                ls = jnp.where(low, lses[0], 0.0)
            if dil == 1:
                dst = pl.ds(q0, tq)
            else:
                dst = pl.ds(r + dil * q0, tq, stride=dil)
            o_ref[0, c, dst, :] = o
            lse_ref[0, c, dst, :] = jnp.broadcast_to(ls, o.shape)
        return carry

    lax.fori_loop(0, seg // tq, tile, 0)


def _a_attn(q, k, v, dil):
    seg = SEQ // dil
    nc = A_GROUP_CHUNKS
    in_spec = pl.BlockSpec((1, 1, nc, seg, LANES), lambda b, r: (b, r, 0, 0, 0))
    out_spec = pl.BlockSpec((1, nc, SEQ, LANES), lambda b, r: (b, 0, 0, 0))
    out_shape = jax.ShapeDtypeStruct((BATCH, nc, SEQ, LANES), F32)
    return pl.pallas_call(
        functools.partial(_a_attn_kernel, seg=seg, dil=dil),
        grid=(BATCH, dil),
        in_specs=[in_spec, in_spec, in_spec],
        out_specs=[out_spec, out_spec],
        out_shape=[out_shape, out_shape],
        compiler_params=_params("parallel", "arbitrary"),
        name=f"a_attn_d{dil}",
    )(q, k, v)


def _a_combine_kernel(o0, o1, o2, l0, l1, l2, out_ref):
    for c in range(A_GROUP_CHUNKS):
        ls = [l0[0, c], l1[0, c], l2[0, c]]
        m = jnp.maximum(jnp.maximum(ls[0], ls[1]), ls[2])
        es = [jnp.exp(l - m) for l in ls]
        inv = 1.0 / (es[0] + es[1] + es[2])
        for g, o in enumerate((o0, o1, o2)):
            col = g * A_GROUP_LANES + c * LANES
            out_ref[:, col:col + LANES] = (o[0, c] * (es[g] * inv)).astype(BF16)


def _a_combine(os_, ls_):
    tm = PROJ_TM
    nb = SEQ // tm
    n = len(DIL_PAIRS) * A_GROUP_LANES
    part = pl.BlockSpec((1, A_GROUP_CHUNKS, tm, LANES), lambda i: (i // nb, 0, i % nb, 0))
    return pl.pallas_call(
        _a_combine_kernel,
        grid=(TOKENS // tm,),
        in_specs=[part] * 6,
        out_specs=pl.BlockSpec((tm, n), lambda i: (i, 0)),
        out_shape=jax.ShapeDtypeStruct((TOKENS, n), BF16),
        compiler_params=_params("parallel"),
        name="a_combine",
    )(*os_, *ls_)


def _b_proj_kernel(x_ref, g_ref, win_ref, qn_ref, kvn_ref, wuq_ref, wuk_ref, wuvt_ref,
                   c_ref, sa_ref, sb_ref, q_out, k_out, vt_out):
    c, sa, sb = c_ref[...], sa_ref[...], sb_ref[...]
    xn = _rms_f32(x_ref[...], g_ref[...]).astype(BF16)
    z = _dot(xn, win_ref[...])
    cq = _rms_f32(z[:, :B_Q_RANK], qn_ref[...]).astype(BF16)
    ckv = _rms_f32(z[:, B_Q_RANK:B_Q_RANK + B_KV_RANK], kvn_ref[...]).astype(BF16)
    k_rope = _rope_chunk(z[:, B_Q_RANK + B_KV_RANK:], c, sa, sb, B_ROPE // 2)
    q = _dot(cq, wuq_ref[...])
    k = _dot(ckv, wuk_ref[...])
    for h in range(B_HEADS):
        sl = slice(h * LANES, (h + 1) * LANES)
        q_out[:, sl] = (_rope_chunk(q[:, sl], c, sa, sb, B_ROPE // 2) * B_Q_SCALE).astype(BF16)
        k_out[:, sl] = (k[:, sl] + k_rope).astype(BF16)
    vt_out[...] = _dot_nt(wuvt_ref[...], ckv).astype(BF16)


def _b_proj(x, g, win, qn, kvn, wuq, wuk, wuvt, tabs):
    tm = PROJ_TM
    nb = SEQ // tm
    tab_spec = pl.BlockSpec((tm, LANES), lambda i: (i % nb, 0))

    def full(a):
        return pl.BlockSpec(a.shape, lambda i: (0,) * a.ndim)

    nqk = B_HEADS * LANES
    nv = B_HEADS * B_V
    return pl.pallas_call(
        _b_proj_kernel,
        grid=(TOKENS // tm,),
        in_specs=[pl.BlockSpec((tm, D_MODEL), lambda i: (i, 0)), full(g), full(win), full(qn),
                  full(kvn), full(wuq), full(wuk), full(wuvt), tab_spec, tab_spec, tab_spec],
        out_specs=[pl.BlockSpec((tm, nqk), lambda i: (i, 0)),
                   pl.BlockSpec((tm, nqk), lambda i: (i, 0)),
                   pl.BlockSpec((nv, tm), lambda i: (0, i))],
        out_shape=[jax.ShapeDtypeStruct((TOKENS, nqk), BF16),
                   jax.ShapeDtypeStruct((TOKENS, nqk), BF16),
                   jax.ShapeDtypeStruct((nv, TOKENS), BF16)],
        compiler_params=_params("parallel"),
        name="b_proj",
    )(x, g, win, qn, kvn, wuq, wuk, wuvt, *tabs)


def _with_ones_rows(vt):
    return jnp.concatenate([vt, jnp.ones((16, vt.shape[1]), vt.dtype)], axis=0)


def _attend_keys_major(streams, n_chunks, finish):
    items = []
    for s0 in range(0, len(streams), 2):
        for j in range(n_chunks):
            items += [(s, j) for s in (s0, s0 + 1) if s < len(streams)]
    state = [None] * len(streams)

    def scores(item):
        s, j = item
        k_chunk, _, qw = streams[s]
        return _dot_nt(k_chunk(j), qw)

    def absorb(item, st):
        s, j = item
        vt1 = _with_ones_rows(streams[s][1](j))
        mj = jnp.max(st, axis=0, keepdims=True)
        if j == 0:
            state[s] = (mj, _dot(vt1, jnp.exp2(st - mj).astype(BF16)))
        else:
            m, acc = state[s]
            m_new = jnp.maximum(m, mj)
            e = jnp.exp2(st - m_new).astype(BF16)
            state[s] = (m_new, acc * jnp.exp2(m - m_new) + _dot(vt1, e))
        if j == n_chunks - 1:
            finish(s, state[s][1])
            state[s] = None

    st = scores(items[0])
    for idx, item in enumerate(items):
        st_next = scores(items[idx + 1]) if idx + 1 < len(items) else None
        absorb(item, st)
        st = st_next


def _b_attn_kernel(q_ref, k_ref, vt_ref, o_ref):
    kc, tq = ATTN_KC, ATTN_TQ
    outs = {}

    def stream(t, hh):
        sl = slice(hh * LANES, (hh + 1) * LANES)
        rows = slice(hh * B_V, (hh + 1) * B_V)
        return (lambda j: k_ref[j * kc:(j + 1) * kc, sl],
                lambda j: vt_ref[rows, j * kc:(j + 1) * kc],
                q_ref[t * tq:(t + 1) * tq, sl])

    def finish(s, acc):
        t, hh = divmod(s, 2)
        outs[hh] = acc[:B_V] * (1.0 / acc[B_V:B_V + 1])
        if hh == 1:
            o = jnp.concatenate([outs[0], outs[1]], axis=0)
            o_ref[t * tq:(t + 1) * tq, :] = o.T.astype(BF16)

    _attend_keys_major([stream(t, hh) for t in range(ATTN_TILES) for hh in range(2)],
                       SEQ // kc, finish)


def _b_attn(q, k, vt):
    rows = ATTN_TQ * ATTN_TILES
    nq = SEQ // rows
    return pl.pallas_call(
        _b_attn_kernel,
        grid=(BATCH, B_HEADS // 2, nq),
        in_specs=[
            pl.BlockSpec((rows, 2 * LANES), lambda b, h, i: (b * nq + i, h)),
            pl.BlockSpec((SEQ, 2 * LANES), lambda b, h, i: (b, h)),
            pl.BlockSpec((2 * B_V, SEQ), lambda b, h, i: (h, b)),
        ],
        out_specs=pl.BlockSpec((rows, LANES), lambda b, h, i: (b * nq + i, h)),
        out_shape=jax.ShapeDtypeStruct((TOKENS, B_HEADS * B_V), BF16),
        compiler_params=_params("parallel", "parallel", "arbitrary"),
        name="b_attn",
    )(q, k, vt)


def _c_group_geometry(gi):
    first_row = gi * C_ROW_GROUP
    wrow = min(max(first_row - NA_ROWS // 2, 0), GRID_ROWS - C_WIN_ROWS)
    n_groups = GRID_ROWS // C_ROW_GROUP
    case = 0 if gi == 0 else (2 if gi == n_groups - 1 else 1)
    return first_row, wrow, case


def _c_row_offset_slot(gi, kr, i):
    first_row, wrow, _ = _c_group_geometry(gi)
    qrow, krow = first_row + i, wrow + kr
    rs = min(max(qrow - NA_ROWS // 2, 0), GRID_ROWS - NA_ROWS)
    if rs <= krow < rs + NA_ROWS:
        return krow - qrow + NA_ROWS - 1
    return 2 * NA_ROWS - 1


def _c_attn_kernel(q_ref, k_ref, vt_ref, left_ref, right_ref, o_ref, bias_ref):
    low = _lane_iota() < HEAD_DIM
    nq = C_ROW_GROUP * GRID_W
    nk = C_WIN_ROWS * GRID_W
    n_groups = GRID_ROWS // C_ROW_GROUP

    @pl.when(pl.program_id(1) == 0)
    def _():
        for case, gi in enumerate((0, 1, n_groups - 1)):
            for kr in range(C_WIN_ROWS):
                for hh in range(2):
                    for ip in range(C_ROW_GROUP // 2):
                        a0 = _c_row_offset_slot(gi, kr, 2 * ip)
                        a1 = _c_row_offset_slot(gi, kr, 2 * ip + 1)
                        col = (hh * C_ROW_GROUP // 2 + ip) * LANES
                        bias_ref[case, kr * GRID_W:(kr + 1) * GRID_W, col:col + LANES] = (
                            left_ref[0, hh, a0] + right_ref[0, hh, a1])

    for gi in range(n_groups):
        first_row, wrow, case = _c_group_geometry(gi)
        q0, ws = first_row * GRID_W, wrow * GRID_W
        qc = q_ref[q0:q0 + nq, :]
        zero = jnp.zeros_like(qc)
        qq = jnp.concatenate([jnp.where(low, qc, zero), jnp.where(low, zero, qc)], axis=0)
        st = _dot_nt(k_ref[ws:ws + nk, :], qq) + bias_ref[case]
        e = jnp.exp2(st - jnp.max(st, axis=0, keepdims=True)).astype(BF16)
        ot = _dot(_with_ones_rows(vt_ref[:, ws:ws + nk]), e)
        ot = ot[:LANES] * (1.0 / ot[LANES:LANES + 1])
        o = jnp.concatenate([ot[:HEAD_DIM, :nq], ot[HEAD_DIM:, nq:]], axis=0)
        o_ref[q0:q0 + nq, :] = o.T.astype(BF16)


def _c_attn(qk, vt, left, right):
    nq = C_ROW_GROUP * GRID_W
    nk = C_WIN_ROWS * GRID_W
    npair = C_HEADS // 2
    tab_spec = pl.BlockSpec((1, 2, 2 * NA_ROWS, GRID_W, LANES), lambda h, b: (h, 0, 0, 0, 0))
    return pl.pallas_call(
        _c_attn_kernel,
        grid=(npair, BATCH),
        in_specs=[
            pl.BlockSpec((SEQ, LANES), lambda h, b: (b, h)),
            pl.BlockSpec((SEQ, LANES), lambda h, b: (b, npair + h)),
            pl.BlockSpec((LANES, SEQ), lambda h, b: (h, b)),
            tab_spec, tab_spec,
        ],
        out_specs=pl.BlockSpec((SEQ, LANES), lambda h, b: (b, h)),
        out_shape=jax.ShapeDtypeStruct((TOKENS, C_HEADS * HEAD_DIM), BF16),
        scratch_shapes=[pltpu.VMEM((3, nk, 2 * nq), F32)],
        compiler_params=_params("arbitrary", "arbitrary"),
        name="c_attn",
    )(qk, qk, vt, left, right)


def _c_column_tables(rpb):
    kc, qc = np.arange(GRID_W)[:, None], np.arange(GRID_W)[None, :]
    win0 = np.clip(qc - NA_COLS // 2, 0, GRID_W - NA_COLS)
    col_ok = (kc >= win0) & (kc < win0 + NA_COLS)
    col_off = np.clip(kc - qc + NA_COLS - 1, 0, 2 * NA_COLS - 2)
    col_sel = (col_off[None] == np.arange(2 * NA_COLS - 1)[:, None, None]).astype(np.float32)
    cols = jnp.einsum("hab,bkq->hakq", rpb, col_sel, precision=lax.Precision.HIGHEST)
    cols = jnp.where(col_ok, cols * LOG2_E, NEG_INF)
    cols = jnp.concatenate([cols, jnp.full((C_HEADS, 1, GRID_W, GRID_W), NEG_INF, F32)], axis=1)
    cols = cols.reshape(C_HEADS // 2, 2, 2 * NA_ROWS, GRID_W, GRID_W)
    zeros = jnp.zeros_like(cols)
    return jnp.concatenate([cols, zeros], axis=-1), jnp.concatenate([zeros, cols], axis=-1)


def _d_attn_kernel(lq1, lk1, lq2, lk2, q_ref, k_ref, vt_ref, sub_ref, o_ref, *, lambda_init):
    lam = (jnp.exp(jnp.sum(lq1[...] * lk1[...], axis=-1, keepdims=True))
           - jnp.exp(jnp.sum(lq2[...] * lk2[...], axis=-1, keepdims=True)) + lambda_init)
    low = _lane_iota() < D_HEAD
    kc, tq, dv = ATTN_KC, ATTN_TQ, 2 * D_HEAD
    k_chunk = lambda j: k_ref[j * kc:(j + 1) * kc, :]
    vt_chunk = lambda j: vt_ref[:, j * kc:(j + 1) * kc]
    outs = {}

    def stream(t, half):
        q = q_ref[t * tq:(t + 1) * tq, :]
        zero = jnp.zeros_like(q)
        return (k_chunk, vt_chunk, jnp.where(low, q, zero) if half == 0 else jnp.where(low, zero, q))

    def finish(s, acc):
        t, half = divmod(s, 2)
        outs[half] = acc[:dv] * (1.0 / acc[dv:dv + 1])
        if half == 1:
            o = (outs[0] - lam * outs[1]).T
            o_ref[t * tq:(t + 1) * tq, :] = (
                _rms_f32(o, sub_ref[...]) * (1.0 - lambda_init)).astype(BF16)

    _attend_keys_major([stream(t, half) for t in range(ATTN_TILES) for half in range(2)],
                       SEQ // kc, finish)


def _d_attn(qk, vt, lq1, lk1, lq2, lk2, subln, lambda_init):
    tq = ATTN_TQ * ATTN_TILES
    nq = SEQ // tq
    vec = pl.BlockSpec((1, D_HEAD), lambda b, h, i: (0, 0))
    return pl.pallas_call(
        functools.partial(_d_attn_kernel, lambda_init=lambda_init),
        grid=(BATCH, D_HEADS, nq),
        in_specs=[
            vec, vec, vec, vec,
            pl.BlockSpec((tq, LANES), lambda b, h, i: (b * nq + i, h)),
            pl.BlockSpec((SEQ, LANES), lambda b, h, i: (b, D_HEADS + h)),
            pl.BlockSpec((LANES, SEQ), lambda b, h, i: (h, b)),
            pl.BlockSpec((1, 2 * D_HEAD), lambda b, h, i: (0, 0)),
        ],
        out_specs=pl.BlockSpec((tq, LANES), lambda b, h, i: (b * nq + i, h)),
        out_shape=jax.ShapeDtypeStruct((TOKENS, 2 * D_HEADS * D_HEAD), BF16),
        compiler_params=_params("parallel", "parallel", "arbitrary"),
        name="d_attn",
    )(lq1, lk1, lq2, lk2, qk, qk, vt, subln)


def _rope_tables(rot_dim, period, lane0):
    r = rot_dim // 2
    inv = ROPE_THETA ** (-jnp.arange(0, rot_dim, 2, dtype=F32) / rot_dim)
    ang = jnp.arange(SEQ, dtype=F32)[:, None] * inv[None, :]
    cos, sin = jnp.cos(ang), jnp.sin(ang)
    lane = np.arange(LANES) % period - lane0
    first = (lane >= 0) & (lane < r)
    second = (lane >= r) & (lane < 2 * r)
    idx = np.where(first, lane, np.where(second, lane - r, 0))
    cg, sg = cos[:, idx], sin[:, idx]
    c = jnp.where(first | second, cg, 1.0)
    sa = jnp.where(first, -sg, 0.0)
    sb = jnp.where(second, sg, 0.0)
    return c, sa, sb


def _a_weights(w_qkv, w_o):
    nh = A_GROUP_HEADS * HEAD_DIM
    pad = jnp.zeros((D_MODEL, A_GROUP_LANES - nh), w_qkv.dtype)

    def cols(part, g):
        base = part * A_HEADS * HEAD_DIM + g * nh
        return [w_qkv[:, base:base + nh], pad]

    pieces = []
    for g in range(len(DIL_PAIRS)):
        pieces += cols(0, g) + cols(1, g)
    for g in range(len(DIL_PAIRS)):
        pieces += cols(2, g)
    w = jnp.concatenate(pieces, axis=1).astype(BF16)
    wo = w_o.reshape(len(DIL_PAIRS), nh, D_MODEL)
    wo = jnp.pad(wo, ((0, 0), (0, A_GROUP_LANES - nh), (0, 0))).astype(BF16)
    return w, wo.reshape(len(DIL_PAIRS) * A_GROUP_LANES, D_MODEL)


def _b_weights(w_in, w_uq, w_ukv):
    split = B_Q_RANK + B_KV_RANK
    win = jnp.concatenate([
        w_in[:, :split], jnp.zeros((D_MODEL, B_NOPE), w_in.dtype), w_in[:, split:],
        jnp.zeros((D_MODEL, LANES - B_NOPE - B_ROPE), w_in.dtype)], axis=1).astype(BF16)
    wuq = w_uq.reshape(B_Q_RANK, B_HEADS, B_NOPE + B_ROPE)
    wuq = jnp.pad(wuq, ((0, 0), (0, 0), (0, LANES - B_NOPE - B_ROPE)))
    wuq = wuq.reshape(B_Q_RANK, B_HEADS * LANES).astype(BF16)
    wukv = w_ukv.reshape(B_KV_RANK, B_HEADS, B_NOPE + B_V)
    wuk = jnp.pad(wukv[:, :, :B_NOPE], ((0, 0), (0, 0), (0, LANES - B_NOPE)))
    wuk = wuk.reshape(B_KV_RANK, B_HEADS * LANES).astype(BF16)
    wuvt = wukv[:, :, B_NOPE:].reshape(B_KV_RANK, B_HEADS * B_V).T.astype(BF16)
    return win, wuq, wuk, wuvt


def _split_qk_vt(w_qkv, n_qk):
    return w_qkv[:, :n_qk].astype(BF16), w_qkv[:, n_qk:].T.astype(BF16)


def kernel(x, p, a_norm, a_w_qkv, a_w_o, b_norm, b_w_in, b_q_norm, b_w_uq, b_kv_norm, b_w_ukv, b_w_o, c_norm, c_w_qkv, c_rpb, c_w_o, d_norm, d_w_qkv, d_lambda_q1, d_lambda_k1, d_lambda_q2, d_lambda_k2, d_subln, d_w_o, mlp_norm, w_up, w_down, ple_norm, w_ple_gate, w_ple_proj, final_norm):
    tabs_p = _rope_tables(ROT_DIM, HEAD_DIM, 0)
    tabs_l = _rope_tables(B_ROPE, LANES, B_NOPE)
    h = x.reshape(TOKENS, D_MODEL)
    row = lambda v: v.reshape(1, -1)

    for i in range(DEPTH):
        rest = (row(mlp_norm[i]), w_up[i].astype(BF16), w_down[i].astype(BF16),
                row(ple_norm[i]), w_ple_gate[i].astype(BF16), p[i].reshape(TOKENS, PLE_DIM),
                w_ple_proj[i].astype(BF16), row(final_norm), i == DEPTH - 1)
        if i == 0:
            w, wo = _a_weights(a_w_qkv[0], a_w_o[0])
            parts = _a_qkv(h, row(a_norm[0]), w, tabs_p)
            os_, ls_ = [], []
            for g, (_, dil) in enumerate(DIL_PAIRS):
                o, lse = _a_attn(parts[3 * g], parts[3 * g + 1], parts[3 * g + 2], dil)
                os_.append(o)
                ls_.append(lse)
            h = _tail_plain(h, _a_combine(os_, ls_), wo, *rest, name="a_tail")
        elif i == 1:
            win, wuq, wuk, wuvt = _b_weights(b_w_in[0], b_w_uq[0], b_w_ukv[0])
            q, k, vt = _b_proj(h, row(b_norm[0]), win, row(b_q_norm[0]), row(b_kv_norm[0]),
                               wuq, wuk, wuvt, tabs_l)
            h = _tail_plain(h, _b_attn(q, k, vt), b_w_o[0].astype(BF16), *rest, name="b_tail")
        elif i == 2:
            w, wvt = _split_qk_vt(c_w_qkv[0], 2 * C_HEADS * HEAD_DIM)
            qk, vt = _norm_proj(h, row(c_norm[0]), w, wvt, tabs_p, False, "c_qkv")
            o = _c_attn(qk, vt, *_c_column_tables(c_rpb[0]))
            h = _tail_plain(h, o, c_w_o[0].astype(BF16), *rest, name="c_tail")
        else:
            lambda_init = 0.8 - 0.6 * math.exp(-0.3 * i)
            w, wvt = _split_qk_vt(d_w_qkv[0], 2 * 2 * D_HEADS * D_HEAD)
            qk, vt = _norm_proj(h, row(d_norm[0]), w, wvt, tabs_p, True, "d_qkv")
            o = _d_attn(qk, vt, row(d_lambda_q1[0]), row(d_lambda_k1[0]), row(d_lambda_q2[0]),
                        row(d_lambda_k2[0]), row(d_subln[0]), lambda_init)
            h = _tail_plain(h, o, d_w_o[0].astype(BF16), *rest, name="d_tail")
    return h.reshape(BATCH, SEQ, D_MODEL)
```

```python
import functools
import math

import numpy as np
import jax
import jax.numpy as jnp
from jax import lax
from jax.experimental import pallas as pl
from jax.experimental.pallas import tpu as pltpu

F32 = jnp.float32
BF16 = jnp.bfloat16

D_MODEL = 1024
BATCH = 8
SEQ = 2048
DEPTH = 4
TOKENS = BATCH * SEQ
HEAD_DIM = 64
ROPE_THETA = 500000.0
ROT_DIM = HEAD_DIM // 4
NEG_INF = -1e30
RMS_EPS = 1e-6
LOG2_E = math.log2(math.e)

DIL_PAIRS = ((128, 1), (512, 4), (2048, 16))
A_GROUP_HEADS = 5
A_HEADS = A_GROUP_HEADS * len(DIL_PAIRS)
A_BAND_HALF = 64
A_GROUP_LANES = 384
A_GROUP_CHUNKS = A_GROUP_LANES // 128

B_HEADS = 16
B_Q_RANK = 256
B_KV_RANK = 128
B_NOPE = 64
B_ROPE = 32
B_V = 64

C_HEADS = 16
GRID_W = 64
GRID_ROWS = SEQ // GRID_W
NA_ROWS = 8
NA_COLS = 16
C_ROW_GROUP = 4
C_WIN_ROWS = 12

D_HEADS = 8
D_HEAD = 64

MLP_HIDDEN = 4 * D_MODEL
PLE_DIM = 256

LANES = 128
VMEM_LIMIT = 48 * 1024 * 1024

PROJ_TM = 512
MLP_TH = 1024
ATTN_TQ = 512
ATTN_KC = 512
ATTN_TILES = 2

B_Q_SCALE = (B_NOPE + B_ROPE) ** -0.5 * LOG2_E
CD_Q_SCALE = HEAD_DIM ** -0.5 * LOG2_E
BAND_TQ = 128
BAND_UNROLL = 2


def _params(*sem):
    return pltpu.CompilerParams(dimension_semantics=sem, vmem_limit_bytes=VMEM_LIMIT)


def _rms_f32(x, g):
    ms = jnp.mean(x * x, axis=-1, keepdims=True)
    return x * lax.rsqrt(ms + RMS_EPS) * g


def _rope_chunk(y, c, sa, sb, shift):
    return (y * c + pltpu.roll(y, LANES - shift, 1) * sa + pltpu.roll(y, shift, 1) * sb)


def _softmax_parts(s):
    m = jnp.max(s, axis=-1, keepdims=True)
    e = jnp.exp(s - m)
    l = jnp.sum(e, axis=-1, keepdims=True)
    return m, e, l


def _dot(a, b):
    return jnp.dot(a, b, preferred_element_type=F32)


def _dot_nt(a, b):
    return lax.dot_general(a, b, (((1,), (1,)), ((), ())), preferred_element_type=F32)


def _lane_iota():
    return lax.broadcasted_iota(jnp.int32, (1, LANES), 1)


def _norm_proj_kernel(x_ref, g_ref, w_ref, wvt_ref, c_ref, sa_ref, sb_ref, qk_ref, vt_ref,
                      *, rope, tn):
    xn = _rms_f32(x_ref[...], g_ref[...]).astype(BF16)
    n = qk_ref.shape[1]
    c, sa, sb = c_ref[...], sa_ref[...], sb_ref[...]
    for j in range(n // tn):
        y = _dot(xn, w_ref[:, j * tn:(j + 1) * tn])
        for i in range(tn // LANES):
            col = j * tn + i * LANES
            chunk = y[:, i * LANES:(i + 1) * LANES]
            if rope:
                chunk = _rope_chunk(chunk, c, sa, sb, ROT_DIM // 2)
            if col < n // 2:
                chunk = chunk * CD_Q_SCALE
            qk_ref[:, col:col + LANES] = chunk.astype(BF16)
    vt_ref[...] = _dot_nt(wvt_ref[...], xn).astype(BF16)


def _norm_proj(x, g, w, wvt, tabs, rope, name):
    n = w.shape[1]
    nv = wvt.shape[0]
    tm = PROJ_TM
    nb = SEQ // tm
    tab_spec = pl.BlockSpec((tm, LANES), lambda i: (i % nb, 0))
    return pl.pallas_call(
        functools.partial(_norm_proj_kernel, rope=rope, tn=512),
        grid=(TOKENS // tm,),
        in_specs=[
            pl.BlockSpec((tm, D_MODEL), lambda i: (i, 0)),
            pl.BlockSpec((1, D_MODEL), lambda i: (0, 0)),
            pl.BlockSpec((D_MODEL, n), lambda i: (0, 0)),
            pl.BlockSpec((nv, D_MODEL), lambda i: (0, 0)),
            tab_spec, tab_spec, tab_spec,
        ],
        out_specs=[pl.BlockSpec((tm, n), lambda i: (i, 0)),
                   pl.BlockSpec((nv, tm), lambda i: (0, i))],
        out_shape=[jax.ShapeDtypeStruct((TOKENS, n), BF16),
                   jax.ShapeDtypeStruct((nv, TOKENS), BF16)],
        compiler_params=_params("parallel"),
        name=name,
    )(x, g, w, wvt, *tabs)


def _mixer_residual(x_ref, o_ref, w_ref):
    return x_ref[...] + _dot(o_ref[...], w_ref[...])


def _tail_kernel(*refs, n_front, front, final):
    front_refs = refs[:n_front]
    gm_ref, wu_ref, wd_ref, gp_ref, wg_ref, p_ref, wp_ref, fg_ref, out_ref = refs[n_front:]
    x1 = front(*front_refs)
    xn = _rms_f32(x1, gm_ref[...]).astype(BF16)
    acc = x1
    for j in range(MLP_HIDDEN // MLP_TH):
        h = _dot(xn, wu_ref[:, j * MLP_TH:(j + 1) * MLP_TH])
        h = jnp.square(jnp.maximum(h, 0.0)).astype(BF16)
        acc = acc + _dot(h, wd_ref[j * MLP_TH:(j + 1) * MLP_TH, :])
    gate = jax.nn.sigmoid(_dot(_rms_f32(acc, gp_ref[...]).astype(BF16), wg_ref[...]))
    y = acc + gate * _dot(p_ref[...].astype(BF16), wp_ref[...])
    if final:
        y = _rms_f32(y, fg_ref[...])
    out_ref[...] = y


def _resident(a):
    return pl.BlockSpec(a.shape, lambda i: (0,) * a.ndim, pipeline_mode=pl.Buffered(1))


def _tail(front, front_args, front_specs, gm, wu, wd, gp, wg, p, wp, fg, final, name):
    tm = PROJ_TM
    shared = (gm, wu, wd, gp, wg)
    return pl.pallas_call(
        functools.partial(_tail_kernel, n_front=len(front_args), front=front, final=final),
        grid=(TOKENS // tm,),
        in_specs=list(front_specs) + [_resident(a) for a in shared] + [
            pl.BlockSpec((tm, PLE_DIM), lambda i: (i, 0)), _resident(wp), _resident(fg)],
        out_specs=pl.BlockSpec((tm, D_MODEL), lambda i: (i, 0)),
        out_shape=jax.ShapeDtypeStruct((TOKENS, D_MODEL), F32),
        compiler_params=_params("parallel"),
        name=name,
    )(*front_args, *shared, p, wp, fg)


def _tail_plain(x, o, w_o, *rest, name):
    tm = PROJ_TM
    specs = [pl.BlockSpec((tm, D_MODEL), lambda i: (i, 0)),
             pl.BlockSpec((tm, o.shape[1]), lambda i: (i, 0)), _resident(w_o)]
    return _tail(_mixer_residual, (x, o, w_o), specs, *rest, name=name)


def _a_qkv_kernel(x_ref, g_ref, w_ref, c_ref, sa_ref, sb_ref, *refs):
    outs, stage = refs[:9], refs[9]
    tm = x_ref.shape[0]
    gl = A_GROUP_LANES
    nc = A_GROUP_CHUNKS
    xn = _rms_f32(x_ref[...], g_ref[...]).astype(BF16)
    c, sa, sb = c_ref[...], sa_ref[...], sb_ref[...]

    def emit(dil, dsts):
        n = tm // dil
        for k, dst in enumerate(dsts):
            for ci in range(nc):
                for r in range(dil):
                    if dil == 1:
                        rows = stage[k * nc + ci]
                    else:
                        rows = stage[k * nc + ci, pl.ds(r, n, stride=dil), :]
                    dst[0, ci, r] = rows.astype(BF16)

    for g, (_, dil) in enumerate(DIL_PAIRS):
        y = _dot(xn, w_ref[:, 2 * gl * g:2 * gl * (g + 1)])
        for i in range(2 * nc):
            stage[i] = _rope_chunk(y[:, i * LANES:(i + 1) * LANES], c, sa, sb, ROT_DIM // 2)
        emit(dil, (outs[3 * g], outs[3 * g + 1]))
    for g, (_, dil) in enumerate(DIL_PAIRS):
        y = _dot(xn, w_ref[:, 6 * gl + gl * g:6 * gl + gl * (g + 1)])
        for i in range(nc):
            stage[i] = y[:, i * LANES:(i + 1) * LANES]
        emit(dil, (outs[3 * g + 2],))


def _a_qkv(x, g, w, tabs):
    tm = PROJ_TM
    nb = SEQ // tm
    gl = A_GROUP_LANES
    tab_spec = pl.BlockSpec((tm, LANES), lambda i: (i % nb, 0))
    out_shapes, out_specs = [], []
    nc = A_GROUP_CHUNKS
    for _, dil in DIL_PAIRS:
        for _ in range(3):
            out_shapes.append(jax.ShapeDtypeStruct((BATCH, nc, dil, SEQ // dil, LANES), BF16))
            out_specs.append(pl.BlockSpec((1, nc, dil, tm // dil, LANES),
                                          lambda i: (i // nb, 0, 0, i % nb, 0)))
    return pl.pallas_call(
        _a_qkv_kernel,
        grid=(TOKENS // tm,),
        in_specs=[
            pl.BlockSpec((tm, D_MODEL), lambda i: (i, 0)),
            pl.BlockSpec((1, D_MODEL), lambda i: (0, 0)),
            pl.BlockSpec((D_MODEL, 9 * gl), lambda i: (0, 0)),
            tab_spec, tab_spec, tab_spec,
        ],
        out_specs=out_specs,
        out_shape=out_shapes,
        scratch_shapes=[pltpu.VMEM((2 * nc, tm, LANES), F32)],
        compiler_params=_params("parallel"),
        name="a_qkv",
    )(x, g, w, *tabs)


def _a_attn_kernel(q_ref, k_ref, v_ref, o_ref, lse_ref, *, seg, dil, token_order):
    tq = BAND_TQ
    kw = tq + 2 * A_BAND_HALF
    low = _lane_iota() < HEAD_DIM
    ones = jnp.ones((kw, LANES), BF16)

    def tile(u):
        p0 = pl.multiple_of(u * tq, tq)
        if seg >= kw:
            seg0 = (p0 // seg) * seg
            ws = jnp.clip(p0 - A_BAND_HALF, seg0, seg0 + seg - kw)
        else:
            ws = (p0 // kw) * kw
        ws = pl.multiple_of(ws, A_BAND_HALF)
        rows = p0 + lax.broadcasted_iota(jnp.int32, (tq, 1), 0)
        cols = ws + lax.broadcasted_iota(jnp.int32, (1, kw), 1)
        valid = jnp.abs(cols - rows) <= A_BAND_HALF
        if seg < kw:
            seg_of = lambda p: lax.shift_right_logical(p, seg.bit_length() - 1)
            valid = jnp.logical_and(valid, seg_of(cols) == seg_of(rows))
        r = p0 // seg
        q0 = p0 - r * seg
        for c in range(A_GROUP_CHUNKS):
            qc = q_ref[0, c, pl.ds(p0, tq), :] * jnp.asarray(HEAD_DIM ** -0.5, BF16)
            kc = k_ref[0, c, pl.ds(ws, kw), :]
            v1 = jnp.concatenate([v_ref[0, c, pl.ds(ws, kw), :], ones], axis=1)
            outs, lses = [], []
            n_heads = 2 if 2 * c + 1 < A_GROUP_HEADS else 1
            for hh in range(n_heads):
                qm = jnp.where(low if hh == 0 else jnp.logical_not(low), qc, jnp.zeros_like(qc))
                s = jnp.where(valid, _dot_nt(qm, kc), NEG_INF)
                m = jnp.max(s, axis=-1, keepdims=True)
                ol = _dot(jnp.exp(s - m).astype(BF16), v1)
                l = ol[:, LANES:]
                outs.append(ol[:, :LANES] * (1.0 / l))
                lses.append(m + jnp.log(l))
            if n_heads == 2:
                o = jnp.where(low, outs[0], outs[1])
                ls = jnp.where(low, lses[0], lses[1])
            else:
                o = jnp.where(low, outs[0], 0.0)
                ls = jnp.where(low, lses[0], 0.0)
            if not token_order or dil == 1:
                dst = pl.ds(p0, tq)
            else:
                dst = pl.ds(r + dil * q0, tq, stride=dil)
            o_ref[0, c, dst, :] = o
            lse_ref[0, c, dst, :] = ls

    def body(i, carry):
        for j in range(BAND_UNROLL):
            tile(i * BAND_UNROLL + j)
        return carry

    lax.fori_loop(0, SEQ // tq // BAND_UNROLL, body, 0)


def _a_attn(q, k, v, dil, token_order):
    seg = SEQ // dil
    nc = A_GROUP_CHUNKS
    q, k, v = (a.reshape(BATCH, nc, SEQ, LANES) for a in (q, k, v))
    in_spec = pl.BlockSpec((1, nc, SEQ, LANES), lambda b: (b, 0, 0, 0))
    out_spec = pl.BlockSpec((1, nc, SEQ, LANES), lambda b: (b, 0, 0, 0))
    out_shape = jax.ShapeDtypeStruct((BATCH, nc, SEQ, LANES), F32)
    return pl.pallas_call(
        functools.partial(_a_attn_kernel, seg=seg, dil=dil, token_order=token_order),
        grid=(BATCH,),
        in_specs=[in_spec, in_spec, in_spec],
        out_specs=[out_spec, out_spec],
        out_shape=[out_shape, out_shape],
        compiler_params=_params("parallel"),
        name=f"a_attn_d{dil}",
    )(q, k, v)


def _a_combine_kernel(o0, o1, o2, l0, l1, l2, out_ref):
    tm = out_ref.shape[0]
    dil = DIL_PAIRS[-1][1]
    seg = SEQ // dil
    pos0 = (pl.program_id(0) % (SEQ // tm)) * (tm // dil)

    def token_rows(ref, c):
        return jnp.concatenate(
            [ref[0, c, pl.ds(pos0 + p, dil, stride=seg), :] for p in range(tm // dil)], axis=0)

    for c in range(A_GROUP_CHUNKS):
        ls = [l0[0, c], l1[0, c], token_rows(l2, c)]
        os_ = [o0[0, c], o1[0, c], token_rows(o2, c)]
        m = jnp.maximum(jnp.maximum(ls[0], ls[1]), ls[2])
        es = [jnp.exp(l - m) for l in ls]
        inv = 1.0 / (es[0] + es[1] + es[2])
        for g in range(len(DIL_PAIRS)):
            col = g * A_GROUP_LANES + c * LANES
            out_ref[:, col:col + LANES] = (os_[g] * (es[g] * inv)).astype(BF16)


def _a_combine(os_, ls_):
    tm = PROJ_TM
    nb = SEQ // tm
    n = len(DIL_PAIRS) * A_GROUP_LANES
    part = pl.BlockSpec((1, A_GROUP_CHUNKS, tm, LANES), lambda i: (i // nb, 0, i % nb, 0))
    whole = pl.BlockSpec((1, A_GROUP_CHUNKS, SEQ, LANES), lambda i: (i // nb, 0, 0, 0))
    return pl.pallas_call(
        _a_combine_kernel,
        grid=(TOKENS // tm,),
        in_specs=[part, part, whole, part, part, whole],
        out_specs=pl.BlockSpec((tm, n), lambda i: (i, 0)),
        out_shape=jax.ShapeDtypeStruct((TOKENS, n), BF16),
        compiler_params=_params("parallel"),
        name="a_combine",
    )(*os_, *ls_)


def _b_proj_kernel(x_ref, g_ref, win_ref, qn_ref, kvn_ref, wuq_ref, wuk_ref, wuvt_ref,
                   c_ref, sa_ref, sb_ref, q_out, k_out, vt_out):
    c, sa, sb = c_ref[...], sa_ref[...], sb_ref[...]
    xn = _rms_f32(x_ref[...], g_ref[...]).astype(BF16)
    z = _dot(xn, win_ref[...])
    cq = _rms_f32(z[:, :B_Q_RANK], qn_ref[...]).astype(BF16)
    ckv = _rms_f32(z[:, B_Q_RANK:B_Q_RANK + B_KV_RANK], kvn_ref[...]).astype(BF16)
    k_rope = _rope_chunk(z[:, B_Q_RANK + B_KV_RANK:], c, sa, sb, B_ROPE // 2)
    q = _dot(cq, wuq_ref[...])
    k = _dot(ckv, wuk_ref[...])
    for h in range(B_HEADS):
        sl = slice(h * LANES, (h + 1) * LANES)
        q_out[:, sl] = (_rope_chunk(q[:, sl], c, sa, sb, B_ROPE // 2) * B_Q_SCALE).astype(BF16)
        k_out[:, sl] = (k[:, sl] + k_rope).astype(BF16)
    vt_out[...] = _dot_nt(wuvt_ref[...], ckv).astype(BF16)


def _b_proj(x, g, win, qn, kvn, wuq, wuk, wuvt, tabs):
    tm = PROJ_TM
    nb = SEQ // tm
    tab_spec = pl.BlockSpec((tm, LANES), lambda i: (i % nb, 0))

    def full(a):
        return pl.BlockSpec(a.shape, lambda i: (0,) * a.ndim)

    nqk = B_HEADS * LANES
    nv = B_HEADS * B_V
    return pl.pallas_call(
        _b_proj_kernel,
        grid=(TOKENS // tm,),
        in_specs=[pl.BlockSpec((tm, D_MODEL), lambda i: (i, 0)), full(g), full(win), full(qn),
                  full(kvn), full(wuq), full(wuk), full(wuvt), tab_spec, tab_spec, tab_spec],
        out_specs=[pl.BlockSpec((tm, nqk), lambda i: (i, 0)),
                   pl.BlockSpec((tm, nqk), lambda i: (i, 0)),
                   pl.BlockSpec((nv, tm), lambda i: (0, i))],
        out_shape=[jax.ShapeDtypeStruct((TOKENS, nqk), BF16),
                   jax.ShapeDtypeStruct((TOKENS, nqk), BF16),
                   jax.ShapeDtypeStruct((nv, TOKENS), BF16)],
        compiler_params=_params("parallel"),
        name="b_proj",
    )(x, g, win, qn, kvn, wuq, wuk, wuvt, *tabs)


def _with_ones_rows(vt):
    return jnp.concatenate([vt, jnp.ones((16, vt.shape[1]), vt.dtype)], axis=0)


def _attend_keys_major(streams, n_chunks, finish):
    items = []
    for s0 in range(0, len(streams), 2):
        for j in range(n_chunks):
            items += [(s, j) for s in (s0, s0 + 1) if s < len(streams)]
    state = [None] * len(streams)

    def scores(item):
        s, j = item
        k_chunk, _, qw = streams[s]
        return _dot_nt(k_chunk(j), qw)

    def absorb(item, st):
        s, j = item
        vt1 = _with_ones_rows(streams[s][1](j))
        mj = jnp.max(st, axis=0, keepdims=True)
        if j == 0:
            state[s] = (mj, _dot(vt1, jnp.exp2(st - mj).astype(BF16)))
        else:
            m, acc = state[s]
            m_new = jnp.maximum(m, mj)
            e = jnp.exp2(st - m_new).astype(BF16)
            state[s] = (m_new, acc * jnp.exp2(m - m_new) + _dot(vt1, e))
        if j == n_chunks - 1:
            finish(s, state[s][1])
            state[s] = None

    st = scores(items[0])
    for idx, item in enumerate(items):
        st_next = scores(items[idx + 1]) if idx + 1 < len(items) else None
        absorb(item, st)
        st = st_next


def _b_attn_kernel(q_ref, k_ref, vt_ref, o_ref):
    kc, tq = ATTN_KC, ATTN_TQ
    outs = {}

    def stream(t, hh):
        sl = slice(hh * LANES, (hh + 1) * LANES)
        rows = slice(hh * B_V, (hh + 1) * B_V)
        return (lambda j: k_ref[j * kc:(j + 1) * kc, sl],
                lambda j: vt_ref[rows, j * kc:(j + 1) * kc],
                q_ref[t * tq:(t + 1) * tq, sl])

    def finish(s, acc):
        t, hh = divmod(s, 2)
        outs[hh] = acc[:B_V] * (1.0 / acc[B_V:B_V + 1])
        if hh == 1:
            o = jnp.concatenate([outs[0], outs[1]], axis=0)
            o_ref[t * tq:(t + 1) * tq, :] = o.T.astype(BF16)

    _attend_keys_major([stream(t, hh) for t in range(ATTN_TILES) for hh in range(2)],
                       SEQ // kc, finish)


def _b_attn(q, k, vt):
    rows = ATTN_TQ * ATTN_TILES
    nq = SEQ // rows
    return pl.pallas_call(
        _b_attn_kernel,
        grid=(BATCH, B_HEADS // 2, nq),
        in_specs=[
            pl.BlockSpec((rows, 2 * LANES), lambda b, h, i: (b * nq + i, h)),
            pl.BlockSpec((SEQ, 2 * LANES), lambda b, h, i: (b, h)),
            pl.BlockSpec((2 * B_V, SEQ), lambda b, h, i: (h, b)),
        ],
        out_specs=pl.BlockSpec((rows, LANES), lambda b, h, i: (b * nq + i, h)),
        out_shape=jax.ShapeDtypeStruct((TOKENS, B_HEADS * B_V), BF16),
        compiler_params=_params("parallel", "parallel", "arbitrary"),
        name="b_attn",
    )(q, k, vt)


def _c_group_geometry(gi):
    first_row = gi * C_ROW_GROUP
    wrow = min(max(first_row - NA_ROWS // 2, 0), GRID_ROWS - C_WIN_ROWS)
    n_groups = GRID_ROWS // C_ROW_GROUP
    case = 0 if gi == 0 else (2 if gi == n_groups - 1 else 1)
    return first_row, wrow, case


def _c_row_offset_slot(gi, kr, i):
    first_row, wrow, _ = _c_group_geometry(gi)
    qrow, krow = first_row + i, wrow + kr
    rs = min(max(qrow - NA_ROWS // 2, 0), GRID_ROWS - NA_ROWS)
    if rs <= krow < rs + NA_ROWS:
        return krow - qrow + NA_ROWS - 1
    return 2 * NA_ROWS - 1


def _c_attn_kernel(q_ref, k_ref, vt_ref, left_ref, right_ref, o_ref, bias_ref):
    low = _lane_iota() < HEAD_DIM
    nq = C_ROW_GROUP * GRID_W
    nk = C_WIN_ROWS * GRID_W
    n_groups = GRID_ROWS // C_ROW_GROUP

    @pl.when(pl.program_id(1) == 0)
    def _():
        for case, gi in enumerate((0, 1, n_groups - 1)):
            for kr in range(C_WIN_ROWS):
                for hh in range(2):
                    for ip in range(C_ROW_GROUP // 2):
                        a0 = _c_row_offset_slot(gi, kr, 2 * ip)
                        a1 = _c_row_offset_slot(gi, kr, 2 * ip + 1)
                        col = (hh * C_ROW_GROUP // 2 + ip) * LANES
                        bias_ref[case, kr * GRID_W:(kr + 1) * GRID_W, col:col + LANES] = (
                            left_ref[0, hh, a0] + right_ref[0, hh, a1])

    def scores(gi):
        first_row, wrow, _ = _c_group_geometry(gi)
        q0, ws = first_row * GRID_W, wrow * GRID_W
        qc = q_ref[q0:q0 + nq, :]
        zero = jnp.zeros_like(qc)
        qq = jnp.concatenate([jnp.where(low, qc, zero), jnp.where(low, zero, qc)], axis=0)
        return _dot_nt(k_ref[ws:ws + nk, :], qq)

    def absorb(gi, st):
        first_row, wrow, case = _c_group_geometry(gi)
        q0, ws = first_row * GRID_W, wrow * GRID_W
        st = st + bias_ref[case]
        e = jnp.exp2(st - jnp.max(st, axis=0, keepdims=True)).astype(BF16)
        ot = _dot(_with_ones_rows(vt_ref[:, ws:ws + nk]), e)
        ot = ot[:LANES] * (1.0 / ot[LANES:LANES + 1])
        o = jnp.concatenate([ot[:HEAD_DIM, :nq], ot[HEAD_DIM:, nq:]], axis=0)
        o_ref[q0:q0 + nq, :] = o.T.astype(BF16)

    st = scores(0)
    for gi in range(n_groups):
        st_next = scores(gi + 1) if gi + 1 < n_groups else None
        absorb(gi, st)
        st = st_next


def _c_attn(qk, vt, left, right):
    nq = C_ROW_GROUP * GRID_W
    nk = C_WIN_ROWS * GRID_W
    npair = C_HEADS // 2
    tab_spec = pl.BlockSpec((1, 2, 2 * NA_ROWS, GRID_W, LANES), lambda h, b: (h, 0, 0, 0, 0))
    return pl.pallas_call(
        _c_attn_kernel,
        grid=(npair, BATCH),
        in_specs=[
            pl.BlockSpec((SEQ, LANES), lambda h, b: (b, h)),
            pl.BlockSpec((SEQ, LANES), lambda h, b: (b, npair + h)),
            pl.BlockSpec((LANES, SEQ), lambda h, b: (h, b)),
            tab_spec, tab_spec,
        ],
        out_specs=pl.BlockSpec((SEQ, LANES), lambda h, b: (b, h)),
        out_shape=jax.ShapeDtypeStruct((TOKENS, C_HEADS * HEAD_DIM), BF16),
        scratch_shapes=[pltpu.VMEM((3, nk, 2 * nq), F32)],
        compiler_params=_params("arbitrary", "arbitrary"),
        name="c_attn",
    )(qk, qk, vt, left, right)


def _c_column_tables(rpb):
    kc, qc = np.arange(GRID_W)[:, None], np.arange(GRID_W)[None, :]
    win0 = np.clip(qc - NA_COLS // 2, 0, GRID_W - NA_COLS)
    col_ok = (kc >= win0) & (kc < win0 + NA_COLS)
    col_off = np.clip(kc - qc + NA_COLS - 1, 0, 2 * NA_COLS - 2)
    col_sel = (col_off[None] == np.arange(2 * NA_COLS - 1)[:, None, None]).astype(np.float32)
    cols = jnp.einsum("hab,bkq->hakq", rpb, col_sel, precision=lax.Precision.HIGHEST)
    cols = jnp.where(col_ok, cols * LOG2_E, NEG_INF)
    cols = jnp.concatenate([cols, jnp.full((C_HEADS, 1, GRID_W, GRID_W), NEG_INF, F32)], axis=1)
    cols = cols.reshape(C_HEADS // 2, 2, 2 * NA_ROWS, GRID_W, GRID_W)
    zeros = jnp.zeros_like(cols)
    return jnp.concatenate([cols, zeros], axis=-1), jnp.concatenate([zeros, cols], axis=-1)


def _d_attn_kernel(lq1, lk1, lq2, lk2, q_ref, k_ref, vt_ref, sub_ref, o_ref, *, lambda_init):
    lam = (jnp.exp(jnp.sum(lq1[...] * lk1[...], axis=-1, keepdims=True))
           - jnp.exp(jnp.sum(lq2[...] * lk2[...], axis=-1, keepdims=True)) + lambda_init)
    low = _lane_iota() < D_HEAD
    kc, tq, dv = ATTN_KC, ATTN_TQ, 2 * D_HEAD
    k_chunk = lambda j: k_ref[j * kc:(j + 1) * kc, :]
    vt_chunk = lambda j: vt_ref[:, j * kc:(j + 1) * kc]
    outs = {}

    def stream(t, half):
        q = q_ref[t * tq:(t + 1) * tq, :]
        zero = jnp.zeros_like(q)
        return (k_chunk, vt_chunk, jnp.where(low, q, zero) if half == 0 else jnp.where(low, zero, q))

    def finish(s, acc):
        t, half = divmod(s, 2)
        outs[half] = acc[:dv] * (1.0 / acc[dv:dv + 1])
        if half == 1:
            o = (outs[0] - lam * outs[1]).T
            o_ref[t * tq:(t + 1) * tq, :] = (
                _rms_f32(o, sub_ref[...]) * (1.0 - lambda_init)).astype(BF16)

    _attend_keys_major([stream(t, half) for t in range(ATTN_TILES) for half in range(2)],
                       SEQ // kc, finish)


def _d_attn(qk, vt, lq1, lk1, lq2, lk2, subln, lambda_init):
    tq = ATTN_TQ * ATTN_TILES
    nq = SEQ // tq
    vec = pl.BlockSpec((1, D_HEAD), lambda b, h, i: (0, 0))
    return pl.pallas_call(
        functools.partial(_d_attn_kernel, lambda_init=lambda_init),
        grid=(BATCH, D_HEADS, nq),
        in_specs=[
            vec, vec, vec, vec,
            pl.BlockSpec((tq, LANES), lambda b, h, i: (b * nq + i, h)),
            pl.BlockSpec((SEQ, LANES), lambda b, h, i: (b, D_HEADS + h)),
            pl.BlockSpec((LANES, SEQ), lambda b, h, i: (h, b)),
            pl.BlockSpec((1, 2 * D_HEAD), lambda b, h, i: (0, 0)),
        ],
        out_specs=pl.BlockSpec((tq, LANES), lambda b, h, i: (b * nq + i, h)),
        out_shape=jax.ShapeDtypeStruct((TOKENS, 2 * D_HEADS * D_HEAD), BF16),
        compiler_params=_params("parallel", "parallel", "arbitrary"),
        name="d_attn",
    )(lq1, lk1, lq2, lk2, qk, qk, vt, subln)


def _rope_tables(rot_dim, period, lane0):
    r = rot_dim // 2
    inv = ROPE_THETA ** (-jnp.arange(0, rot_dim, 2, dtype=F32) / rot_dim)
    ang = jnp.arange(SEQ, dtype=F32)[:, None] * inv[None, :]
    cos, sin = jnp.cos(ang), jnp.sin(ang)
    lane = np.arange(LANES) % period - lane0
    first = (lane >= 0) & (lane < r)
    second = (lane >= r) & (lane < 2 * r)
    idx = np.where(first, lane, np.where(second, lane - r, 0))
    cg, sg = cos[:, idx], sin[:, idx]
    c = jnp.where(first | second, cg, 1.0)
    sa = jnp.where(first, -sg, 0.0)
    sb = jnp.where(second, sg, 0.0)
    return c, sa, sb


def _a_weights(w_qkv, w_o):
    nh = A_GROUP_HEADS * HEAD_DIM
    pad = jnp.zeros((D_MODEL, A_GROUP_LANES - nh), w_qkv.dtype)

    def cols(part, g):
        base = part * A_HEADS * HEAD_DIM + g * nh
        return [w_qkv[:, base:base + nh], pad]

    pieces = []
    for g in range(len(DIL_PAIRS)):
        pieces += cols(0, g) + cols(1, g)
    for g in range(len(DIL_PAIRS)):
        pieces += cols(2, g)
    w = jnp.concatenate(pieces, axis=1).astype(BF16)
    wo = w_o.reshape(len(DIL_PAIRS), nh, D_MODEL)
    wo = jnp.pad(wo, ((0, 0), (0, A_GROUP_LANES - nh), (0, 0))).astype(BF16)
    return w, wo.reshape(len(DIL_PAIRS) * A_GROUP_LANES, D_MODEL)


def _b_weights(w_in, w_uq, w_ukv):
    split = B_Q_RANK + B_KV_RANK
    win = jnp.concatenate([
        w_in[:, :split], jnp.zeros((D_MODEL, B_NOPE), w_in.dtype), w_in[:, split:],
        jnp.zeros((D_MODEL, LANES - B_NOPE - B_ROPE), w_in.dtype)], axis=1).astype(BF16)
    wuq = w_uq.reshape(B_Q_RANK, B_HEADS, B_NOPE + B_ROPE)
    wuq = jnp.pad(wuq, ((0, 0), (0, 0), (0, LANES - B_NOPE - B_ROPE)))
    wuq = wuq.reshape(B_Q_RANK, B_HEADS * LANES).astype(BF16)
    wukv = w_ukv.reshape(B_KV_RANK, B_HEADS, B_NOPE + B_V)
    wuk = jnp.pad(wukv[:, :, :B_NOPE], ((0, 0), (0, 0), (0, LANES - B_NOPE)))
    wuk = wuk.reshape(B_KV_RANK, B_HEADS * LANES).astype(BF16)
    wuvt = wukv[:, :, B_NOPE:].reshape(B_KV_RANK, B_HEADS * B_V).T.astype(BF16)
    return win, wuq, wuk, wuvt


def _split_qk_vt(w_qkv, n_qk):
    return w_qkv[:, :n_qk].astype(BF16), w_qkv[:, n_qk:].T.astype(BF16)


def kernel(x, p, a_norm, a_w_qkv, a_w_o, b_norm, b_w_in, b_q_norm, b_w_uq, b_kv_norm, b_w_ukv, b_w_o, c_norm, c_w_qkv, c_rpb, c_w_o, d_norm, d_w_qkv, d_lambda_q1, d_lambda_k1, d_lambda_q2, d_lambda_k2, d_subln, d_w_o, mlp_norm, w_up, w_down, ple_norm, w_ple_gate, w_ple_proj, final_norm):
    tabs_p = _rope_tables(ROT_DIM, HEAD_DIM, 0)
    tabs_l = _rope_tables(B_ROPE, LANES, B_NOPE)
    h = x.reshape(TOKENS, D_MODEL)
    row = lambda v: v.reshape(1, -1)

    for i in range(DEPTH):
        rest = (row(mlp_norm[i]), w_up[i].astype(BF16), w_down[i].astype(BF16),
                row(ple_norm[i]), w_ple_gate[i].astype(BF16), p[i].reshape(TOKENS, PLE_DIM),
                w_ple_proj[i].astype(BF16), row(final_norm), i == DEPTH - 1)
        if i == 0:
            w, wo = _a_weights(a_w_qkv[0], a_w_o[0])
            parts = _a_qkv(h, row(a_norm[0]), w, tabs_p)
            os_, ls_ = [], []
            for g, (_, dil) in enumerate(DIL_PAIRS):
                o, lse = _a_attn(parts[3 * g], parts[3 * g + 1], parts[3 * g + 2], dil,
                                 token_order=g < len(DIL_PAIRS) - 1)
                os_.append(o)
                ls_.append(lse)
            h = _tail_plain(h, _a_combine(os_, ls_), wo, *rest, name="a_tail")
        elif i == 1:
            win, wuq, wuk, wuvt = _b_weights(b_w_in[0], b_w_uq[0], b_w_ukv[0])
            q, k, vt = _b_proj(h, row(b_norm[0]), win, row(b_q_norm[0]), row(b_kv_norm[0]),
                               wuq, wuk, wuvt, tabs_l)
            h = _tail_plain(h, _b_attn(q, k, vt), b_w_o[0].astype(BF16), *rest, name="b_tail")
        elif i == 2:
            w, wvt = _split_qk_vt(c_w_qkv[0], 2 * C_HEADS * HEAD_DIM)
            qk, vt = _norm_proj(h, row(c_norm[0]), w, wvt, tabs_p, False, "c_qkv")
            o = _c_attn(qk, vt, *_c_column_tables(c_rpb[0]))
            h = _tail_plain(h, o, c_w_o[0].astype(BF16), *rest, name="c_tail")
        else:
            lambda_init = 0.8 - 0.6 * math.exp(-0.3 * i)
            w, wvt = _split_qk_vt(d_w_qkv[0], 2 * 2 * D_HEADS * D_HEAD)
            qk, vt = _norm_proj(h, row(d_norm[0]), w, wvt, tabs_p, True, "d_qkv")
            o = _d_attn(qk, vt, row(d_lambda_q1[0]), row(d_lambda_k1[0]), row(d_lambda_q2[0]),
                        row(d_lambda_k2[0]), row(d_subln[0]), lambda_init)
            h = _tail_plain(h, o, d_w_o[0].astype(BF16), *rest, name="d_tail")
    return h.reshape(BATCH, SEQ, D_MODEL)
```

```python
import functools
import math

import numpy as np
import jax
import jax.numpy as jnp
from jax import lax
from jax.experimental import pallas as pl
from jax.experimental.pallas import tpu as pltpu

F32 = jnp.float32
BF16 = jnp.bfloat16

D_MODEL = 1024
BATCH = 8
SEQ = 2048
DEPTH = 4
TOKENS = BATCH * SEQ
HEAD_DIM = 64
ROPE_THETA = 500000.0
ROT_DIM = HEAD_DIM // 4
NEG_INF = -1e30
RMS_EPS = 1e-6
LOG2_E = math.log2(math.e)

DIL_PAIRS = ((128, 1), (512, 4), (2048, 16))
A_GROUP_HEADS = 5
A_HEADS = A_GROUP_HEADS * len(DIL_PAIRS)
A_BAND_HALF = 64
A_GROUP_LANES = 384
A_GROUP_CHUNKS = A_GROUP_LANES // 128

B_HEADS = 16
B_Q_RANK = 256
B_KV_RANK = 128
B_NOPE = 64
B_ROPE = 32
B_V = 64

C_HEADS = 16
GRID_W = 64
GRID_ROWS = SEQ // GRID_W
NA_ROWS = 8
NA_COLS = 16
C_ROW_GROUP = 4
C_WIN_ROWS = 12
C_KEY_CHUNK = 768
C_WIDTH = 4

D_HEADS = 8
D_HEAD = 64

MLP_HIDDEN = 4 * D_MODEL
PLE_DIM = 256

LANES = 128
VMEM_LIMIT = 48 * 1024 * 1024

PROJ_TM = 512
MLP_TH = 1024
ATTN_TQ = 512
ATTN_KC = 512
ATTN_TILES = 2
ATTN_LOOKAHEAD = 3

B_Q_SCALE = (B_NOPE + B_ROPE) ** -0.5 * LOG2_E
CD_Q_SCALE = HEAD_DIM ** -0.5 * LOG2_E
BAND_TQ = 128
BAND_UNROLL = 2


def _params(*sem):
    return pltpu.CompilerParams(dimension_semantics=sem, vmem_limit_bytes=VMEM_LIMIT)


def _rms_f32(x, g):
    ms = jnp.mean(x * x, axis=-1, keepdims=True)
    return x * lax.rsqrt(ms + RMS_EPS) * g


def _rope_chunk(y, c, sa, sb, shift):
    return (y * c + pltpu.roll(y, LANES - shift, 1) * sa + pltpu.roll(y, shift, 1) * sb)


def _softmax_parts(s):
    m = jnp.max(s, axis=-1, keepdims=True)
    e = jnp.exp(s - m)
    l = jnp.sum(e, axis=-1, keepdims=True)
    return m, e, l


def _dot(a, b):
    return jnp.dot(a, b, preferred_element_type=F32)


def _dot_nt(a, b):
    return lax.dot_general(a, b, (((1,), (1,)), ((), ())), preferred_element_type=F32)


def _lane_iota():
    return lax.broadcasted_iota(jnp.int32, (1, LANES), 1)


def _norm_proj_kernel(x_ref, g_ref, w_ref, wvt_ref, c_ref, sa_ref, sb_ref, qk_ref, vt_ref,
                      *, rope, tn):
    xn = _rms_f32(x_ref[...], g_ref[...]).astype(BF16)
    n = qk_ref.shape[1]
    c, sa, sb = c_ref[...], sa_ref[...], sb_ref[...]
    for j in range(n // tn):
        y = _dot(xn, w_ref[:, j * tn:(j + 1) * tn])
        for i in range(tn // LANES):
            col = j * tn + i * LANES
            chunk = y[:, i * LANES:(i + 1) * LANES]
            if rope:
                chunk = _rope_chunk(chunk, c, sa, sb, ROT_DIM // 2)
            if col < n // 2:
                chunk = chunk * CD_Q_SCALE
            qk_ref[:, col:col + LANES] = chunk.astype(BF16)
    vt_ref[...] = _dot_nt(wvt_ref[...], xn).astype(BF16)


def _norm_proj(x, g, w, wvt, tabs, rope, name):
    n = w.shape[1]
    nv = wvt.shape[0]
    tm = PROJ_TM
    nb = SEQ // tm
    tab_spec = pl.BlockSpec((tm, LANES), lambda i: (i % nb, 0))
    return pl.pallas_call(
        functools.partial(_norm_proj_kernel, rope=rope, tn=512),
        grid=(TOKENS // tm,),
        in_specs=[
            pl.BlockSpec((tm, D_MODEL), lambda i: (i, 0)),
            pl.BlockSpec((1, D_MODEL), lambda i: (0, 0)),
            pl.BlockSpec((D_MODEL, n), lambda i: (0, 0)),
            pl.BlockSpec((nv, D_MODEL), lambda i: (0, 0)),
            tab_spec, tab_spec, tab_spec,
        ],
        out_specs=[pl.BlockSpec((tm, n), lambda i: (i, 0)),
                   pl.BlockSpec((nv, tm), lambda i: (0, i))],
        out_shape=[jax.ShapeDtypeStruct((TOKENS, n), BF16),
                   jax.ShapeDtypeStruct((nv, TOKENS), BF16)],
        compiler_params=_params("parallel"),
        name=name,
    )(x, g, w, wvt, *tabs)


def _mixer_residual(x_ref, o_ref, w_ref):
    return x_ref[...] + _dot(o_ref[...], w_ref[...])


def _tail_kernel(*refs, n_front, front, final):
    front_refs = refs[:n_front]
    gm_ref, wu_ref, wd_ref, gp_ref, wg_ref, p_ref, wp_ref, fg_ref, out_ref = refs[n_front:]
    x1 = front(*front_refs)
    xn = _rms_f32(x1, gm_ref[...]).astype(BF16)
    acc = x1
    for j in range(MLP_HIDDEN // MLP_TH):
        h = _dot(xn, wu_ref[:, j * MLP_TH:(j + 1) * MLP_TH])
        h = jnp.square(jnp.maximum(h, 0.0)).astype(BF16)
        acc = acc + _dot(h, wd_ref[j * MLP_TH:(j + 1) * MLP_TH, :])
    gate = jax.nn.sigmoid(_dot(_rms_f32(acc, gp_ref[...]).astype(BF16), wg_ref[...]))
    y = acc + gate * _dot(p_ref[...].astype(BF16), wp_ref[...])
    if final:
        y = _rms_f32(y, fg_ref[...])
    out_ref[...] = y


def _resident(a):
    return pl.BlockSpec(a.shape, lambda i: (0,) * a.ndim, pipeline_mode=pl.Buffered(1))


def _tail(front, front_args, front_specs, gm, wu, wd, gp, wg, p, wp, fg, final, name):
    tm = PROJ_TM
    shared = (gm, wu, wd, gp, wg)
    return pl.pallas_call(
        functools.partial(_tail_kernel, n_front=len(front_args), front=front, final=final),
        grid=(TOKENS // tm,),
        in_specs=list(front_specs) + [_resident(a) for a in shared] + [
            pl.BlockSpec((tm, PLE_DIM), lambda i: (i, 0)), _resident(wp), _resident(fg)],
        out_specs=pl.BlockSpec((tm, D_MODEL), lambda i: (i, 0)),
        out_shape=jax.ShapeDtypeStruct((TOKENS, D_MODEL), F32),
        compiler_params=_params("parallel"),
        name=name,
    )(*front_args, *shared, p, wp, fg)


def _tail_plain(x, o, w_o, *rest, name):
    tm = PROJ_TM
    specs = [pl.BlockSpec((tm, D_MODEL), lambda i: (i, 0)),
             pl.BlockSpec((tm, o.shape[1]), lambda i: (i, 0)), _resident(w_o)]
    return _tail(_mixer_residual, (x, o, w_o), specs, *rest, name=name)


def _a_qkv_kernel(x_ref, g_ref, w_ref, c_ref, sa_ref, sb_ref, *refs):
    outs, stage = refs[:9], refs[9]
    tm = x_ref.shape[0]
    gl = A_GROUP_LANES
    nc = A_GROUP_CHUNKS
    xn = _rms_f32(x_ref[...], g_ref[...]).astype(BF16)
    c, sa, sb = c_ref[...], sa_ref[...], sb_ref[...]

    def emit(dil, dsts):
        n = tm // dil
        for k, dst in enumerate(dsts):
            for ci in range(nc):
                for r in range(dil):
                    if dil == 1:
                        rows = stage[k * nc + ci]
                    else:
                        rows = stage[k * nc + ci, pl.ds(r, n, stride=dil), :]
                    dst[0, ci, r] = rows.astype(BF16)

    for g, (_, dil) in enumerate(DIL_PAIRS):
        y = _dot(xn, w_ref[:, 2 * gl * g:2 * gl * (g + 1)])
        for i in range(2 * nc):
            stage[i] = _rope_chunk(y[:, i * LANES:(i + 1) * LANES], c, sa, sb, ROT_DIM // 2)
        emit(dil, (outs[3 * g], outs[3 * g + 1]))
    for g, (_, dil) in enumerate(DIL_PAIRS):
        y = _dot(xn, w_ref[:, 6 * gl + gl * g:6 * gl + gl * (g + 1)])
        for i in range(nc):
            stage[i] = y[:, i * LANES:(i + 1) * LANES]
        emit(dil, (outs[3 * g + 2],))


def _a_qkv(x, g, w, tabs):
    tm = PROJ_TM
    nb = SEQ // tm
    gl = A_GROUP_LANES
    tab_spec = pl.BlockSpec((tm, LANES), lambda i: (i % nb, 0))
    out_shapes, out_specs = [], []
    nc = A_GROUP_CHUNKS
    for _, dil in DIL_PAIRS:
        for _ in range(3):
            out_shapes.append(jax.ShapeDtypeStruct((BATCH, nc, dil, SEQ // dil, LANES), BF16))
            out_specs.append(pl.BlockSpec((1, nc, dil, tm // dil, LANES),
                                          lambda i: (i // nb, 0, 0, i % nb, 0)))
    return pl.pallas_call(
        _a_qkv_kernel,
        grid=(TOKENS // tm,),
        in_specs=[
            pl.BlockSpec((tm, D_MODEL), lambda i: (i, 0)),
            pl.BlockSpec((1, D_MODEL), lambda i: (0, 0)),
            pl.BlockSpec((D_MODEL, 9 * gl), lambda i: (0, 0)),
            tab_spec, tab_spec, tab_spec,
        ],
        out_specs=out_specs,
        out_shape=out_shapes,
        scratch_shapes=[pltpu.VMEM((2 * nc, tm, LANES), F32)],
        compiler_params=_params("parallel"),
        name="a_qkv",
    )(x, g, w, *tabs)


def _a_attn_kernel(q_ref, k_ref, v_ref, o_ref, lse_ref, *, seg, dil, token_order):
    tq = BAND_TQ
    kw = tq + 2 * A_BAND_HALF
    low = _lane_iota() < HEAD_DIM
    ones = jnp.ones((kw, LANES), BF16)

    def tile(u):
        p0 = pl.multiple_of(u * tq, tq)
        if seg >= kw:
            seg0 = (p0 // seg) * seg
            ws = jnp.clip(p0 - A_BAND_HALF, seg0, seg0 + seg - kw)
        else:
            ws = (p0 // kw) * kw
        ws = pl.multiple_of(ws, A_BAND_HALF)
        rows = p0 + lax.broadcasted_iota(jnp.int32, (tq, 1), 0)
        cols = ws + lax.broadcasted_iota(jnp.int32, (1, kw), 1)
        valid = jnp.abs(cols - rows) <= A_BAND_HALF
        if seg < kw:
            seg_of = lambda p: lax.shift_right_logical(p, seg.bit_length() - 1)
            valid = jnp.logical_and(valid, seg_of(cols) == seg_of(rows))
        r = p0 // seg
        q0 = p0 - r * seg
        for c in range(A_GROUP_CHUNKS):
            qc = q_ref[0, c, pl.ds(p0, tq), :] * jnp.asarray(HEAD_DIM ** -0.5, BF16)
            kc = k_ref[0, c, pl.ds(ws, kw), :]
            v1 = jnp.concatenate([v_ref[0, c, pl.ds(ws, kw), :], ones], axis=1)
            outs, lses = [], []
            n_heads = 2 if 2 * c + 1 < A_GROUP_HEADS else 1
            for hh in range(n_heads):
                qm = jnp.where(low if hh == 0 else jnp.logical_not(low), qc, jnp.zeros_like(qc))
                s = jnp.where(valid, _dot_nt(qm, kc), NEG_INF)
                m = jnp.max(s, axis=-1, keepdims=True)
                ol = _dot(jnp.exp(s - m).astype(BF16), v1)
                l = ol[:, LANES:]
                outs.append(ol[:, :LANES] * (1.0 / l))
                lses.append(m + jnp.log(l))
            if n_heads == 2:
                o = jnp.where(low, outs[0], outs[1])
                ls = jnp.where(low, lses[0], lses[1])
            else:
                o = jnp.where(low, outs[0], 0.0)
                ls = jnp.where(low, lses[0], 0.0)
            if not token_order or dil == 1:
                dst = pl.ds(p0, tq)
            else:
                dst = pl.ds(r + dil * q0, tq, stride=dil)
            o_ref[0, c, dst, :] = o
            lse_ref[0, c, dst, :] = ls

    def body(i, carry):
        for j in range(BAND_UNROLL):
            tile(i * BAND_UNROLL + j)
        return carry

    lax.fori_loop(0, SEQ // tq // BAND_UNROLL, body, 0)


def _a_attn(q, k, v, dil, token_order):
    seg = SEQ // dil
    nc = A_GROUP_CHUNKS
    q, k, v = (a.reshape(BATCH, nc, SEQ, LANES) for a in (q, k, v))
    in_spec = pl.BlockSpec((1, nc, SEQ, LANES), lambda b: (b, 0, 0, 0))
    out_spec = pl.BlockSpec((1, nc, SEQ, LANES), lambda b: (b, 0, 0, 0))
    out_shape = jax.ShapeDtypeStruct((BATCH, nc, SEQ, LANES), F32)
    return pl.pallas_call(
        functools.partial(_a_attn_kernel, seg=seg, dil=dil, token_order=token_order),
        grid=(BATCH,),
        in_specs=[in_spec, in_spec, in_spec],
        out_specs=[out_spec, out_spec],
        out_shape=[out_shape, out_shape],
        compiler_params=_params("parallel"),
        name=f"a_attn_d{dil}",
    )(q, k, v)


def _a_combine_kernel(o0, o1, o2, l0, l1, l2, out_ref):
    tm = out_ref.shape[0]
    dil = DIL_PAIRS[-1][1]
    seg = SEQ // dil
    pos0 = (pl.program_id(0) % (SEQ // tm)) * (tm // dil)

    def token_rows(ref, c):
        return jnp.concatenate(
            [ref[0, c, pl.ds(pos0 + p, dil, stride=seg), :] for p in range(tm // dil)], axis=0)

    for c in range(A_GROUP_CHUNKS):
        ls = [l0[0, c], l1[0, c], token_rows(l2, c)]
        os_ = [o0[0, c], o1[0, c], token_rows(o2, c)]
        m = jnp.maximum(jnp.maximum(ls[0], ls[1]), ls[2])
        es = [jnp.exp(l - m) for l in ls]
        inv = 1.0 / (es[0] + es[1] + es[2])
        for g in range(len(DIL_PAIRS)):
            col = g * A_GROUP_LANES + c * LANES
            out_ref[:, col:col + LANES] = (os_[g] * (es[g] * inv)).astype(BF16)


def _a_combine(os_, ls_):
    tm = PROJ_TM
    nb = SEQ // tm
    n = len(DIL_PAIRS) * A_GROUP_LANES
    part = pl.BlockSpec((1, A_GROUP_CHUNKS, tm, LANES), lambda i: (i // nb, 0, i % nb, 0))
    whole = pl.BlockSpec((1, A_GROUP_CHUNKS, SEQ, LANES), lambda i: (i // nb, 0, 0, 0))
    return pl.pallas_call(
        _a_combine_kernel,
        grid=(TOKENS // tm,),
        in_specs=[part, part, whole, part, part, whole],
        out_specs=pl.BlockSpec((tm, n), lambda i: (i, 0)),
        out_shape=jax.ShapeDtypeStruct((TOKENS, n), BF16),
        compiler_params=_params("parallel"),
        name="a_combine",
    )(*os_, *ls_)


def _b_proj_kernel(x_ref, g_ref, win_ref, qn_ref, kvn_ref, wuq_ref, wuk_ref, wuvt_ref,
                   c_ref, sa_ref, sb_ref, q_out, k_out, vt_out):
    c, sa, sb = c_ref[...], sa_ref[...], sb_ref[...]
    xn = _rms_f32(x_ref[...], g_ref[...]).astype(BF16)
    z = _dot(xn, win_ref[...])
    cq = _rms_f32(z[:, :B_Q_RANK], qn_ref[...]).astype(BF16)
    ckv = _rms_f32(z[:, B_Q_RANK:B_Q_RANK + B_KV_RANK], kvn_ref[...]).astype(BF16)
    k_rope = _rope_chunk(z[:, B_Q_RANK + B_KV_RANK:], c, sa, sb, B_ROPE // 2)
    q = _dot(cq, wuq_ref[...])
    k = _dot(ckv, wuk_ref[...])
    for h in range(B_HEADS):
        sl = slice(h * LANES, (h + 1) * LANES)
        q_out[:, sl] = (_rope_chunk(q[:, sl], c, sa, sb, B_ROPE // 2) * B_Q_SCALE).astype(BF16)
        k_out[:, sl] = (k[:, sl] + k_rope).astype(BF16)
    vt_out[...] = _dot_nt(wuvt_ref[...], ckv).astype(BF16)


def _b_proj(x, g, win, qn, kvn, wuq, wuk, wuvt, tabs):
    tm = PROJ_TM
    nb = SEQ // tm
    tab_spec = pl.BlockSpec((tm, LANES), lambda i: (i % nb, 0))

    def full(a):
        return pl.BlockSpec(a.shape, lambda i: (0,) * a.ndim)

    nqk = B_HEADS * LANES
    nv = B_HEADS * B_V
    return pl.pallas_call(
        _b_proj_kernel,
        grid=(TOKENS // tm,),
        in_specs=[pl.BlockSpec((tm, D_MODEL), lambda i: (i, 0)), full(g), full(win), full(qn),
                  full(kvn), full(wuq), full(wuk), full(wuvt), tab_spec, tab_spec, tab_spec],
        out_specs=[pl.BlockSpec((tm, nqk), lambda i: (i, 0)),
                   pl.BlockSpec((tm, nqk), lambda i: (i, 0)),
                   pl.BlockSpec((nv, tm), lambda i: (0, i))],
        out_shape=[jax.ShapeDtypeStruct((TOKENS, nqk), BF16),
                   jax.ShapeDtypeStruct((TOKENS, nqk), BF16),
                   jax.ShapeDtypeStruct((nv, TOKENS), BF16)],
        compiler_params=_params("parallel"),
        name="b_proj",
    )(x, g, win, qn, kvn, wuq, wuk, wuvt, *tabs)


def _with_ones_rows(vt):
    return jnp.concatenate([vt, jnp.ones((16, vt.shape[1]), vt.dtype)], axis=0)


def _attend_keys_major(streams, n_chunks, finish, width):
    items = []
    for s0 in range(0, len(streams), width):
        items += [(s, j) for j in range(n_chunks) for s in range(s0, min(s0 + width, len(streams)))]
    state = [None] * len(streams)

    def scores(item):
        s, j = item
        k_chunk, _, qw, _ = streams[s]
        return _dot_nt(k_chunk(j), qw)

    def absorb(item, st):
        s, j = item
        vt1 = _with_ones_rows(streams[s][1](j))
        bias_chunk = streams[s][3]
        if bias_chunk is not None:
            st = st + bias_chunk(j)
        mj = jnp.max(st, axis=0, keepdims=True)
        if j == 0:
            state[s] = (mj, _dot(vt1, jnp.exp2(st - mj).astype(BF16)))
        else:
            m, acc = state[s]
            m_new = jnp.maximum(m, mj)
            e = jnp.exp2(st - m_new).astype(BF16)
            state[s] = (m_new, acc * jnp.exp2(m - m_new) + _dot(vt1, e))
        if j == n_chunks - 1:
            finish(s, state[s][1])
            state[s] = None

    ahead = ATTN_LOOKAHEAD
    pending = [scores(item) for item in items[:ahead]]
    for idx, item in enumerate(items):
        if idx + ahead < len(items):
            pending.append(scores(items[idx + ahead]))
        absorb(item, pending.pop(0))


def _b_attn_kernel(q_ref, k_ref, vt_ref, o_ref):
    kc, tq = ATTN_KC, ATTN_TQ
    outs = {}

    def stream(t, hh):
        sl = slice(hh * LANES, (hh + 1) * LANES)
        rows = slice(hh * B_V, (hh + 1) * B_V)
        return (lambda j: k_ref[j * kc:(j + 1) * kc, sl],
                lambda j: vt_ref[rows, j * kc:(j + 1) * kc],
                q_ref[t * tq:(t + 1) * tq, sl], None)

    def finish(s, acc):
        t, hh = divmod(s, 2)
        outs[hh] = acc[:B_V] * (1.0 / acc[B_V:B_V + 1])
        if hh == 1:
            o = jnp.concatenate([outs[0], outs[1]], axis=0)
            o_ref[t * tq:(t + 1) * tq, :] = o.T.astype(BF16)

    _attend_keys_major([stream(t, hh) for t in range(ATTN_TILES) for hh in range(2)],
                       SEQ // kc, finish, width=2 * ATTN_TILES)


def _b_attn(q, k, vt):
    rows = ATTN_TQ * ATTN_TILES
    nq = SEQ // rows
    return pl.pallas_call(
        _b_attn_kernel,
        grid=(BATCH, B_HEADS // 2, nq),
        in_specs=[
            pl.BlockSpec((rows, 2 * LANES), lambda b, h, i: (b * nq + i, h)),
            pl.BlockSpec((SEQ, 2 * LANES), lambda b, h, i: (b, h)),
            pl.BlockSpec((2 * B_V, SEQ), lambda b, h, i: (h, b)),
        ],
        out_specs=pl.BlockSpec((rows, LANES), lambda b, h, i: (b * nq + i, h)),
        out_shape=jax.ShapeDtypeStruct((TOKENS, B_HEADS * B_V), BF16),
        compiler_params=_params("parallel", "parallel", "arbitrary"),
        name="b_attn",
    )(q, k, vt)


def _c_group_geometry(gi):
    first_row = gi * C_ROW_GROUP
    wrow = min(max(first_row - NA_ROWS // 2, 0), GRID_ROWS - C_WIN_ROWS)
    n_groups = GRID_ROWS // C_ROW_GROUP
    case = 0 if gi == 0 else (2 if gi == n_groups - 1 else 1)
    return first_row, wrow, case


def _c_row_offset_slot(gi, kr, i):
    first_row, wrow, _ = _c_group_geometry(gi)
    qrow, krow = first_row + i, wrow + kr
    rs = min(max(qrow - NA_ROWS // 2, 0), GRID_ROWS - NA_ROWS)
    if rs <= krow < rs + NA_ROWS:
        return krow - qrow + NA_ROWS - 1
    return 2 * NA_ROWS - 1


def _c_attn_kernel(q_ref, k_ref, vt_ref, left_ref, right_ref, o_ref, bias_ref):
    low = _lane_iota() < HEAD_DIM
    nq = C_ROW_GROUP * GRID_W
    nk = C_WIN_ROWS * GRID_W
    n_groups = GRID_ROWS // C_ROW_GROUP

    @pl.when(pl.program_id(1) == 0)
    def _():
        for case, gi in enumerate((0, 1, n_groups - 1)):
            for kr in range(C_WIN_ROWS):
                for hh in range(2):
                    for ip in range(C_ROW_GROUP // 2):
                        a0 = _c_row_offset_slot(gi, kr, 2 * ip)
                        a1 = _c_row_offset_slot(gi, kr, 2 * ip + 1)
                        col = (hh * C_ROW_GROUP // 2 + ip) * LANES
                        bias_ref[case, kr * GRID_W:(kr + 1) * GRID_W, col:col + LANES] = (
                            left_ref[0, hh, a0] + right_ref[0, hh, a1])

    kc = C_KEY_CHUNK

    def stream(gi):
        first_row, wrow, case = _c_group_geometry(gi)
        q0, ws = first_row * GRID_W, wrow * GRID_W
        qc = q_ref[q0:q0 + nq, :]
        zero = jnp.zeros_like(qc)
        qq = jnp.concatenate([jnp.where(low, qc, zero), jnp.where(low, zero, qc)], axis=0)
        return (lambda j: k_ref[ws + j * kc:ws + (j + 1) * kc, :],
                lambda j: vt_ref[:, ws + j * kc:ws + (j + 1) * kc],
                qq,
                lambda j: bias_ref[case, j * kc:(j + 1) * kc, :])

    def finish(gi, acc):
        q0 = _c_group_geometry(gi)[0] * GRID_W
        ot = acc[:LANES] * (1.0 / acc[LANES:LANES + 1])
        o = jnp.concatenate([ot[:HEAD_DIM, :nq], ot[HEAD_DIM:, nq:]], axis=0)
        o_ref[q0:q0 + nq, :] = o.T.astype(BF16)

    _attend_keys_major([stream(gi) for gi in range(n_groups)], nk // kc, finish, width=C_WIDTH)


def _c_attn(qk, vt, left, right):
    nq = C_ROW_GROUP * GRID_W
    nk = C_WIN_ROWS * GRID_W
    npair = C_HEADS // 2
    tab_spec = pl.BlockSpec((1, 2, 2 * NA_ROWS, GRID_W, LANES), lambda h, b: (h, 0, 0, 0, 0))
    return pl.pallas_call(
        _c_attn_kernel,
        grid=(npair, BATCH),
        in_specs=[
            pl.BlockSpec((SEQ, LANES), lambda h, b: (b, h)),
            pl.BlockSpec((SEQ, LANES), lambda h, b: (b, npair + h)),
            pl.BlockSpec((LANES, SEQ), lambda h, b: (h, b)),
            tab_spec, tab_spec,
        ],
        out_specs=pl.BlockSpec((SEQ, LANES), lambda h, b: (b, h)),
        out_shape=jax.ShapeDtypeStruct((TOKENS, C_HEADS * HEAD_DIM), BF16),
        scratch_shapes=[pltpu.VMEM((3, nk, 2 * nq), F32)],
        compiler_params=_params("arbitrary", "arbitrary"),
        name="c_attn",
    )(qk, qk, vt, left, right)


def _c_column_tables(rpb):
    kc, qc = np.arange(GRID_W)[:, None], np.arange(GRID_W)[None, :]
    win0 = np.clip(qc - NA_COLS // 2, 0, GRID_W - NA_COLS)
    col_ok = (kc >= win0) & (kc < win0 + NA_COLS)
    col_off = np.clip(kc - qc + NA_COLS - 1, 0, 2 * NA_COLS - 2)
    col_sel = (col_off[None] == np.arange(2 * NA_COLS - 1)[:, None, None]).astype(np.float32)
    cols = jnp.einsum("hab,bkq->hakq", rpb, col_sel, precision=lax.Precision.HIGHEST)
    cols = jnp.where(col_ok, cols * LOG2_E, NEG_INF)
    cols = jnp.concatenate([cols, jnp.full((C_HEADS, 1, GRID_W, GRID_W), NEG_INF, F32)], axis=1)
    cols = cols.reshape(C_HEADS // 2, 2, 2 * NA_ROWS, GRID_W, GRID_W)
    zeros = jnp.zeros_like(cols)
    return jnp.concatenate([cols, zeros], axis=-1), jnp.concatenate([zeros, cols], axis=-1)


def _d_attn_kernel(lq1, lk1, lq2, lk2, q_ref, k_ref, vt_ref, sub_ref, o_ref, *, lambda_init):
    lam = (jnp.exp(jnp.sum(lq1[...] * lk1[...], axis=-1, keepdims=True))
           - jnp.exp(jnp.sum(lq2[...] * lk2[...], axis=-1, keepdims=True)) + lambda_init)
    low = _lane_iota() < D_HEAD
    kc, tq, dv = ATTN_KC, ATTN_TQ, 2 * D_HEAD
    k_chunk = lambda j: k_ref[j * kc:(j + 1) * kc, :]
    vt_chunk = lambda j: vt_ref[:, j * kc:(j + 1) * kc]
    outs = {}

    def stream(t, half):
        q = q_ref[t * tq:(t + 1) * tq, :]
        zero = jnp.zeros_like(q)
        qw = jnp.where(low, q, zero) if half == 0 else jnp.where(low, zero, q)
        return (k_chunk, vt_chunk, qw, None)

    def finish(s, acc):
        t, half = divmod(s, 2)
        outs[half] = acc[:dv] * (1.0 / acc[dv:dv + 1])
        if half == 1:
            o = (outs[0] - lam * outs[1]).T
            o_ref[t * tq:(t + 1) * tq, :] = (
                _rms_f32(o, sub_ref[...]) * (1.0 - lambda_init)).astype(BF16)

    _attend_keys_major([stream(t, half) for t in range(ATTN_TILES) for half in range(2)],
                       SEQ // kc, finish, width=2 * ATTN_TILES)


def _d_attn(qk, vt, lq1, lk1, lq2, lk2, subln, lambda_init):
    tq = ATTN_TQ * ATTN_TILES
    nq = SEQ // tq
    vec = pl.BlockSpec((1, D_HEAD), lambda b, h, i: (0, 0))
    return pl.pallas_call(
        functools.partial(_d_attn_kernel, lambda_init=lambda_init),
        grid=(BATCH, D_HEADS, nq),
        in_specs=[
            vec, vec, vec, vec,
            pl.BlockSpec((tq, LANES), lambda b, h, i: (b * nq + i, h)),
            pl.BlockSpec((SEQ, LANES), lambda b, h, i: (b, D_HEADS + h)),
            pl.BlockSpec((LANES, SEQ), lambda b, h, i: (h, b)),
            pl.BlockSpec((1, 2 * D_HEAD), lambda b, h, i: (0, 0)),
        ],
        out_specs=pl.BlockSpec((tq, LANES), lambda b, h, i: (b * nq + i, h)),
        out_shape=jax.ShapeDtypeStruct((TOKENS, 2 * D_HEADS * D_HEAD), BF16),
        compiler_params=_params("parallel", "parallel", "arbitrary"),
        name="d_attn",
    )(lq1, lk1, lq2, lk2, qk, qk, vt, subln)


def _rope_tables(rot_dim, period, lane0):
    r = rot_dim // 2
    inv = ROPE_THETA ** (-jnp.arange(0, rot_dim, 2, dtype=F32) / rot_dim)
    ang = jnp.arange(SEQ, dtype=F32)[:, None] * inv[None, :]
    cos, sin = jnp.cos(ang), jnp.sin(ang)
    lane = np.arange(LANES) % period - lane0
    first = (lane >= 0) & (lane < r)
    second = (lane >= r) & (lane < 2 * r)
    idx = np.where(first, lane, np.where(second, lane - r, 0))
    cg, sg = cos[:, idx], sin[:, idx]
    c = jnp.where(first | second, cg, 1.0)
    sa = jnp.where(first, -sg, 0.0)
    sb = jnp.where(second, sg, 0.0)
    return c, sa, sb


def _a_weights(w_qkv, w_o):
    nh = A_GROUP_HEADS * HEAD_DIM
    pad = jnp.zeros((D_MODEL, A_GROUP_LANES - nh), w_qkv.dtype)

    def cols(part, g):
        base = part * A_HEADS * HEAD_DIM + g * nh
        return [w_qkv[:, base:base + nh], pad]

    pieces = []
    for g in range(len(DIL_PAIRS)):
        pieces += cols(0, g) + cols(1, g)
    for g in range(len(DIL_PAIRS)):
        pieces += cols(2, g)
    w = jnp.concatenate(pieces, axis=1).astype(BF16)
    wo = w_o.reshape(len(DIL_PAIRS), nh, D_MODEL)
    wo = jnp.pad(wo, ((0, 0), (0, A_GROUP_LANES - nh), (0, 0))).astype(BF16)
    return w, wo.reshape(len(DIL_PAIRS) * A_GROUP_LANES, D_MODEL)


def _b_weights(w_in, w_uq, w_ukv):
    split = B_Q_RANK + B_KV_RANK
    win = jnp.concatenate([
        w_in[:, :split], jnp.zeros((D_MODEL, B_NOPE), w_in.dtype), w_in[:, split:],
        jnp.zeros((D_MODEL, LANES - B_NOPE - B_ROPE), w_in.dtype)], axis=1).astype(BF16)
    wuq = w_uq.reshape(B_Q_RANK, B_HEADS, B_NOPE + B_ROPE)
    wuq = jnp.pad(wuq, ((0, 0), (0, 0), (0, LANES - B_NOPE - B_ROPE)))
    wuq = wuq.reshape(B_Q_RANK, B_HEADS * LANES).astype(BF16)
    wukv = w_ukv.reshape(B_KV_RANK, B_HEADS, B_NOPE + B_V)
    wuk = jnp.pad(wukv[:, :, :B_NOPE], ((0, 0), (0, 0), (0, LANES - B_NOPE)))
    wuk = wuk.reshape(B_KV_RANK, B_HEADS * LANES).astype(BF16)
    wuvt = wukv[:, :, B_NOPE:].reshape(B_KV_RANK, B_HEADS * B_V).T.astype(BF16)
    return win, wuq, wuk, wuvt


def _split_qk_vt(w_qkv, n_qk):
    return w_qkv[:, :n_qk].astype(BF16), w_qkv[:, n_qk:].T.astype(BF16)


def kernel(x, p, a_norm, a_w_qkv, a_w_o, b_norm, b_w_in, b_q_norm, b_w_uq, b_kv_norm, b_w_ukv, b_w_o, c_norm, c_w_qkv, c_rpb, c_w_o, d_norm, d_w_qkv, d_lambda_q1, d_lambda_k1, d_lambda_q2, d_lambda_k2, d_subln, d_w_o, mlp_norm, w_up, w_down, ple_norm, w_ple_gate, w_ple_proj, final_norm):
    tabs_p = _rope_tables(ROT_DIM, HEAD_DIM, 0)
    tabs_l = _rope_tables(B_ROPE, LANES, B_NOPE)
    h = x.reshape(TOKENS, D_MODEL)
    row = lambda v: v.reshape(1, -1)

    for i in range(DEPTH):
        rest = (row(mlp_norm[i]), w_up[i].astype(BF16), w_down[i].astype(BF16),
                row(ple_norm[i]), w_ple_gate[i].astype(BF16), p[i].reshape(TOKENS, PLE_DIM),
                w_ple_proj[i].astype(BF16), row(final_norm), i == DEPTH - 1)
        if i == 0:
            w, wo = _a_weights(a_w_qkv[0], a_w_o[0])
            parts = _a_qkv(h, row(a_norm[0]), w, tabs_p)
            os_, ls_ = [], []
            for g, (_, dil) in enumerate(DIL_PAIRS):
                o, lse = _a_attn(parts[3 * g], parts[3 * g + 1], parts[3 * g + 2], dil,
                                 token_order=g < len(DIL_PAIRS) - 1)
                os_.append(o)
                ls_.append(lse)
            h = _tail_plain(h, _a_combine(os_, ls_), wo, *rest, name="a_tail")
        elif i == 1:
            win, wuq, wuk, wuvt = _b_weights(b_w_in[0], b_w_uq[0], b_w_ukv[0])
            q, k, vt = _b_proj(h, row(b_norm[0]), win, row(b_q_norm[0]), row(b_kv_norm[0]),
                               wuq, wuk, wuvt, tabs_l)
            h = _tail_plain(h, _b_attn(q, k, vt), b_w_o[0].astype(BF16), *rest, name="b_tail")
        elif i == 2:
            w, wvt = _split_qk_vt(c_w_qkv[0], 2 * C_HEADS * HEAD_DIM)
            qk, vt = _norm_proj(h, row(c_norm[0]), w, wvt, tabs_p, False, "c_qkv")
            o = _c_attn(qk, vt, *_c_column_tables(c_rpb[0]))
            h = _tail_plain(h, o, c_w_o[0].astype(BF16), *rest, name="c_tail")
        else:
            lambda_init = 0.8 - 0.6 * math.exp(-0.3 * i)
            w, wvt = _split_qk_vt(d_w_qkv[0], 2 * 2 * D_HEADS * D_HEAD)
            qk, vt = _norm_proj(h, row(d_norm[0]), w, wvt, tabs_p, True, "d_qkv")
            o = _d_attn(qk, vt, row(d_lambda_q1[0]), row(d_lambda_k1[0]), row(d_lambda_q2[0]),
                        row(d_lambda_k2[0]), row(d_subln[0]), lambda_init)
            h = _tail_plain(h, o, d_w_o[0].astype(BF16), *rest, name="d_tail")
    return h.reshape(BATCH, SEQ, D_MODEL)
```

```python
import functools
import math

import numpy as np
import jax
import jax.numpy as jnp
from jax import lax
from jax.experimental import pallas as pl
from jax.experimental.pallas import tpu as pltpu

F32 = jnp.float32
BF16 = jnp.bfloat16

D_MODEL = 1024
BATCH = 8
SEQ = 2048
DEPTH = 4
TOKENS = BATCH * SEQ
HEAD_DIM = 64
ROPE_THETA = 500000.0
ROT_DIM = HEAD_DIM // 4
NEG_INF = -1e30
RMS_EPS = 1e-6
LOG2_E = math.log2(math.e)

DIL_PAIRS = ((128, 1), (512, 4), (2048, 16))
A_GROUP_HEADS = 5
A_HEADS = A_GROUP_HEADS * len(DIL_PAIRS)
A_BAND_HALF = 64
A_GROUP_LANES = 384
A_GROUP_CHUNKS = A_GROUP_LANES // 128

B_HEADS = 16
B_Q_RANK = 256
B_KV_RANK = 128
B_NOPE = 64
B_ROPE = 32
B_V = 64

C_HEADS = 16
GRID_W = 64
GRID_ROWS = SEQ // GRID_W
NA_ROWS = 8
NA_COLS = 16
C_ROW_GROUP = 4
C_WIN_ROWS = 12
C_KEY_CHUNK = 768
C_WIDTH = 4

D_HEADS = 8
D_HEAD = 64

MLP_HIDDEN = 4 * D_MODEL
PLE_DIM = 256

LANES = 128
VMEM_LIMIT = 48 * 1024 * 1024

PROJ_TM = 512
MLP_TH = 1024
ATTN_TQ = 512
ATTN_KC = 512
ATTN_TILES = 4
ATTN_LOOKAHEAD = 5

B_Q_SCALE = (B_NOPE + B_ROPE) ** -0.5 * LOG2_E
CD_Q_SCALE = HEAD_DIM ** -0.5 * LOG2_E
BAND_TQ = 128
BAND_UNROLL = 2


def _params(*sem):
    return pltpu.CompilerParams(dimension_semantics=sem, vmem_limit_bytes=VMEM_LIMIT)


def _rms_f32(x, g):
    ms = jnp.mean(x * x, axis=-1, keepdims=True)
    return x * lax.rsqrt(ms + RMS_EPS) * g


def _rope_chunk(y, c, sa, sb, shift):
    return (y * c + pltpu.roll(y, LANES - shift, 1) * sa + pltpu.roll(y, shift, 1) * sb)


def _softmax_parts(s):
    m = jnp.max(s, axis=-1, keepdims=True)
    e = jnp.exp(s - m)
    l = jnp.sum(e, axis=-1, keepdims=True)
    return m, e, l


def _dot(a, b):
    return jnp.dot(a, b, preferred_element_type=F32)


def _dot_nt(a, b):
    return lax.dot_general(a, b, (((1,), (1,)), ((), ())), preferred_element_type=F32)


def _lane_iota():
    return lax.broadcasted_iota(jnp.int32, (1, LANES), 1)


def _norm_proj_kernel(x_ref, g_ref, w_ref, wvt_ref, c_ref, sa_ref, sb_ref, qk_ref, vt_ref,
                      *, rope, tn):
    xn = _rms_f32(x_ref[...], g_ref[...]).astype(BF16)
    n = qk_ref.shape[1]
    c, sa, sb = c_ref[...], sa_ref[...], sb_ref[...]
    for j in range(n // tn):
        y = _dot(xn, w_ref[:, j * tn:(j + 1) * tn])
        for i in range(tn // LANES):
            col = j * tn + i * LANES
            chunk = y[:, i * LANES:(i + 1) * LANES]
            if rope:
                chunk = _rope_chunk(chunk, c, sa, sb, ROT_DIM // 2)
            if col < n // 2:
                chunk = chunk * CD_Q_SCALE
            qk_ref[:, col:col + LANES] = chunk.astype(BF16)
    vt_ref[...] = _dot_nt(wvt_ref[...], xn).astype(BF16)


def _norm_proj(x, g, w, wvt, tabs, rope, name):
    n = w.shape[1]
    nv = wvt.shape[0]
    tm = PROJ_TM
    nb = SEQ // tm
    tab_spec = pl.BlockSpec((tm, LANES), lambda i: (i % nb, 0))
    return pl.pallas_call(
        functools.partial(_norm_proj_kernel, rope=rope, tn=512),
        grid=(TOKENS // tm,),
        in_specs=[
            pl.BlockSpec((tm, D_MODEL), lambda i: (i, 0)),
            pl.BlockSpec((1, D_MODEL), lambda i: (0, 0)),
            pl.BlockSpec((D_MODEL, n), lambda i: (0, 0)),
            pl.BlockSpec((nv, D_MODEL), lambda i: (0, 0)),
            tab_spec, tab_spec, tab_spec,
        ],
        out_specs=[pl.BlockSpec((tm, n), lambda i: (i, 0)),
                   pl.BlockSpec((nv, tm), lambda i: (0, i))],
        out_shape=[jax.ShapeDtypeStruct((TOKENS, n), BF16),
                   jax.ShapeDtypeStruct((nv, TOKENS), BF16)],
        compiler_params=_params("parallel"),
        name=name,
    )(x, g, w, wvt, *tabs)


def _mixer_residual(x_ref, o_ref, w_ref):
    return x_ref[...] + _dot(o_ref[...], w_ref[...])


def _tail_kernel(*refs, n_front, front, final):
    front_refs = refs[:n_front]
    gm_ref, wu_ref, wd_ref, gp_ref, wg_ref, p_ref, wp_ref, fg_ref, out_ref = refs[n_front:]
    x1 = front(*front_refs)
    xn = _rms_f32(x1, gm_ref[...]).astype(BF16)
    acc = x1
    for j in range(MLP_HIDDEN // MLP_TH):
        h = _dot(xn, wu_ref[:, j * MLP_TH:(j + 1) * MLP_TH])
        h = jnp.square(jnp.maximum(h, 0.0)).astype(BF16)
        acc = acc + _dot(h, wd_ref[j * MLP_TH:(j + 1) * MLP_TH, :])
    gate = jax.nn.sigmoid(_dot(_rms_f32(acc, gp_ref[...]).astype(BF16), wg_ref[...]))
    y = acc + gate * _dot(p_ref[...].astype(BF16), wp_ref[...])
    if final:
        y = _rms_f32(y, fg_ref[...])
    out_ref[...] = y


def _resident(a):
    return pl.BlockSpec(a.shape, lambda i: (0,) * a.ndim, pipeline_mode=pl.Buffered(1))


def _tail(front, front_args, front_specs, gm, wu, wd, gp, wg, p, wp, fg, final, name):
    tm = PROJ_TM
    shared = (gm, wu, wd, gp, wg)
    return pl.pallas_call(
        functools.partial(_tail_kernel, n_front=len(front_args), front=front, final=final),
        grid=(TOKENS // tm,),
        in_specs=list(front_specs) + [_resident(a) for a in shared] + [
            pl.BlockSpec((tm, PLE_DIM), lambda i: (i, 0)), _resident(wp), _resident(fg)],
        out_specs=pl.BlockSpec((tm, D_MODEL), lambda i: (i, 0)),
        out_shape=jax.ShapeDtypeStruct((TOKENS, D_MODEL), F32),
        compiler_params=_params("parallel"),
        name=name,
    )(*front_args, *shared, p, wp, fg)


def _tail_plain(x, o, w_o, *rest, name):
    tm = PROJ_TM
    specs = [pl.BlockSpec((tm, D_MODEL), lambda i: (i, 0)),
             pl.BlockSpec((tm, o.shape[1]), lambda i: (i, 0)), _resident(w_o)]
    return _tail(_mixer_residual, (x, o, w_o), specs, *rest, name=name)


def _a_qkv_kernel(x_ref, g_ref, w_ref, c_ref, sa_ref, sb_ref, *refs):
    outs, stage = refs[:9], refs[9]
    tm = x_ref.shape[0]
    gl = A_GROUP_LANES
    nc = A_GROUP_CHUNKS
    xn = _rms_f32(x_ref[...], g_ref[...]).astype(BF16)
    c, sa, sb = c_ref[...], sa_ref[...], sb_ref[...]

    def emit(dil, dsts):
        n = tm // dil
        for k, dst in enumerate(dsts):
            for ci in range(nc):
                for r in range(dil):
                    if dil == 1:
                        rows = stage[k * nc + ci]
                    else:
                        rows = stage[k * nc + ci, pl.ds(r, n, stride=dil), :]
                    dst[0, ci, r] = rows.astype(BF16)

    for g, (_, dil) in enumerate(DIL_PAIRS):
        y = _dot(xn, w_ref[:, 2 * gl * g:2 * gl * (g + 1)])
        for i in range(2 * nc):
            stage[i] = _rope_chunk(y[:, i * LANES:(i + 1) * LANES], c, sa, sb, ROT_DIM // 2)
        emit(dil, (outs[3 * g], outs[3 * g + 1]))
    for g, (_, dil) in enumerate(DIL_PAIRS):
        y = _dot(xn, w_ref[:, 6 * gl + gl * g:6 * gl + gl * (g + 1)])
        for i in range(nc):
            stage[i] = y[:, i * LANES:(i + 1) * LANES]
        emit(dil, (outs[3 * g + 2],))


def _a_qkv(x, g, w, tabs):
    tm = PROJ_TM
    nb = SEQ // tm
    gl = A_GROUP_LANES
    tab_spec = pl.BlockSpec((tm, LANES), lambda i: (i % nb, 0))
    out_shapes, out_specs = [], []
    nc = A_GROUP_CHUNKS
    for _, dil in DIL_PAIRS:
        for _ in range(3):
            out_shapes.append(jax.ShapeDtypeStruct((BATCH, nc, dil, SEQ // dil, LANES), BF16))
            out_specs.append(pl.BlockSpec((1, nc, dil, tm // dil, LANES),
                                          lambda i: (i // nb, 0, 0, i % nb, 0)))
    return pl.pallas_call(
        _a_qkv_kernel,
        grid=(TOKENS // tm,),
        in_specs=[
            pl.BlockSpec((tm, D_MODEL), lambda i: (i, 0)),
            pl.BlockSpec((1, D_MODEL), lambda i: (0, 0)),
            pl.BlockSpec((D_MODEL, 9 * gl), lambda i: (0, 0)),
            tab_spec, tab_spec, tab_spec,
        ],
        out_specs=out_specs,
        out_shape=out_shapes,
        scratch_shapes=[pltpu.VMEM((2 * nc, tm, LANES), F32)],
        compiler_params=_params("parallel"),
        name="a_qkv",
    )(x, g, w, *tabs)


def _a_attn_kernel(q_ref, k_ref, v_ref, o_ref, lse_ref, *, seg, dil, token_order):
    tq = BAND_TQ
    kw = tq + 2 * A_BAND_HALF
    low = _lane_iota() < HEAD_DIM
    ones = jnp.ones((kw, LANES), BF16)

    def tile(u):
        p0 = pl.multiple_of(u * tq, tq)
        if seg >= kw:
            seg0 = (p0 // seg) * seg
            ws = jnp.clip(p0 - A_BAND_HALF, seg0, seg0 + seg - kw)
        else:
            ws = (p0 // kw) * kw
        ws = pl.multiple_of(ws, A_BAND_HALF)
        rows = p0 + lax.broadcasted_iota(jnp.int32, (tq, 1), 0)
        cols = ws + lax.broadcasted_iota(jnp.int32, (1, kw), 1)
        valid = jnp.abs(cols - rows) <= A_BAND_HALF
        if seg < kw:
            seg_of = lambda p: lax.shift_right_logical(p, seg.bit_length() - 1)
            valid = jnp.logical_and(valid, seg_of(cols) == seg_of(rows))
        r = p0 // seg
        q0 = p0 - r * seg
        for c in range(A_GROUP_CHUNKS):
            qc = q_ref[0, c, pl.ds(p0, tq), :] * jnp.asarray(HEAD_DIM ** -0.5, BF16)
            kc = k_ref[0, c, pl.ds(ws, kw), :]
            v1 = jnp.concatenate([v_ref[0, c, pl.ds(ws, kw), :], ones], axis=1)
            outs, lses = [], []
            n_heads = 2 if 2 * c + 1 < A_GROUP_HEADS else 1
            for hh in range(n_heads):
                qm = jnp.where(low if hh == 0 else jnp.logical_not(low), qc, jnp.zeros_like(qc))
                s = jnp.where(valid, _dot_nt(qm, kc), NEG_INF)
                m = jnp.max(s, axis=-1, keepdims=True)
                ol = _dot(jnp.exp(s - m).astype(BF16), v1)
                l = ol[:, LANES:]
                outs.append(ol[:, :LANES] * (1.0 / l))
                lses.append(m + jnp.log(l))
            if n_heads == 2:
                o = jnp.where(low, outs[0], outs[1])
                ls = jnp.where(low, lses[0], lses[1])
            else:
                o = jnp.where(low, outs[0], 0.0)
                ls = jnp.where(low, lses[0], 0.0)
            if not token_order or dil == 1:
                dst = pl.ds(p0, tq)
            else:
                dst = pl.ds(r + dil * q0, tq, stride=dil)
            o_ref[0, c, dst, :] = o
            lse_ref[0, c, dst, :] = ls

    def body(i, carry):
        for j in range(BAND_UNROLL):
            tile(i * BAND_UNROLL + j)
        return carry

    lax.fori_loop(0, SEQ // tq // BAND_UNROLL, body, 0)


def _a_attn(q, k, v, dil, token_order):
    seg = SEQ // dil
    nc = A_GROUP_CHUNKS
    q, k, v = (a.reshape(BATCH, nc, SEQ, LANES) for a in (q, k, v))
    in_spec = pl.BlockSpec((1, nc, SEQ, LANES), lambda b: (b, 0, 0, 0))
    out_spec = pl.BlockSpec((1, nc, SEQ, LANES), lambda b: (b, 0, 0, 0))
    out_shape = jax.ShapeDtypeStruct((BATCH, nc, SEQ, LANES), F32)
    return pl.pallas_call(
        functools.partial(_a_attn_kernel, seg=seg, dil=dil, token_order=token_order),
        grid=(BATCH,),
        in_specs=[in_spec, in_spec, in_spec],
        out_specs=[out_spec, out_spec],
        out_shape=[out_shape, out_shape],
        compiler_params=_params("parallel"),
        name=f"a_attn_d{dil}",
    )(q, k, v)


def _a_combine_kernel(o0, o1, o2, l0, l1, l2, out_ref):
    tm = out_ref.shape[0]
    dil = DIL_PAIRS[-1][1]
    seg = SEQ // dil
    pos0 = (pl.program_id(0) % (SEQ // tm)) * (tm // dil)

    def token_rows(ref, c):
        return jnp.concatenate(
            [ref[0, c, pl.ds(pos0 + p, dil, stride=seg), :] for p in range(tm // dil)], axis=0)

    for c in range(A_GROUP_CHUNKS):
        ls = [l0[0, c], l1[0, c], token_rows(l2, c)]
        os_ = [o0[0, c], o1[0, c], token_rows(o2, c)]
        m = jnp.maximum(jnp.maximum(ls[0], ls[1]), ls[2])
        es = [jnp.exp(l - m) for l in ls]
        inv = 1.0 / (es[0] + es[1] + es[2])
        for g in range(len(DIL_PAIRS)):
            col = g * A_GROUP_LANES + c * LANES
            out_ref[:, col:col + LANES] = (os_[g] * (es[g] * inv)).astype(BF16)


def _a_combine(os_, ls_):
    tm = PROJ_TM
    nb = SEQ // tm
    n = len(DIL_PAIRS) * A_GROUP_LANES
    part = pl.BlockSpec((1, A_GROUP_CHUNKS, tm, LANES), lambda i: (i // nb, 0, i % nb, 0))
    whole = pl.BlockSpec((1, A_GROUP_CHUNKS, SEQ, LANES), lambda i: (i // nb, 0, 0, 0))
    return pl.pallas_call(
        _a_combine_kernel,
        grid=(TOKENS // tm,),
        in_specs=[part, part, whole, part, part, whole],
        out_specs=pl.BlockSpec((tm, n), lambda i: (i, 0)),
        out_shape=jax.ShapeDtypeStruct((TOKENS, n), BF16),
        compiler_params=_params("parallel"),
        name="a_combine",
    )(*os_, *ls_)


def _b_proj_kernel(x_ref, g_ref, win_ref, qn_ref, kvn_ref, wuq_ref, wuqr_ref, wuk_ref, wuvt_ref,
                   c_ref, sa_ref, sb_ref, q_out, k_out, vt_out):
    c = c_ref[...]
    s = sb_ref[...] - sa_ref[...]
    slot = B_Q_RANK + B_KV_RANK
    xn = _rms_f32(x_ref[...], g_ref[...]).astype(BF16)
    z = _dot(xn, win_ref[...])
    cq = _rms_f32(z[:, :B_Q_RANK], qn_ref[...]).astype(BF16)
    ckv = _rms_f32(z[:, B_Q_RANK:slot], kvn_ref[...]).astype(BF16)
    k_rope = z[:, slot:slot + LANES] * c + z[:, slot + LANES:] * s
    q = _dot(cq, wuq_ref[...])
    qr = _dot(cq, wuqr_ref[...])
    k = _dot(ckv, wuk_ref[...])
    cq_tab, sq_tab = c * B_Q_SCALE, s * B_Q_SCALE
    for h in range(B_HEADS):
        sl = slice(h * LANES, (h + 1) * LANES)
        q_out[:, sl] = (q[:, sl] * cq_tab + qr[:, sl] * sq_tab).astype(BF16)
        k_out[:, sl] = (k[:, sl] + k_rope).astype(BF16)
    vt_out[...] = _dot_nt(wuvt_ref[...], ckv).astype(BF16)


def _b_proj(x, g, win, qn, kvn, wuq, wuqr, wuk, wuvt, tabs):
    tm = PROJ_TM
    nb = SEQ // tm
    tab_spec = pl.BlockSpec((tm, LANES), lambda i: (i % nb, 0))

    def full(a):
        return pl.BlockSpec(a.shape, lambda i: (0,) * a.ndim)

    nqk = B_HEADS * LANES
    nv = B_HEADS * B_V
    return pl.pallas_call(
        _b_proj_kernel,
        grid=(TOKENS // tm,),
        in_specs=[pl.BlockSpec((tm, D_MODEL), lambda i: (i, 0)), full(g), full(win), full(qn),
                  full(kvn), full(wuq), full(wuqr), full(wuk), full(wuvt),
                  tab_spec, tab_spec, tab_spec],
        out_specs=[pl.BlockSpec((tm, nqk), lambda i: (i, 0)),
                   pl.BlockSpec((tm, nqk), lambda i: (i, 0)),
                   pl.BlockSpec((nv, tm), lambda i: (0, i))],
        out_shape=[jax.ShapeDtypeStruct((TOKENS, nqk), BF16),
                   jax.ShapeDtypeStruct((TOKENS, nqk), BF16),
                   jax.ShapeDtypeStruct((nv, TOKENS), BF16)],
        compiler_params=_params("parallel"),
        name="b_proj",
    )(x, g, win, qn, kvn, wuq, wuqr, wuk, wuvt, *tabs)


def _with_ones_rows(vt):
    return jnp.concatenate([vt, jnp.ones((16, vt.shape[1]), vt.dtype)], axis=0)


def _attend_keys_major(streams, n_chunks, finish, width):
    items = []
    for s0 in range(0, len(streams), width):
        items += [(s, j) for j in range(n_chunks) for s in range(s0, min(s0 + width, len(streams)))]
    state = [None] * len(streams)

    def scores(item):
        s, j = item
        k_chunk, _, qw, _ = streams[s]
        return _dot_nt(k_chunk(j), qw)

    def absorb(item, st):
        s, j = item
        vt1 = _with_ones_rows(streams[s][1](j))
        bias_chunk = streams[s][3]
        if bias_chunk is not None:
            st = st + bias_chunk(j)
        mj = jnp.max(st, axis=0, keepdims=True)
        if j == 0:
            state[s] = (mj, _dot(vt1, jnp.exp2(st - mj).astype(BF16)))
        else:
            m, acc = state[s]
            m_new = jnp.maximum(m, mj)
            e = jnp.exp2(st - m_new).astype(BF16)
            state[s] = (m_new, acc * jnp.exp2(m - m_new) + _dot(vt1, e))
        if j == n_chunks - 1:
            finish(s, state[s][1])
            state[s] = None

    ahead = ATTN_LOOKAHEAD
    pending = [scores(item) for item in items[:ahead]]
    for idx, item in enumerate(items):
        if idx + ahead < len(items):
            pending.append(scores(items[idx + ahead]))
        absorb(item, pending.pop(0))


def _b_attn_kernel(q_ref, k_ref, vt_ref, o_ref):
    kc, tq = ATTN_KC, ATTN_TQ
    outs = {}

    def stream(t, hh):
        sl = slice(hh * LANES, (hh + 1) * LANES)
        rows = slice(hh * B_V, (hh + 1) * B_V)
        return (lambda j: k_ref[j * kc:(j + 1) * kc, sl],
                lambda j: vt_ref[rows, j * kc:(j + 1) * kc],
                q_ref[t * tq:(t + 1) * tq, sl], None)

    def finish(s, acc):
        t, hh = divmod(s, 2)
        outs[hh] = acc[:B_V] * (1.0 / acc[B_V:B_V + 1])
        if hh == 1:
            o = jnp.concatenate([outs[0], outs[1]], axis=0)
            o_ref[t * tq:(t + 1) * tq, :] = o.T.astype(BF16)

    _attend_keys_major([stream(t, hh) for t in range(ATTN_TILES) for hh in range(2)],
                       SEQ // kc, finish, width=2 * ATTN_TILES)


def _b_attn(q, k, vt):
    rows = ATTN_TQ * ATTN_TILES
    nq = SEQ // rows
    return pl.pallas_call(
        _b_attn_kernel,
        grid=(BATCH, B_HEADS // 2, nq),
        in_specs=[
            pl.BlockSpec((rows, 2 * LANES), lambda b, h, i: (b * nq + i, h)),
            pl.BlockSpec((SEQ, 2 * LANES), lambda b, h, i: (b, h)),
            pl.BlockSpec((2 * B_V, SEQ), lambda b, h, i: (h, b)),
        ],
        out_specs=pl.BlockSpec((rows, LANES), lambda b, h, i: (b * nq + i, h)),
        out_shape=jax.ShapeDtypeStruct((TOKENS, B_HEADS * B_V), BF16),
        compiler_params=_params("parallel", "parallel", "arbitrary"),
        name="b_attn",
    )(q, k, vt)


def _c_group_geometry(gi):
    first_row = gi * C_ROW_GROUP
    wrow = min(max(first_row - NA_ROWS // 2, 0), GRID_ROWS - C_WIN_ROWS)
    n_groups = GRID_ROWS // C_ROW_GROUP
    case = 0 if gi == 0 else (2 if gi == n_groups - 1 else 1)
    return first_row, wrow, case


def _c_row_offset_slot(gi, kr, i):
    first_row, wrow, _ = _c_group_geometry(gi)
    qrow, krow = first_row + i, wrow + kr
    rs = min(max(qrow - NA_ROWS // 2, 0), GRID_ROWS - NA_ROWS)
    if rs <= krow < rs + NA_ROWS:
        return krow - qrow + NA_ROWS - 1
    return 2 * NA_ROWS - 1


def _c_attn_kernel(q_ref, k_ref, vt_ref, left_ref, right_ref, o_ref, bias_ref):
    low = _lane_iota() < HEAD_DIM
    nq = C_ROW_GROUP * GRID_W
    nk = C_WIN_ROWS * GRID_W
    n_groups = GRID_ROWS // C_ROW_GROUP

    @pl.when(pl.program_id(1) == 0)
    def _():
        for case, gi in enumerate((0, 1, n_groups - 1)):
            for kr in range(C_WIN_ROWS):
                for hh in range(2):
                    for ip in range(C_ROW_GROUP // 2):
                        a0 = _c_row_offset_slot(gi, kr, 2 * ip)
                        a1 = _c_row_offset_slot(gi, kr, 2 * ip + 1)
                        col = (hh * C_ROW_GROUP // 2 + ip) * LANES
                        bias_ref[case, kr * GRID_W:(kr + 1) * GRID_W, col:col + LANES] = (
                            left_ref[0, hh, a0] + right_ref[0, hh, a1])

    kc = C_KEY_CHUNK

    def stream(gi):
        first_row, wrow, case = _c_group_geometry(gi)
        q0, ws = first_row * GRID_W, wrow * GRID_W
        qc = q_ref[q0:q0 + nq, :]
        zero = jnp.zeros_like(qc)
        qq = jnp.concatenate([jnp.where(low, qc, zero), jnp.where(low, zero, qc)], axis=0)
        return (lambda j: k_ref[ws + j * kc:ws + (j + 1) * kc, :],
                lambda j: vt_ref[:, ws + j * kc:ws + (j + 1) * kc],
                qq,
                lambda j: bias_ref[case, j * kc:(j + 1) * kc, :])

    def finish(gi, acc):
        q0 = _c_group_geometry(gi)[0] * GRID_W
        ot = acc[:LANES] * (1.0 / acc[LANES:LANES + 1])
        o = jnp.concatenate([ot[:HEAD_DIM, :nq], ot[HEAD_DIM:, nq:]], axis=0)
        o_ref[q0:q0 + nq, :] = o.T.astype(BF16)

    _attend_keys_major([stream(gi) for gi in range(n_groups)], nk // kc, finish, width=C_WIDTH)


def _c_attn(qk, vt, left, right):
    nq = C_ROW_GROUP * GRID_W
    nk = C_WIN_ROWS * GRID_W
    npair = C_HEADS // 2
    tab_spec = pl.BlockSpec((1, 2, 2 * NA_ROWS, GRID_W, LANES), lambda h, b: (h, 0, 0, 0, 0))
    return pl.pallas_call(
        _c_attn_kernel,
        grid=(npair, BATCH),
        in_specs=[
            pl.BlockSpec((SEQ, LANES), lambda h, b: (b, h)),
            pl.BlockSpec((SEQ, LANES), lambda h, b: (b, npair + h)),
            pl.BlockSpec((LANES, SEQ), lambda h, b: (h, b)),
            tab_spec, tab_spec,
        ],
        out_specs=pl.BlockSpec((SEQ, LANES), lambda h, b: (b, h)),
        out_shape=jax.ShapeDtypeStruct((TOKENS, C_HEADS * HEAD_DIM), BF16),
        scratch_shapes=[pltpu.VMEM((3, nk, 2 * nq), F32)],
        compiler_params=_params("arbitrary", "arbitrary"),
        name="c_attn",
    )(qk, qk, vt, left, right)


def _c_column_tables(rpb):
    kc, qc = np.arange(GRID_W)[:, None], np.arange(GRID_W)[None, :]
    win0 = np.clip(qc - NA_COLS // 2, 0, GRID_W - NA_COLS)
    col_ok = (kc >= win0) & (kc < win0 + NA_COLS)
    col_off = np.clip(kc - qc + NA_COLS - 1, 0, 2 * NA_COLS - 2)
    col_sel = (col_off[None] == np.arange(2 * NA_COLS - 1)[:, None, None]).astype(np.float32)
    cols = jnp.einsum("hab,bkq->hakq", rpb, col_sel, precision=lax.Precision.HIGHEST)
    cols = jnp.where(col_ok, cols * LOG2_E, NEG_INF)
    cols = jnp.concatenate([cols, jnp.full((C_HEADS, 1, GRID_W, GRID_W), NEG_INF, F32)], axis=1)
    cols = cols.reshape(C_HEADS // 2, 2, 2 * NA_ROWS, GRID_W, GRID_W)
    zeros = jnp.zeros_like(cols)
    return jnp.concatenate([cols, zeros], axis=-1), jnp.concatenate([zeros, cols], axis=-1)


def _d_attn_kernel(lq1, lk1, lq2, lk2, q_ref, k_ref, vt_ref, sub_ref, o_ref, *, lambda_init):
    lam = (jnp.exp(jnp.sum(lq1[...] * lk1[...], axis=-1, keepdims=True))
           - jnp.exp(jnp.sum(lq2[...] * lk2[...], axis=-1, keepdims=True)) + lambda_init)
    low = _lane_iota() < D_HEAD
    kc, tq, dv = ATTN_KC, ATTN_TQ, 2 * D_HEAD
    k_chunk = lambda j: k_ref[j * kc:(j + 1) * kc, :]
    vt_chunk = lambda j: vt_ref[:, j * kc:(j + 1) * kc]
    outs = {}

    def stream(t, half):
        q = q_ref[t * tq:(t + 1) * tq, :]
        zero = jnp.zeros_like(q)
        qw = jnp.where(low, q, zero) if half == 0 else jnp.where(low, zero, q)
        return (k_chunk, vt_chunk, qw, None)

    def finish(s, acc):
        t, half = divmod(s, 2)
        outs[half] = acc[:dv] * (1.0 / acc[dv:dv + 1])
        if half == 1:
            o = (outs[0] - lam * outs[1]).T
            o_ref[t * tq:(t + 1) * tq, :] = (
                _rms_f32(o, sub_ref[...]) * (1.0 - lambda_init)).astype(BF16)

    _attend_keys_major([stream(t, half) for t in range(ATTN_TILES) for half in range(2)],
                       SEQ // kc, finish, width=2 * ATTN_TILES)


def _d_attn(qk, vt, lq1, lk1, lq2, lk2, subln, lambda_init):
    tq = ATTN_TQ * ATTN_TILES
    nq = SEQ // tq
    vec = pl.BlockSpec((1, D_HEAD), lambda b, h, i: (0, 0))
    return pl.pallas_call(
        functools.partial(_d_attn_kernel, lambda_init=lambda_init),
        grid=(BATCH, D_HEADS, nq),
        in_specs=[
            vec, vec, vec, vec,
            pl.BlockSpec((tq, LANES), lambda b, h, i: (b * nq + i, h)),
            pl.BlockSpec((SEQ, LANES), lambda b, h, i: (b, D_HEADS + h)),
            pl.BlockSpec((LANES, SEQ), lambda b, h, i: (h, b)),
            pl.BlockSpec((1, 2 * D_HEAD), lambda b, h, i: (0, 0)),
        ],
        out_specs=pl.BlockSpec((tq, LANES), lambda b, h, i: (b * nq + i, h)),
        out_shape=jax.ShapeDtypeStruct((TOKENS, 2 * D_HEADS * D_HEAD), BF16),
        compiler_params=_params("parallel", "parallel", "arbitrary"),
        name="d_attn",
    )(lq1, lk1, lq2, lk2, qk, qk, vt, subln)


def _rope_tables(rot_dim, period, lane0):
    r = rot_dim // 2
    inv = ROPE_THETA ** (-jnp.arange(0, rot_dim, 2, dtype=F32) / rot_dim)
    ang = jnp.arange(SEQ, dtype=F32)[:, None] * inv[None, :]
    cos, sin = jnp.cos(ang), jnp.sin(ang)
    lane = np.arange(LANES) % period - lane0
    first = (lane >= 0) & (lane < r)
    second = (lane >= r) & (lane < 2 * r)
    idx = np.where(first, lane, np.where(second, lane - r, 0))
    cg, sg = cos[:, idx], sin[:, idx]
    c = jnp.where(first | second, cg, 1.0)
    sa = jnp.where(first, -sg, 0.0)
    sb = jnp.where(second, sg, 0.0)
    return c, sa, sb


def _a_weights(w_qkv, w_o):
    nh = A_GROUP_HEADS * HEAD_DIM
    pad = jnp.zeros((D_MODEL, A_GROUP_LANES - nh), w_qkv.dtype)

    def cols(part, g):
        base = part * A_HEADS * HEAD_DIM + g * nh
        return [w_qkv[:, base:base + nh], pad]

    pieces = []
    for g in range(len(DIL_PAIRS)):
        pieces += cols(0, g) + cols(1, g)
    for g in range(len(DIL_PAIRS)):
        pieces += cols(2, g)
    w = jnp.concatenate(pieces, axis=1).astype(BF16)
    wo = w_o.reshape(len(DIL_PAIRS), nh, D_MODEL)
    wo = jnp.pad(wo, ((0, 0), (0, A_GROUP_LANES - nh), (0, 0))).astype(BF16)
    return w, wo.reshape(len(DIL_PAIRS) * A_GROUP_LANES, D_MODEL)


def _rotate_half_columns(w_rope):
    r = B_ROPE // 2
    return jnp.concatenate([-w_rope[..., r:], w_rope[..., :r]], axis=-1)


def _b_weights(w_in, w_uq, w_ukv):
    split = B_Q_RANK + B_KV_RANK
    pad_lo = jnp.zeros((D_MODEL, B_NOPE), w_in.dtype)
    pad_hi = jnp.zeros((D_MODEL, LANES - B_NOPE - B_ROPE), w_in.dtype)
    win = jnp.concatenate([
        w_in[:, :split], pad_lo, w_in[:, split:], pad_hi,
        pad_lo, _rotate_half_columns(w_in[:, split:]), pad_hi], axis=1).astype(BF16)
    wuq = w_uq.reshape(B_Q_RANK, B_HEADS, B_NOPE + B_ROPE)
    slot_pad = ((0, 0), (0, 0), (0, LANES - B_NOPE - B_ROPE))
    wuqr = jnp.concatenate([jnp.zeros_like(wuq[:, :, :B_NOPE]),
                            _rotate_half_columns(wuq[:, :, B_NOPE:])], axis=-1)
    wuqr = jnp.pad(wuqr, slot_pad).reshape(B_Q_RANK, B_HEADS * LANES).astype(BF16)
    wuq = jnp.pad(wuq, slot_pad)
    wuq = wuq.reshape(B_Q_RANK, B_HEADS * LANES).astype(BF16)
    wukv = w_ukv.reshape(B_KV_RANK, B_HEADS, B_NOPE + B_V)
    wuk = jnp.pad(wukv[:, :, :B_NOPE], ((0, 0), (0, 0), (0, LANES - B_NOPE)))
    wuk = wuk.reshape(B_KV_RANK, B_HEADS * LANES).astype(BF16)
    wuvt = wukv[:, :, B_NOPE:].reshape(B_KV_RANK, B_HEADS * B_V).T.astype(BF16)
    return win, wuq, wuqr, wuk, wuvt


def _split_qk_vt(w_qkv, n_qk):
    return w_qkv[:, :n_qk].astype(BF16), w_qkv[:, n_qk:].T.astype(BF16)


def kernel(x, p, a_norm, a_w_qkv, a_w_o, b_norm, b_w_in, b_q_norm, b_w_uq, b_kv_norm, b_w_ukv, b_w_o, c_norm, c_w_qkv, c_rpb, c_w_o, d_norm, d_w_qkv, d_lambda_q1, d_lambda_k1, d_lambda_q2, d_lambda_k2, d_subln, d_w_o, mlp_norm, w_up, w_down, ple_norm, w_ple_gate, w_ple_proj, final_norm):
    tabs_p = _rope_tables(ROT_DIM, HEAD_DIM, 0)
    tabs_l = _rope_tables(B_ROPE, LANES, B_NOPE)
    h = x.reshape(TOKENS, D_MODEL)
    row = lambda v: v.reshape(1, -1)

    for i in range(DEPTH):
        rest = (row(mlp_norm[i]), w_up[i].astype(BF16), w_down[i].astype(BF16),
                row(ple_norm[i]), w_ple_gate[i].astype(BF16), p[i].reshape(TOKENS, PLE_DIM),
                w_ple_proj[i].astype(BF16), row(final_norm), i == DEPTH - 1)
        if i == 0:
            w, wo = _a_weights(a_w_qkv[0], a_w_o[0])
            parts = _a_qkv(h, row(a_norm[0]), w, tabs_p)
            os_, ls_ = [], []
            for g, (_, dil) in enumerate(DIL_PAIRS):
                o, lse = _a_attn(parts[3 * g], parts[3 * g + 1], parts[3 * g + 2], dil,
                                 token_order=g < len(DIL_PAIRS) - 1)
                os_.append(o)
                ls_.append(lse)
            h = _tail_plain(h, _a_combine(os_, ls_), wo, *rest, name="a_tail")
        elif i == 1:
            b_w = _b_weights(b_w_in[0], b_w_uq[0], b_w_ukv[0])
            q, k, vt = _b_proj(h, row(b_norm[0]), b_w[0], row(b_q_norm[0]), row(b_kv_norm[0]),
                               *b_w[1:], tabs_l)
            h = _tail_plain(h, _b_attn(q, k, vt), b_w_o[0].astype(BF16), *rest, name="b_tail")
        elif i == 2:
            w, wvt = _split_qk_vt(c_w_qkv[0], 2 * C_HEADS * HEAD_DIM)
            qk, vt = _norm_proj(h, row(c_norm[0]), w, wvt, tabs_p, False, "c_qkv")
            o = _c_attn(qk, vt, *_c_column_tables(c_rpb[0]))
            h = _tail_plain(h, o, c_w_o[0].astype(BF16), *rest, name="c_tail")
        else:
            lambda_init = 0.8 - 0.6 * math.exp(-0.3 * i)
            w, wvt = _split_qk_vt(d_w_qkv[0], 2 * 2 * D_HEADS * D_HEAD)
            qk, vt = _norm_proj(h, row(d_norm[0]), w, wvt, tabs_p, True, "d_qkv")
            o = _d_attn(qk, vt, row(d_lambda_q1[0]), row(d_lambda_k1[0]), row(d_lambda_q2[0]),
                        row(d_lambda_k2[0]), row(d_subln[0]), lambda_init)
            h = _tail_plain(h, o, d_w_o[0].astype(BF16), *rest, name="d_tail")
    return h.reshape(BATCH, SEQ, D_MODEL)
```

```python
import functools
import math

import numpy as np
import jax
import jax.numpy as jnp
from jax import lax
from jax.experimental import pallas as pl
from jax.experimental.pallas import tpu as pltpu

F32 = jnp.float32
BF16 = jnp.bfloat16

D_MODEL = 1024
BATCH = 8
SEQ = 2048
DEPTH = 4
TOKENS = BATCH * SEQ
HEAD_DIM = 64
ROPE_THETA = 500000.0
ROT_DIM = HEAD_DIM // 4
NEG_INF = -1e30
RMS_EPS = 1e-6
LOG2_E = math.log2(math.e)

DIL_PAIRS = ((128, 1), (512, 4), (2048, 16))
A_GROUP_HEADS = 5
A_HEADS = A_GROUP_HEADS * len(DIL_PAIRS)
A_BAND_HALF = 64
A_GROUP_LANES = 384
A_GROUP_CHUNKS = A_GROUP_LANES // 128

B_HEADS = 16
B_Q_RANK = 256
B_KV_RANK = 128
B_NOPE = 64
B_ROPE = 32
B_V = 64

C_HEADS = 16
GRID_W = 64
GRID_ROWS = SEQ // GRID_W
NA_ROWS = 8
NA_COLS = 16
C_ROW_GROUP = 4
C_WIN_ROWS = 12
C_KEY_CHUNK = 768
C_WIDTH = 4

D_HEADS = 8
D_HEAD = 64

MLP_HIDDEN = 4 * D_MODEL
PLE_DIM = 256

LANES = 128
VMEM_LIMIT = 48 * 1024 * 1024

PROJ_TM = 512
MLP_TH = 1024
ATTN_TQ = 512
ATTN_KC = 512
ATTN_TILES = 4
ATTN_LOOKAHEAD = 5

B_Q_SCALE = (B_NOPE + B_ROPE) ** -0.5 * LOG2_E
CD_Q_SCALE = HEAD_DIM ** -0.5 * LOG2_E
BAND_TQ = 128
BAND_UNROLL = 2


def _params(*sem):
    return pltpu.CompilerParams(dimension_semantics=sem, vmem_limit_bytes=VMEM_LIMIT)


def _rms_f32(x, g):
    ms = jnp.mean(x * x, axis=-1, keepdims=True)
    return x * lax.rsqrt(ms + RMS_EPS) * g


def _rope_chunk(y, c, sa, sb, shift):
    return (y * c + pltpu.roll(y, LANES - shift, 1) * sa + pltpu.roll(y, shift, 1) * sb)


def _softmax_parts(s):
    m = jnp.max(s, axis=-1, keepdims=True)
    e = jnp.exp(s - m)
    l = jnp.sum(e, axis=-1, keepdims=True)
    return m, e, l


def _dot(a, b):
    return jnp.dot(a, b, preferred_element_type=F32)


def _dot_nt(a, b):
    return lax.dot_general(a, b, (((1,), (1,)), ((), ())), preferred_element_type=F32)


def _lane_iota():
    return lax.broadcasted_iota(jnp.int32, (1, LANES), 1)


def _norm_proj_kernel(x_ref, g_ref, w_ref, wvt_ref, c_ref, sa_ref, sb_ref, qk_ref, vt_ref,
                      *, rope, tn):
    xn = _rms_f32(x_ref[...], g_ref[...]).astype(BF16)
    n = qk_ref.shape[1]
    c, sa, sb = c_ref[...], sa_ref[...], sb_ref[...]
    for j in range(n // tn):
        y = _dot(xn, w_ref[:, j * tn:(j + 1) * tn])
        for i in range(tn // LANES):
            col = j * tn + i * LANES
            chunk = y[:, i * LANES:(i + 1) * LANES]
            if rope:
                chunk = _rope_chunk(chunk, c, sa, sb, ROT_DIM // 2)
            if col < n // 2:
                chunk = chunk * CD_Q_SCALE
            qk_ref[:, col:col + LANES] = chunk.astype(BF16)
    vt_ref[...] = _dot_nt(wvt_ref[...], xn).astype(BF16)


def _norm_proj(x, g, w, wvt, tabs, rope, name):
    n = w.shape[1]
    nv = wvt.shape[0]
    tm = PROJ_TM
    nb = SEQ // tm
    tab_spec = pl.BlockSpec((tm, LANES), lambda i: (i % nb, 0))
    return pl.pallas_call(
        functools.partial(_norm_proj_kernel, rope=rope, tn=512),
        grid=(TOKENS // tm,),
        in_specs=[
            pl.BlockSpec((tm, D_MODEL), lambda i: (i, 0)),
            pl.BlockSpec((1, D_MODEL), lambda i: (0, 0)),
            pl.BlockSpec((D_MODEL, n), lambda i: (0, 0)),
            pl.BlockSpec((nv, D_MODEL), lambda i: (0, 0)),
            tab_spec, tab_spec, tab_spec,
        ],
        out_specs=[pl.BlockSpec((tm, n), lambda i: (i, 0)),
                   pl.BlockSpec((nv, tm), lambda i: (0, i))],
        out_shape=[jax.ShapeDtypeStruct((TOKENS, n), BF16),
                   jax.ShapeDtypeStruct((nv, TOKENS), BF16)],
        compiler_params=_params("parallel"),
        name=name,
    )(x, g, w, wvt, *tabs)


def _mixer_residual(x_ref, o_ref, w_ref):
    return x_ref[...] + _dot(o_ref[...], w_ref[...])


def _tail_kernel(*refs, n_front, front, final):
    front_refs = refs[:n_front]
    gm_ref, wu_ref, wd_ref, gp_ref, wg_ref, p_ref, wp_ref, fg_ref, out_ref = refs[n_front:]
    x1 = front(*front_refs)
    xn = _rms_f32(x1, gm_ref[...]).astype(BF16)
    acc = x1
    for j in range(MLP_HIDDEN // MLP_TH):
        h = _dot(xn, wu_ref[:, j * MLP_TH:(j + 1) * MLP_TH])
        h = jnp.square(jnp.maximum(h, 0.0)).astype(BF16)
        acc = acc + _dot(h, wd_ref[j * MLP_TH:(j + 1) * MLP_TH, :])
    gate = jax.nn.sigmoid(_dot(_rms_f32(acc, gp_ref[...]).astype(BF16), wg_ref[...]))
    y = acc + gate * _dot(p_ref[...].astype(BF16), wp_ref[...])
    if final:
        y = _rms_f32(y, fg_ref[...])
    out_ref[...] = y


def _resident(a):
    return pl.BlockSpec(a.shape, lambda i: (0,) * a.ndim, pipeline_mode=pl.Buffered(1))


def _resident_layer(a, layer):
    return pl.BlockSpec((None,) + a.shape[1:], lambda i: (layer,) + (0,) * (a.ndim - 1),
                        pipeline_mode=pl.Buffered(1))


def _tail(front, front_args, front_specs, layer, gm, wu, wd, gp, wg, p, wp, fg, name):
    tm = PROJ_TM
    nt = TOKENS // tm
    shared = (gm, wu, wd, gp, wg)
    return pl.pallas_call(
        functools.partial(_tail_kernel, n_front=len(front_args), front=front,
                          final=layer == DEPTH - 1),
        grid=(nt,),
        in_specs=list(front_specs) + [_resident_layer(a, layer) for a in shared] + [
            pl.BlockSpec((tm, PLE_DIM), lambda i: (layer * nt + i, 0)),
            _resident_layer(wp, layer), _resident(fg)],
        out_specs=pl.BlockSpec((tm, D_MODEL), lambda i: (i, 0)),
        out_shape=jax.ShapeDtypeStruct((TOKENS, D_MODEL), F32),
        compiler_params=_params("parallel"),
        name=name,
    )(*front_args, *shared, p, wp, fg)


def _tail_plain(x, o, w_o, *rest, name):
    tm = PROJ_TM
    specs = [pl.BlockSpec((tm, D_MODEL), lambda i: (i, 0)),
             pl.BlockSpec((tm, o.shape[1]), lambda i: (i, 0)), _resident(w_o)]
    return _tail(_mixer_residual, (x, o, w_o), specs, *rest, name=name)


def _a_qkv_kernel(x_ref, g_ref, w_ref, c_ref, sa_ref, sb_ref, *refs):
    outs, stage = refs[:9], refs[9]
    tm = x_ref.shape[0]
    gl = A_GROUP_LANES
    nc = A_GROUP_CHUNKS
    xn = _rms_f32(x_ref[...], g_ref[...]).astype(BF16)
    c, sa, sb = c_ref[...], sa_ref[...], sb_ref[...]

    def emit(dil, dsts):
        n = tm // dil
        for k, dst in enumerate(dsts):
            for ci in range(nc):
                for r in range(dil):
                    if dil == 1:
                        rows = stage[k * nc + ci]
                    else:
                        rows = stage[k * nc + ci, pl.ds(r, n, stride=dil), :]
                    dst[0, ci, r] = rows.astype(BF16)

    for g, (_, dil) in enumerate(DIL_PAIRS):
        y = _dot(xn, w_ref[:, 2 * gl * g:2 * gl * (g + 1)])
        for i in range(2 * nc):
            stage[i] = _rope_chunk(y[:, i * LANES:(i + 1) * LANES], c, sa, sb, ROT_DIM // 2)
        emit(dil, (outs[3 * g], outs[3 * g + 1]))
    for g, (_, dil) in enumerate(DIL_PAIRS):
        y = _dot(xn, w_ref[:, 6 * gl + gl * g:6 * gl + gl * (g + 1)])
        for i in range(nc):
            stage[i] = y[:, i * LANES:(i + 1) * LANES]
        emit(dil, (outs[3 * g + 2],))


def _a_qkv(x, g, w, tabs):
    tm = PROJ_TM
    nb = SEQ // tm
    gl = A_GROUP_LANES
    tab_spec = pl.BlockSpec((tm, LANES), lambda i: (i % nb, 0))
    out_shapes, out_specs = [], []
    nc = A_GROUP_CHUNKS
    for _, dil in DIL_PAIRS:
        for _ in range(3):
            out_shapes.append(jax.ShapeDtypeStruct((BATCH, nc, dil, SEQ // dil, LANES), BF16))
            out_specs.append(pl.BlockSpec((1, nc, dil, tm // dil, LANES),
                                          lambda i: (i // nb, 0, 0, i % nb, 0)))
    return pl.pallas_call(
        _a_qkv_kernel,
        grid=(TOKENS // tm,),
        in_specs=[
            pl.BlockSpec((tm, D_MODEL), lambda i: (i, 0)),
            pl.BlockSpec((1, D_MODEL), lambda i: (0, 0)),
            pl.BlockSpec((D_MODEL, 9 * gl), lambda i: (0, 0)),
            tab_spec, tab_spec, tab_spec,
        ],
        out_specs=out_specs,
        out_shape=out_shapes,
        scratch_shapes=[pltpu.VMEM((2 * nc, tm, LANES), F32)],
        compiler_params=_params("parallel"),
        name="a_qkv",
    )(x, g, w, *tabs)


def _a_attn_kernel(q_ref, k_ref, v_ref, o_ref, lse_ref, *, seg, dil, token_order):
    tq = BAND_TQ
    kw = tq + 2 * A_BAND_HALF
    low = _lane_iota() < HEAD_DIM
    ones = jnp.ones((kw, LANES), BF16)

    def tile(u):
        p0 = pl.multiple_of(u * tq, tq)
        if seg >= kw:
            seg0 = (p0 // seg) * seg
            ws = jnp.clip(p0 - A_BAND_HALF, seg0, seg0 + seg - kw)
        else:
            ws = (p0 // kw) * kw
        ws = pl.multiple_of(ws, A_BAND_HALF)
        rows = p0 + lax.broadcasted_iota(jnp.int32, (tq, 1), 0)
        cols = ws + lax.broadcasted_iota(jnp.int32, (1, kw), 1)
        valid = jnp.abs(cols - rows) <= A_BAND_HALF
        if seg < kw:
            seg_of = lambda p: lax.shift_right_logical(p, seg.bit_length() - 1)
            valid = jnp.logical_and(valid, seg_of(cols) == seg_of(rows))
        r = p0 // seg
        q0 = p0 - r * seg
        for c in range(A_GROUP_CHUNKS):
            qc = q_ref[0, c, pl.ds(p0, tq), :] * jnp.asarray(HEAD_DIM ** -0.5, BF16)
            kc = k_ref[0, c, pl.ds(ws, kw), :]
            v1 = jnp.concatenate([v_ref[0, c, pl.ds(ws, kw), :], ones], axis=1)
            outs, lses = [], []
            n_heads = 2 if 2 * c + 1 < A_GROUP_HEADS else 1
            for hh in range(n_heads):
                qm = jnp.where(low if hh == 0 else jnp.logical_not(low), qc, jnp.zeros_like(qc))
                s = jnp.where(valid, _dot_nt(qm, kc), NEG_INF)
                m = jnp.max(s, axis=-1, keepdims=True)
                ol = _dot(jnp.exp(s - m).astype(BF16), v1)
                l = ol[:, LANES:]
                outs.append(ol[:, :LANES] * (1.0 / l))
                lses.append(m + jnp.log(l))
            if n_heads == 2:
                o = jnp.where(low, outs[0], outs[1])
                ls = jnp.where(low, lses[0], lses[1])
            else:
                o = jnp.where(low, outs[0], 0.0)
                ls = jnp.where(low, lses[0], 0.0)
            if not token_order or dil == 1:
                dst = pl.ds(p0, tq)
            else:
                dst = pl.ds(r + dil * q0, tq, stride=dil)
            o_ref[0, c, dst, :] = o
            lse_ref[0, c, dst, :] = ls

    def body(i, carry):
        for j in range(BAND_UNROLL):
            tile(i * BAND_UNROLL + j)
        return carry

    lax.fori_loop(0, SEQ // tq // BAND_UNROLL, body, 0)


def _a_attn(q, k, v, dil, token_order):
    seg = SEQ // dil
    nc = A_GROUP_CHUNKS
    q, k, v = (a.reshape(BATCH, nc, SEQ, LANES) for a in (q, k, v))
    in_spec = pl.BlockSpec((1, nc, SEQ, LANES), lambda b: (b, 0, 0, 0))
    out_spec = pl.BlockSpec((1, nc, SEQ, LANES), lambda b: (b, 0, 0, 0))
    out_shape = jax.ShapeDtypeStruct((BATCH, nc, SEQ, LANES), F32)
    return pl.pallas_call(
        functools.partial(_a_attn_kernel, seg=seg, dil=dil, token_order=token_order),
        grid=(BATCH,),
        in_specs=[in_spec, in_spec, in_spec],
        out_specs=[out_spec, out_spec],
        out_shape=[out_shape, out_shape],
        compiler_params=_params("parallel"),
        name=f"a_attn_d{dil}",
    )(q, k, v)


def _a_combine_kernel(o0, o1, o2, l0, l1, l2, out_ref):
    tm = out_ref.shape[0]
    dil = DIL_PAIRS[-1][1]
    seg = SEQ // dil
    pos0 = (pl.program_id(0) % (SEQ // tm)) * (tm // dil)

    def token_rows(ref, c):
        return jnp.concatenate(
            [ref[0, c, pl.ds(pos0 + p, dil, stride=seg), :] for p in range(tm // dil)], axis=0)

    for c in range(A_GROUP_CHUNKS):
        ls = [l0[0, c], l1[0, c], token_rows(l2, c)]
        os_ = [o0[0, c], o1[0, c], token_rows(o2, c)]
        m = jnp.maximum(jnp.maximum(ls[0], ls[1]), ls[2])
        es = [jnp.exp(l - m) for l in ls]
        inv = 1.0 / (es[0] + es[1] + es[2])
        for g in range(len(DIL_PAIRS)):
            col = g * A_GROUP_LANES + c * LANES
            out_ref[:, col:col + LANES] = (os_[g] * (es[g] * inv)).astype(BF16)


def _a_combine(os_, ls_):
    tm = PROJ_TM
    nb = SEQ // tm
    n = len(DIL_PAIRS) * A_GROUP_LANES
    part = pl.BlockSpec((1, A_GROUP_CHUNKS, tm, LANES), lambda i: (i // nb, 0, i % nb, 0))
    whole = pl.BlockSpec((1, A_GROUP_CHUNKS, SEQ, LANES), lambda i: (i // nb, 0, 0, 0))
    return pl.pallas_call(
        _a_combine_kernel,
        grid=(TOKENS // tm,),
        in_specs=[part, part, whole, part, part, whole],
        out_specs=pl.BlockSpec((tm, n), lambda i: (i, 0)),
        out_shape=jax.ShapeDtypeStruct((TOKENS, n), BF16),
        compiler_params=_params("parallel"),
        name="a_combine",
    )(*os_, *ls_)


def _b_proj_kernel(x_ref, g_ref, win_ref, qn_ref, kvn_ref, wuq_ref, wuqr_ref, wuk_ref, wuvt_ref,
                   c_ref, sa_ref, sb_ref, q_out, k_out, vt_out):
    c = c_ref[...]
    s = sb_ref[...] - sa_ref[...]
    slot = B_Q_RANK + B_KV_RANK
    xn = _rms_f32(x_ref[...], g_ref[...]).astype(BF16)
    z = _dot(xn, win_ref[...])
    cq = _rms_f32(z[:, :B_Q_RANK], qn_ref[...]).astype(BF16)
    ckv = _rms_f32(z[:, B_Q_RANK:slot], kvn_ref[...]).astype(BF16)
    k_rope = z[:, slot:slot + LANES] * c + z[:, slot + LANES:] * s
    q = _dot(cq, wuq_ref[...])
    qr = _dot(cq, wuqr_ref[...])
    k = _dot(ckv, wuk_ref[...])
    cq_tab, sq_tab = c * B_Q_SCALE, s * B_Q_SCALE
    for h in range(B_HEADS):
        sl = slice(h * LANES, (h + 1) * LANES)
        q_out[:, sl] = (q[:, sl] * cq_tab + qr[:, sl] * sq_tab).astype(BF16)
        k_out[:, sl] = (k[:, sl] + k_rope).astype(BF16)
    vt_out[...] = _dot_nt(wuvt_ref[...], ckv).astype(BF16)


def _b_proj(x, g, win, qn, kvn, wuq, wuqr, wuk, wuvt, tabs):
    tm = PROJ_TM
    nb = SEQ // tm
    tab_spec = pl.BlockSpec((tm, LANES), lambda i: (i % nb, 0))

    def full(a):
        return pl.BlockSpec(a.shape, lambda i: (0,) * a.ndim)

    nqk = B_HEADS * LANES
    nv = B_HEADS * B_V
    return pl.pallas_call(
        _b_proj_kernel,
        grid=(TOKENS // tm,),
        in_specs=[pl.BlockSpec((tm, D_MODEL), lambda i: (i, 0)), full(g), full(win), full(qn),
                  full(kvn), full(wuq), full(wuqr), full(wuk), full(wuvt),
                  tab_spec, tab_spec, tab_spec],
        out_specs=[pl.BlockSpec((tm, nqk), lambda i: (i, 0)),
                   pl.BlockSpec((tm, nqk), lambda i: (i, 0)),
                   pl.BlockSpec((nv, tm), lambda i: (0, i))],
        out_shape=[jax.ShapeDtypeStruct((TOKENS, nqk), BF16),
                   jax.ShapeDtypeStruct((TOKENS, nqk), BF16),
                   jax.ShapeDtypeStruct((nv, TOKENS), BF16)],
        compiler_params=_params("parallel"),
        name="b_proj",
    )(x, g, win, qn, kvn, wuq, wuqr, wuk, wuvt, *tabs)


def _with_ones_rows(vt):
    return jnp.concatenate([vt, jnp.ones((16, vt.shape[1]), vt.dtype)], axis=0)


def _attend_keys_major(streams, n_chunks, finish, width):
    items = []
    for s0 in range(0, len(streams), width):
        items += [(s, j) for j in range(n_chunks) for s in range(s0, min(s0 + width, len(streams)))]
    state = [None] * len(streams)

    def scores(item):
        s, j = item
        k_chunk, _, qw, _ = streams[s]
        return _dot_nt(k_chunk(j), qw)

    def absorb(item, st):
        s, j = item
        vt1 = _with_ones_rows(streams[s][1](j))
        bias_chunk = streams[s][3]
        if bias_chunk is not None:
            st = st + bias_chunk(j)
        mj = jnp.max(st, axis=0, keepdims=True)
        if j == 0:
            state[s] = (mj, _dot(vt1, jnp.exp2(st - mj).astype(BF16)))
        else:
            m, acc = state[s]
            m_new = jnp.maximum(m, mj)
            e = jnp.exp2(st - m_new).astype(BF16)
            state[s] = (m_new, acc * jnp.exp2(m - m_new) + _dot(vt1, e))
        if j == n_chunks - 1:
            finish(s, state[s][1])
            state[s] = None

    ahead = ATTN_LOOKAHEAD
    pending = [scores(item) for item in items[:ahead]]
    for idx, item in enumerate(items):
        if idx + ahead < len(items):
            pending.append(scores(items[idx + ahead]))
        absorb(item, pending.pop(0))


def _b_attn_kernel(q_ref, k_ref, vt_ref, o_ref):
    kc, tq = ATTN_KC, ATTN_TQ
    outs = {}

    def stream(t, hh):
        sl = slice(hh * LANES, (hh + 1) * LANES)
        rows = slice(hh * B_V, (hh + 1) * B_V)
        return (lambda j: k_ref[j * kc:(j + 1) * kc, sl],
                lambda j: vt_ref[rows, j * kc:(j + 1) * kc],
                q_ref[t * tq:(t + 1) * tq, sl], None)

    def finish(s, acc):
        t, hh = divmod(s, 2)
        outs[hh] = acc[:B_V] * (1.0 / acc[B_V:B_V + 1])
        if hh == 1:
            o = jnp.concatenate([outs[0], outs[1]], axis=0)
            o_ref[t * tq:(t + 1) * tq, :] = o.T.astype(BF16)

    _attend_keys_major([stream(t, hh) for t in range(ATTN_TILES) for hh in range(2)],
                       SEQ // kc, finish, width=2 * ATTN_TILES)


def _b_attn(q, k, vt):
    rows = ATTN_TQ * ATTN_TILES
    nq = SEQ // rows
    return pl.pallas_call(
        _b_attn_kernel,
        grid=(BATCH, B_HEADS // 2, nq),
        in_specs=[
            pl.BlockSpec((rows, 2 * LANES), lambda b, h, i: (b * nq + i, h)),
            pl.BlockSpec((SEQ, 2 * LANES), lambda b, h, i: (b, h)),
            pl.BlockSpec((2 * B_V, SEQ), lambda b, h, i: (h, b)),
        ],
        out_specs=pl.BlockSpec((rows, LANES), lambda b, h, i: (b * nq + i, h)),
        out_shape=jax.ShapeDtypeStruct((TOKENS, B_HEADS * B_V), BF16),
        compiler_params=_params("parallel", "parallel", "arbitrary"),
        name="b_attn",
    )(q, k, vt)


def _c_group_geometry(gi):
    first_row = gi * C_ROW_GROUP
    wrow = min(max(first_row - NA_ROWS // 2, 0), GRID_ROWS - C_WIN_ROWS)
    n_groups = GRID_ROWS // C_ROW_GROUP
    case = 0 if gi == 0 else (2 if gi == n_groups - 1 else 1)
    return first_row, wrow, case


def _c_row_offset_slot(gi, kr, i):
    first_row, wrow, _ = _c_group_geometry(gi)
    qrow, krow = first_row + i, wrow + kr
    rs = min(max(qrow - NA_ROWS // 2, 0), GRID_ROWS - NA_ROWS)
    if rs <= krow < rs + NA_ROWS:
        return krow - qrow + NA_ROWS - 1
    return 2 * NA_ROWS - 1


def _c_attn_kernel(q_ref, k_ref, vt_ref, left_ref, right_ref, o_ref, bias_ref):
    low = _lane_iota() < HEAD_DIM
    nq = C_ROW_GROUP * GRID_W
    nk = C_WIN_ROWS * GRID_W
    n_groups = GRID_ROWS // C_ROW_GROUP

    @pl.when(pl.program_id(1) == 0)
    def _():
        for case, gi in enumerate((0, 1, n_groups - 1)):
            for kr in range(C_WIN_ROWS):
                for hh in range(2):
                    for ip in range(C_ROW_GROUP // 2):
                        a0 = _c_row_offset_slot(gi, kr, 2 * ip)
                        a1 = _c_row_offset_slot(gi, kr, 2 * ip + 1)
                        col = (hh * C_ROW_GROUP // 2 + ip) * LANES
                        bias_ref[case, kr * GRID_W:(kr + 1) * GRID_W, col:col + LANES] = (
                            left_ref[0, hh, a0] + right_ref[0, hh, a1])

    kc = C_KEY_CHUNK

    def stream(gi):
        first_row, wrow, case = _c_group_geometry(gi)
        q0, ws = first_row * GRID_W, wrow * GRID_W
        qc = q_ref[q0:q0 + nq, :]
        zero = jnp.zeros_like(qc)
        qq = jnp.concatenate([jnp.where(low, qc, zero), jnp.where(low, zero, qc)], axis=0)
        return (lambda j: k_ref[ws + j * kc:ws + (j + 1) * kc, :],
                lambda j: vt_ref[:, ws + j * kc:ws + (j + 1) * kc],
                qq,
                lambda j: bias_ref[case, j * kc:(j + 1) * kc, :])

    def finish(gi, acc):
        q0 = _c_group_geometry(gi)[0] * GRID_W
        ot = acc[:LANES] * (1.0 / acc[LANES:LANES + 1])
        o = jnp.concatenate([ot[:HEAD_DIM, :nq], ot[HEAD_DIM:, nq:]], axis=0)
        o_ref[q0:q0 + nq, :] = o.T.astype(BF16)

    _attend_keys_major([stream(gi) for gi in range(n_groups)], nk // kc, finish, width=C_WIDTH)


def _c_attn(qk, vt, left, right):
    nq = C_ROW_GROUP * GRID_W
    nk = C_WIN_ROWS * GRID_W
    npair = C_HEADS // 2
    tab_spec = pl.BlockSpec((1, 2, 2 * NA_ROWS, GRID_W, LANES), lambda h, b: (h, 0, 0, 0, 0))
    return pl.pallas_call(
        _c_attn_kernel,
        grid=(npair, BATCH),
        in_specs=[
            pl.BlockSpec((SEQ, LANES), lambda h, b: (b, h)),
            pl.BlockSpec((SEQ, LANES), lambda h, b: (b, npair + h)),
            pl.BlockSpec((LANES, SEQ), lambda h, b: (h, b)),
            tab_spec, tab_spec,
        ],
        out_specs=pl.BlockSpec((SEQ, LANES), lambda h, b: (b, h)),
        out_shape=jax.ShapeDtypeStruct((TOKENS, C_HEADS * HEAD_DIM), BF16),
        scratch_shapes=[pltpu.VMEM((3, nk, 2 * nq), F32)],
        compiler_params=_params("arbitrary", "arbitrary"),
        name="c_attn",
    )(qk, qk, vt, left, right)


def _c_column_tables(rpb):
    kc, qc = np.arange(GRID_W)[:, None], np.arange(GRID_W)[None, :]
    win0 = np.clip(qc - NA_COLS // 2, 0, GRID_W - NA_COLS)
    col_ok = (kc >= win0) & (kc < win0 + NA_COLS)
    col_off = np.clip(kc - qc + NA_COLS - 1, 0, 2 * NA_COLS - 2)
    col_sel = (col_off[None] == np.arange(2 * NA_COLS - 1)[:, None, None]).astype(np.float32)
    cols = jnp.einsum("hab,bkq->hakq", rpb, col_sel, precision=lax.Precision.HIGHEST)
    cols = jnp.where(col_ok, cols * LOG2_E, NEG_INF)
    cols = jnp.concatenate([cols, jnp.full((C_HEADS, 1, GRID_W, GRID_W), NEG_INF, F32)], axis=1)
    cols = cols.reshape(C_HEADS // 2, 2, 2 * NA_ROWS, GRID_W, GRID_W)
    zeros = jnp.zeros_like(cols)
    return jnp.concatenate([cols, zeros], axis=-1), jnp.concatenate([zeros, cols], axis=-1)


def _d_attn_kernel(lq1, lk1, lq2, lk2, q_ref, k_ref, vt_ref, sub_ref, o_ref, *, lambda_init):
    lam = (jnp.exp(jnp.sum(lq1[...] * lk1[...], axis=-1, keepdims=True))
           - jnp.exp(jnp.sum(lq2[...] * lk2[...], axis=-1, keepdims=True)) + lambda_init)
    low = _lane_iota() < D_HEAD
    kc, tq, dv = ATTN_KC, ATTN_TQ, 2 * D_HEAD
    k_chunk = lambda j: k_ref[j * kc:(j + 1) * kc, :]
    vt_chunk = lambda j: vt_ref[:, j * kc:(j + 1) * kc]
    outs = {}

    def stream(t, half):
        q = q_ref[t * tq:(t + 1) * tq, :]
        zero = jnp.zeros_like(q)
        qw = jnp.where(low, q, zero) if half == 0 else jnp.where(low, zero, q)
        return (k_chunk, vt_chunk, qw, None)

    def finish(s, acc):
        t, half = divmod(s, 2)
        outs[half] = acc[:dv] * (1.0 / acc[dv:dv + 1])
        if half == 1:
            o = (outs[0] - lam * outs[1]).T
            o_ref[t * tq:(t + 1) * tq, :] = (
                _rms_f32(o, sub_ref[...]) * (1.0 - lambda_init)).astype(BF16)

    _attend_keys_major([stream(t, half) for t in range(ATTN_TILES) for half in range(2)],
                       SEQ // kc, finish, width=2 * ATTN_TILES)


def _d_attn(qk, vt, lq1, lk1, lq2, lk2, subln, lambda_init):
    tq = ATTN_TQ * ATTN_TILES
    nq = SEQ // tq
    vec = pl.BlockSpec((1, D_HEAD), lambda b, h, i: (0, 0))
    return pl.pallas_call(
        functools.partial(_d_attn_kernel, lambda_init=lambda_init),
        grid=(BATCH, D_HEADS, nq),
        in_specs=[
            vec, vec, vec, vec,
            pl.BlockSpec((tq, LANES), lambda b, h, i: (b * nq + i, h)),
            pl.BlockSpec((SEQ, LANES), lambda b, h, i: (b, D_HEADS + h)),
            pl.BlockSpec((LANES, SEQ), lambda b, h, i: (h, b)),
            pl.BlockSpec((1, 2 * D_HEAD), lambda b, h, i: (0, 0)),
        ],
        out_specs=pl.BlockSpec((tq, LANES), lambda b, h, i: (b * nq + i, h)),
        out_shape=jax.ShapeDtypeStruct((TOKENS, 2 * D_HEADS * D_HEAD), BF16),
        compiler_params=_params("parallel", "parallel", "arbitrary"),
        name="d_attn",
    )(lq1, lk1, lq2, lk2, qk, qk, vt, subln)


def _rope_tables(rot_dim, period, lane0):
    r = rot_dim // 2
    inv = ROPE_THETA ** (-jnp.arange(0, rot_dim, 2, dtype=F32) / rot_dim)
    ang = jnp.arange(SEQ, dtype=F32)[:, None] * inv[None, :]
    cos, sin = jnp.cos(ang), jnp.sin(ang)
    lane = np.arange(LANES) % period - lane0
    first = (lane >= 0) & (lane < r)
    second = (lane >= r) & (lane < 2 * r)
    idx = np.where(first, lane, np.where(second, lane - r, 0))
    cg, sg = cos[:, idx], sin[:, idx]
    c = jnp.where(first | second, cg, 1.0)
    sa = jnp.where(first, -sg, 0.0)
    sb = jnp.where(second, sg, 0.0)
    return c, sa, sb


def _a_weights(w_qkv, w_o):
    nh = A_GROUP_HEADS * HEAD_DIM
    pad = jnp.zeros((D_MODEL, A_GROUP_LANES - nh), w_qkv.dtype)

    def cols(part, g):
        base = part * A_HEADS * HEAD_DIM + g * nh
        return [w_qkv[:, base:base + nh], pad]

    pieces = []
    for g in range(len(DIL_PAIRS)):
        pieces += cols(0, g) + cols(1, g)
    for g in range(len(DIL_PAIRS)):
        pieces += cols(2, g)
    w = jnp.concatenate(pieces, axis=1).astype(BF16)
    wo = w_o.reshape(len(DIL_PAIRS), nh, D_MODEL)
    wo = jnp.pad(wo, ((0, 0), (0, A_GROUP_LANES - nh), (0, 0))).astype(BF16)
    return w, wo.reshape(len(DIL_PAIRS) * A_GROUP_LANES, D_MODEL)


def _rotate_half_columns(w_rope):
    r = B_ROPE // 2
    return jnp.concatenate([-w_rope[..., r:], w_rope[..., :r]], axis=-1)


def _b_weights(w_in, w_uq, w_ukv):
    split = B_Q_RANK + B_KV_RANK
    pad_lo = jnp.zeros((D_MODEL, B_NOPE), w_in.dtype)
    pad_hi = jnp.zeros((D_MODEL, LANES - B_NOPE - B_ROPE), w_in.dtype)
    win = jnp.concatenate([
        w_in[:, :split], pad_lo, w_in[:, split:], pad_hi,
        pad_lo, _rotate_half_columns(w_in[:, split:]), pad_hi], axis=1).astype(BF16)
    wuq = w_uq.reshape(B_Q_RANK, B_HEADS, B_NOPE + B_ROPE)
    slot_pad = ((0, 0), (0, 0), (0, LANES - B_NOPE - B_ROPE))
    wuqr = jnp.concatenate([jnp.zeros_like(wuq[:, :, :B_NOPE]),
                            _rotate_half_columns(wuq[:, :, B_NOPE:])], axis=-1)
    wuqr = jnp.pad(wuqr, slot_pad).reshape(B_Q_RANK, B_HEADS * LANES).astype(BF16)
    wuq = jnp.pad(wuq, slot_pad)
    wuq = wuq.reshape(B_Q_RANK, B_HEADS * LANES).astype(BF16)
    wukv = w_ukv.reshape(B_KV_RANK, B_HEADS, B_NOPE + B_V)
    wuk = jnp.pad(wukv[:, :, :B_NOPE], ((0, 0), (0, 0), (0, LANES - B_NOPE)))
    wuk = wuk.reshape(B_KV_RANK, B_HEADS * LANES).astype(BF16)
    wuvt = wukv[:, :, B_NOPE:].reshape(B_KV_RANK, B_HEADS * B_V).T.astype(BF16)
    return win, wuq, wuqr, wuk, wuvt


def _split_qk_vt(w_qkv, n_qk):
    return w_qkv[:, :n_qk].astype(BF16), w_qkv[:, n_qk:].T.astype(BF16)


def kernel(x, p, a_norm, a_w_qkv, a_w_o, b_norm, b_w_in, b_q_norm, b_w_uq, b_kv_norm, b_w_ukv, b_w_o, c_norm, c_w_qkv, c_rpb, c_w_o, d_norm, d_w_qkv, d_lambda_q1, d_lambda_k1, d_lambda_q2, d_lambda_k2, d_subln, d_w_o, mlp_norm, w_up, w_down, ple_norm, w_ple_gate, w_ple_proj, final_norm):
    tabs_p = _rope_tables(ROT_DIM, HEAD_DIM, 0)
    tabs_l = _rope_tables(B_ROPE, LANES, B_NOPE)
    h = x.reshape(TOKENS, D_MODEL)
    row = lambda v: v.reshape(1, -1)

    stacks = (mlp_norm.reshape(DEPTH, 1, D_MODEL), w_up.astype(BF16), w_down.astype(BF16),
              ple_norm.reshape(DEPTH, 1, D_MODEL), w_ple_gate.astype(BF16),
              p.reshape(DEPTH * TOKENS, PLE_DIM), w_ple_proj.astype(BF16), row(final_norm))
    for i in range(DEPTH):
        rest = (i,) + stacks
        if i == 0:
            w, wo = _a_weights(a_w_qkv[0], a_w_o[0])
            parts = _a_qkv(h, row(a_norm[0]), w, tabs_p)
            os_, ls_ = [], []
            for g, (_, dil) in enumerate(DIL_PAIRS):
                o, lse = _a_attn(parts[3 * g], parts[3 * g + 1], parts[3 * g + 2], dil,
                                 token_order=g < len(DIL_PAIRS) - 1)
                os_.append(o)
                ls_.append(lse)
            h = _tail_plain(h, _a_combine(os_, ls_), wo, *rest, name="a_tail")
        elif i == 1:
            b_w = _b_weights(b_w_in[0], b_w_uq[0], b_w_ukv[0])
            q, k, vt = _b_proj(h, row(b_norm[0]), b_w[0], row(b_q_norm[0]), row(b_kv_norm[0]),
                               *b_w[1:], tabs_l)
            h = _tail_plain(h, _b_attn(q, k, vt), b_w_o[0].astype(BF16), *rest, name="b_tail")
        elif i == 2:
            w, wvt = _split_qk_vt(c_w_qkv[0], 2 * C_HEADS * HEAD_DIM)
            qk, vt = _norm_proj(h, row(c_norm[0]), w, wvt, tabs_p, False, "c_qkv")
            o = _c_attn(qk, vt, *_c_column_tables(c_rpb[0]))
            h = _tail_plain(h, o, c_w_o[0].astype(BF16), *rest, name="c_tail")
        else:
            lambda_init = 0.8 - 0.6 * math.exp(-0.3 * i)
            w, wvt = _split_qk_vt(d_w_qkv[0], 2 * 2 * D_HEADS * D_HEAD)
            qk, vt = _norm_proj(h, row(d_norm[0]), w, wvt, tabs_p, True, "d_qkv")
            o = _d_attn(qk, vt, row(d_lambda_q1[0]), row(d_lambda_k1[0]), row(d_lambda_q2[0]),
                        row(d_lambda_k2[0]), row(d_subln[0]), lambda_init)
            h = _tail_plain(h, o, d_w_o[0].astype(BF16), *rest, name="d_tail")
    return h.reshape(BATCH, SEQ, D_MODEL)
```

```python
import functools
import math

import numpy as np
import jax
import jax.numpy as jnp
from jax import lax
from jax.experimental import pallas as pl
from jax.experimental.pallas import tpu as pltpu

F32 = jnp.float32
BF16 = jnp.bfloat16

D_MODEL = 1024
BATCH = 8
SEQ = 2048
DEPTH = 4
TOKENS = BATCH * SEQ
HEAD_DIM = 64
ROPE_THETA = 500000.0
ROT_DIM = HEAD_DIM // 4
NEG_INF = -1e30
RMS_EPS = 1e-6
LOG2_E = math.log2(math.e)

DIL_PAIRS = ((128, 1), (512, 4), (2048, 16))
A_GROUP_HEADS = 5
A_HEADS = A_GROUP_HEADS * len(DIL_PAIRS)
A_BAND_HALF = 64
A_GROUP_LANES = 384
A_GROUP_CHUNKS = A_GROUP_LANES // 128

B_HEADS = 16
B_Q_RANK = 256
B_KV_RANK = 128
B_NOPE = 64
B_ROPE = 32
B_V = 64

C_HEADS = 16
GRID_W = 64
GRID_ROWS = SEQ // GRID_W
NA_ROWS = 8
NA_COLS = 16
C_ROW_GROUP = 4
C_WIN_ROWS = 12
C_KEY_CHUNK = 768
C_WIDTH = 4

D_HEADS = 8
D_HEAD = 64

MLP_HIDDEN = 4 * D_MODEL
PLE_DIM = 256

LANES = 128
VMEM_LIMIT = 48 * 1024 * 1024

PROJ_TM = 512
MLP_TH = 1024
ATTN_TQ = 512
ATTN_KC = 512
ATTN_TILES = 4
ATTN_LOOKAHEAD = 5

B_Q_SCALE = (B_NOPE + B_ROPE) ** -0.5 * LOG2_E
CD_Q_SCALE = HEAD_DIM ** -0.5 * LOG2_E
BAND_TQ = 128
A_QKV_TN = 1024
BAND_UNROLL = 8


def _params(*sem):
    return pltpu.CompilerParams(dimension_semantics=sem, vmem_limit_bytes=VMEM_LIMIT)


def _rms_f32(x, g):
    ms = jnp.mean(x * x, axis=-1, keepdims=True)
    return x * lax.rsqrt(ms + RMS_EPS) * g


def _rope_chunk(y, c, sa, sb, shift):
    return (y * c + pltpu.roll(y, LANES - shift, 1) * sa + pltpu.roll(y, shift, 1) * sb)


def _softmax_parts(s):
    m = jnp.max(s, axis=-1, keepdims=True)
    e = jnp.exp(s - m)
    l = jnp.sum(e, axis=-1, keepdims=True)
    return m, e, l


def _dot(a, b):
    return jnp.dot(a, b, preferred_element_type=F32)


def _dot_nt(a, b):
    return lax.dot_general(a, b, (((1,), (1,)), ((), ())), preferred_element_type=F32)


def _lane_iota():
    return lax.broadcasted_iota(jnp.int32, (1, LANES), 1)


def _norm_proj_kernel(x_ref, g_ref, w_ref, wvt_ref, c_ref, sa_ref, sb_ref, qk_ref, vt_ref,
                      *, rope, tn):
    xn = _rms_f32(x_ref[...], g_ref[...]).astype(BF16)
    n = qk_ref.shape[1]
    c, sa, sb = c_ref[...], sa_ref[...], sb_ref[...]
    for j in range(n // tn):
        y = _dot(xn, w_ref[:, j * tn:(j + 1) * tn])
        for i in range(tn // LANES):
            col = j * tn + i * LANES
            chunk = y[:, i * LANES:(i + 1) * LANES]
            if rope:
                chunk = _rope_chunk(chunk, c, sa, sb, ROT_DIM // 2)
            if col < n // 2:
                chunk = chunk * CD_Q_SCALE
            qk_ref[:, col:col + LANES] = chunk.astype(BF16)
    vt_ref[...] = _dot_nt(wvt_ref[...], xn).astype(BF16)


def _norm_proj(x, g, w, wvt, tabs, rope, name):
    n = w.shape[1]
    nv = wvt.shape[0]
    tm = PROJ_TM
    nb = SEQ // tm
    tab_spec = pl.BlockSpec((tm, LANES), lambda i: (i % nb, 0))
    return pl.pallas_call(
        functools.partial(_norm_proj_kernel, rope=rope, tn=512),
        grid=(TOKENS // tm,),
        in_specs=[
            pl.BlockSpec((tm, D_MODEL), lambda i: (i, 0)),
            pl.BlockSpec((1, D_MODEL), lambda i: (0, 0)),
            pl.BlockSpec((D_MODEL, n), lambda i: (0, 0)),
            pl.BlockSpec((nv, D_MODEL), lambda i: (0, 0)),
            tab_spec, tab_spec, tab_spec,
        ],
        out_specs=[pl.BlockSpec((tm, n), lambda i: (i, 0)),
                   pl.BlockSpec((nv, tm), lambda i: (0, i))],
        out_shape=[jax.ShapeDtypeStruct((TOKENS, n), BF16),
                   jax.ShapeDtypeStruct((nv, TOKENS), BF16)],
        compiler_params=_params("parallel"),
        name=name,
    )(x, g, w, wvt, *tabs)


def _mixer_residual(x_ref, o_ref, w_ref):
    return x_ref[...] + _dot(o_ref[...], w_ref[...])


def _tail_kernel(*refs, n_front, front, final):
    front_refs = refs[:n_front]
    gm_ref, wu_ref, wd_ref, gp_ref, wg_ref, p_ref, wp_ref, fg_ref, out_ref = refs[n_front:]
    x1 = front(*front_refs)
    xn = _rms_f32(x1, gm_ref[...]).astype(BF16)
    acc = x1
    for j in range(MLP_HIDDEN // MLP_TH):
        h = _dot(xn, wu_ref[:, j * MLP_TH:(j + 1) * MLP_TH])
        h = jnp.square(jnp.maximum(h, 0.0)).astype(BF16)
        acc = acc + _dot(h, wd_ref[j * MLP_TH:(j + 1) * MLP_TH, :])
    gate = jax.nn.sigmoid(_dot(_rms_f32(acc, gp_ref[...]).astype(BF16), wg_ref[...]))
    y = acc + gate * _dot(p_ref[...].astype(BF16), wp_ref[...])
    if final:
        y = _rms_f32(y, fg_ref[...])
    out_ref[...] = y


def _resident(a):
    return pl.BlockSpec(a.shape, lambda i: (0,) * a.ndim, pipeline_mode=pl.Buffered(1))


def _resident_layer(a, layer):
    return pl.BlockSpec((None,) + a.shape[1:], lambda i: (layer,) + (0,) * (a.ndim - 1),
                        pipeline_mode=pl.Buffered(1))


def _tail(front, front_args, front_specs, layer, gm, wu, wd, gp, wg, p, wp, fg, name):
    tm = PROJ_TM
    nt = TOKENS // tm
    shared = (gm, wu, wd, gp, wg)
    return pl.pallas_call(
        functools.partial(_tail_kernel, n_front=len(front_args), front=front,
                          final=layer == DEPTH - 1),
        grid=(nt,),
        in_specs=list(front_specs) + [_resident_layer(a, layer) for a in shared] + [
            pl.BlockSpec((tm, PLE_DIM), lambda i: (layer * nt + i, 0)),
            _resident_layer(wp, layer), _resident(fg)],
        out_specs=pl.BlockSpec((tm, D_MODEL), lambda i: (i, 0)),
        out_shape=jax.ShapeDtypeStruct((TOKENS, D_MODEL), F32),
        compiler_params=_params("parallel"),
        name=name,
    )(*front_args, *shared, p, wp, fg)


def _tail_plain(x, o, w_o, *rest, name):
    tm = PROJ_TM
    specs = [pl.BlockSpec((tm, D_MODEL), lambda i: (i, 0)),
             pl.BlockSpec((tm, o.shape[1]), lambda i: (i, 0)), _resident(w_o)]
    return _tail(_mixer_residual, (x, o, w_o), specs, *rest, name=name)


def _a_qkv_kernel(x_ref, g_ref, w_ref, c_ref, sa_ref, sb_ref, *refs):
    outs, stage = refs[:9], refs[9]
    tm = x_ref.shape[0]
    nc = A_GROUP_CHUNKS
    n_groups = len(DIL_PAIRS)
    xn = _rms_f32(x_ref[...], g_ref[...]).astype(BF16)
    c, sa, sb = c_ref[...], sa_ref[...], sb_ref[...]
    n_chunks = 3 * n_groups * nc
    per_dot = A_QKV_TN // LANES
    for first in range(0, n_chunks, per_dot):
        last = min(first + per_dot, n_chunks)
        y = _dot(xn, w_ref[:, first * LANES:last * LANES])
        for idx in range(first, last):
            chunk = y[:, (idx - first) * LANES:(idx - first + 1) * LANES]
            if idx < 2 * n_groups * nc:
                g, part, ci = idx // (2 * nc), (idx // nc) % 2, idx % nc
                chunk = _rope_chunk(chunk, c, sa, sb, ROT_DIM // 2)
            else:
                g, part, ci = (idx - 2 * n_groups * nc) // nc, 2, idx % nc
            dil = DIL_PAIRS[g][1]
            dst = outs[3 * g + part]
            if dil == 1:
                dst[0, ci, 0] = chunk.astype(BF16)
            else:
                slot = idx % 2
                stage[slot] = chunk
                for r in range(dil):
                    dst[0, ci, r] = stage[slot, pl.ds(r, tm // dil, stride=dil), :].astype(BF16)


def _a_qkv(x, g, w, tabs):
    tm = PROJ_TM
    nb = SEQ // tm
    gl = A_GROUP_LANES
    tab_spec = pl.BlockSpec((tm, LANES), lambda i: (i % nb, 0))
    out_shapes, out_specs = [], []
    nc = A_GROUP_CHUNKS
    for _, dil in DIL_PAIRS:
        for _ in range(3):
            out_shapes.append(jax.ShapeDtypeStruct((BATCH, nc, dil, SEQ // dil, LANES), BF16))
            out_specs.append(pl.BlockSpec((1, nc, dil, tm // dil, LANES),
                                          lambda i: (i // nb, 0, 0, i % nb, 0)))
    return pl.pallas_call(
        _a_qkv_kernel,
        grid=(TOKENS // tm,),
        in_specs=[
            pl.BlockSpec((tm, D_MODEL), lambda i: (i, 0)),
            pl.BlockSpec((1, D_MODEL), lambda i: (0, 0)),
            pl.BlockSpec((D_MODEL, 9 * gl), lambda i: (0, 0)),
            tab_spec, tab_spec, tab_spec,
        ],
        out_specs=out_specs,
        out_shape=out_shapes,
        scratch_shapes=[pltpu.VMEM((2, tm, LANES), F32)],
        compiler_params=_params("parallel"),
        name="a_qkv",
    )(x, g, w, *tabs)


def _a_attn_kernel(q_ref, k_ref, v_ref, o_ref, lse_ref, *, seg, dil, token_order):
    tq = BAND_TQ
    kw = tq + 2 * A_BAND_HALF
    low = _lane_iota() < HEAD_DIM
    ones = jnp.ones((kw, LANES), BF16)

    def tile(u):
        p0 = pl.multiple_of(u * tq, tq)
        if seg >= kw:
            seg0 = (p0 // seg) * seg
            ws = jnp.clip(p0 - A_BAND_HALF, seg0, seg0 + seg - kw)
        else:
            ws = (p0 // kw) * kw
        ws = pl.multiple_of(ws, A_BAND_HALF)
        rows = p0 + lax.broadcasted_iota(jnp.int32, (tq, 1), 0)
        cols = ws + lax.broadcasted_iota(jnp.int32, (1, kw), 1)
        valid = jnp.abs(cols - rows) <= A_BAND_HALF
        if seg < kw:
            seg_of = lambda p: lax.shift_right_logical(p, seg.bit_length() - 1)
            valid = jnp.logical_and(valid, seg_of(cols) == seg_of(rows))
        r = p0 // seg
        q0 = p0 - r * seg
        for c in range(A_GROUP_CHUNKS):
            qc = q_ref[0, c, pl.ds(p0, tq), :] * jnp.asarray(HEAD_DIM ** -0.5, BF16)
            kc = k_ref[0, c, pl.ds(ws, kw), :]
            v1 = jnp.concatenate([v_ref[0, c, pl.ds(ws, kw), :], ones], axis=1)
            outs, lses = [], []
            n_heads = 2 if 2 * c + 1 < A_GROUP_HEADS else 1
            for hh in range(n_heads):
                qm = jnp.where(low if hh == 0 else jnp.logical_not(low), qc, jnp.zeros_like(qc))
                s = jnp.where(valid, _dot_nt(qm, kc), NEG_INF)
                m = jnp.max(s, axis=-1, keepdims=True)
                ol = _dot(jnp.exp(s - m).astype(BF16), v1)
                l = ol[:, LANES:]
                outs.append(ol[:, :LANES] * (1.0 / l))
                lses.append(m + jnp.log(l))
            if n_heads == 2:
                o = jnp.where(low, outs[0], outs[1])
                ls = jnp.where(low, lses[0], lses[1])
            else:
                o = jnp.where(low, outs[0], 0.0)
                ls = jnp.where(low, lses[0], 0.0)
            if not token_order or dil == 1:
                dst = pl.ds(p0, tq)
            else:
                dst = pl.ds(r + dil * q0, tq, stride=dil)
            o_ref[0, c, dst, :] = o
            lse_ref[0, c, dst, :] = ls

    def body(i, carry):
        for j in range(BAND_UNROLL):
            tile(i * BAND_UNROLL + j)
        return carry

    lax.fori_loop(0, SEQ // tq // BAND_UNROLL, body, 0)


def _a_attn(q, k, v, dil, token_order):
    seg = SEQ // dil
    nc = A_GROUP_CHUNKS
    q, k, v = (a.reshape(BATCH, nc, SEQ, LANES) for a in (q, k, v))
    in_spec = pl.BlockSpec((1, nc, SEQ, LANES), lambda b: (b, 0, 0, 0))
    out_spec = pl.BlockSpec((1, nc, SEQ, LANES), lambda b: (b, 0, 0, 0))
    out_shape = jax.ShapeDtypeStruct((BATCH, nc, SEQ, LANES), F32)
    return pl.pallas_call(
        functools.partial(_a_attn_kernel, seg=seg, dil=dil, token_order=token_order),
        grid=(BATCH,),
        in_specs=[in_spec, in_spec, in_spec],
        out_specs=[out_spec, out_spec],
        out_shape=[out_shape, out_shape],
        compiler_params=_params("parallel"),
        name=f"a_attn_d{dil}",
    )(q, k, v)


def _a_combine_kernel(o0, o1, o2, l0, l1, l2, out_ref):
    tm = out_ref.shape[0]
    dil = DIL_PAIRS[-1][1]
    seg = SEQ // dil
    pos0 = (pl.program_id(0) % (SEQ // tm)) * (tm // dil)

    def token_rows(ref, c):
        return jnp.concatenate(
            [ref[0, c, pl.ds(pos0 + p, dil, stride=seg), :] for p in range(tm // dil)], axis=0)

    for c in range(A_GROUP_CHUNKS):
        ls = [l0[0, c], l1[0, c], token_rows(l2, c)]
        os_ = [o0[0, c], o1[0, c], token_rows(o2, c)]
        m = jnp.maximum(jnp.maximum(ls[0], ls[1]), ls[2])
        es = [jnp.exp(l - m) for l in ls]
        inv = 1.0 / (es[0] + es[1] + es[2])
        for g in range(len(DIL_PAIRS)):
            col = g * A_GROUP_LANES + c * LANES
            out_ref[:, col:col + LANES] = (os_[g] * (es[g] * inv)).astype(BF16)


def _a_combine(os_, ls_):
    tm = PROJ_TM
    nb = SEQ // tm
    n = len(DIL_PAIRS) * A_GROUP_LANES
    part = pl.BlockSpec((1, A_GROUP_CHUNKS, tm, LANES), lambda i: (i // nb, 0, i % nb, 0))
    whole = pl.BlockSpec((1, A_GROUP_CHUNKS, SEQ, LANES), lambda i: (i // nb, 0, 0, 0))
    return pl.pallas_call(
        _a_combine_kernel,
        grid=(TOKENS // tm,),
        in_specs=[part, part, whole, part, part, whole],
        out_specs=pl.BlockSpec((tm, n), lambda i: (i, 0)),
        out_shape=jax.ShapeDtypeStruct((TOKENS, n), BF16),
        compiler_params=_params("parallel"),
        name="a_combine",
    )(*os_, *ls_)


def _b_proj_kernel(x_ref, g_ref, win_ref, qn_ref, kvn_ref, wuq_ref, wuqr_ref, wuk_ref, wuvt_ref,
                   c_ref, sa_ref, sb_ref, q_out, k_out, vt_out):
    c = c_ref[...]
    s = sb_ref[...] - sa_ref[...]
    slot = B_Q_RANK + B_KV_RANK
    xn = _rms_f32(x_ref[...], g_ref[...]).astype(BF16)
    z = _dot(xn, win_ref[...])
    cq = _rms_f32(z[:, :B_Q_RANK], qn_ref[...]).astype(BF16)
    ckv = _rms_f32(z[:, B_Q_RANK:slot], kvn_ref[...]).astype(BF16)
    k_rope = z[:, slot:slot + LANES] * c + z[:, slot + LANES:] * s
    q = _dot(cq, wuq_ref[...])
    qr = _dot(cq, wuqr_ref[...])
    k = _dot(ckv, wuk_ref[...])
    cq_tab, sq_tab = c * B_Q_SCALE, s * B_Q_SCALE
    for h in range(B_HEADS):
        sl = slice(h * LANES, (h + 1) * LANES)
        q_out[:, sl] = (q[:, sl] * cq_tab + qr[:, sl] * sq_tab).astype(BF16)
        k_out[:, sl] = (k[:, sl] + k_rope).astype(BF16)
    vt_out[...] = _dot_nt(wuvt_ref[...], ckv).astype(BF16)


def _b_proj(x, g, win, qn, kvn, wuq, wuqr, wuk, wuvt, tabs):
    tm = PROJ_TM
    nb = SEQ // tm
    tab_spec = pl.BlockSpec((tm, LANES), lambda i: (i % nb, 0))

    def full(a):
        return pl.BlockSpec(a.shape, lambda i: (0,) * a.ndim)

    nqk = B_HEADS * LANES
    nv = B_HEADS * B_V
    return pl.pallas_call(
        _b_proj_kernel,
        grid=(TOKENS // tm,),
        in_specs=[pl.BlockSpec((tm, D_MODEL), lambda i: (i, 0)), full(g), full(win), full(qn),
                  full(kvn), full(wuq), full(wuqr), full(wuk), full(wuvt),
                  tab_spec, tab_spec, tab_spec],
        out_specs=[pl.BlockSpec((tm, nqk), lambda i: (i, 0)),
                   pl.BlockSpec((tm, nqk), lambda i: (i, 0)),
                   pl.BlockSpec((nv, tm), lambda i: (0, i))],
        out_shape=[jax.ShapeDtypeStruct((TOKENS, nqk), BF16),
                   jax.ShapeDtypeStruct((TOKENS, nqk), BF16),
                   jax.ShapeDtypeStruct((nv, TOKENS), BF16)],
        compiler_params=_params("parallel"),
        name="b_proj",
    )(x, g, win, qn, kvn, wuq, wuqr, wuk, wuvt, *tabs)


def _with_ones_rows(vt):
    return jnp.concatenate([vt, jnp.ones((16, vt.shape[1]), vt.dtype)], axis=0)


def _attend_keys_major(streams, n_chunks, finish, width):
    items = []
    for s0 in range(0, len(streams), width):
        items += [(s, j) for j in range(n_chunks) for s in range(s0, min(s0 + width, len(streams)))]
    state = [None] * len(streams)

    def scores(item):
        s, j = item
        k_chunk, _, qw, _ = streams[s]
        return _dot_nt(k_chunk(j), qw)

    def absorb(item, st):
        s, j = item
        vt1 = _with_ones_rows(streams[s][1](j))
        bias_chunk = streams[s][3]
        if bias_chunk is not None:
            st = st + bias_chunk(j)
        mj = jnp.max(st, axis=0, keepdims=True)
        if j == 0:
            state[s] = (mj, _dot(vt1, jnp.exp2(st - mj).astype(BF16)))
        else:
            m, acc = state[s]
            m_new = jnp.maximum(m, mj)
            e = jnp.exp2(st - m_new).astype(BF16)
            state[s] = (m_new, acc * jnp.exp2(m - m_new) + _dot(vt1, e))
        if j == n_chunks - 1:
            finish(s, state[s][1])
            state[s] = None

    ahead = ATTN_LOOKAHEAD
    pending = [scores(item) for item in items[:ahead]]
    for idx, item in enumerate(items):
        if idx + ahead < len(items):
            pending.append(scores(items[idx + ahead]))
        absorb(item, pending.pop(0))


def _b_attn_kernel(q_ref, k_ref, vt_ref, o_ref):
    kc, tq = ATTN_KC, ATTN_TQ
    outs = {}

    def stream(t, hh):
        sl = slice(hh * LANES, (hh + 1) * LANES)
        rows = slice(hh * B_V, (hh + 1) * B_V)
        return (lambda j: k_ref[j * kc:(j + 1) * kc, sl],
                lambda j: vt_ref[rows, j * kc:(j + 1) * kc],
                q_ref[t * tq:(t + 1) * tq, sl], None)

    def finish(s, acc):
        t, hh = divmod(s, 2)
        outs[hh] = acc[:B_V] * (1.0 / acc[B_V:B_V + 1])
        if hh == 1:
            o = jnp.concatenate([outs[0], outs[1]], axis=0)
            o_ref[t * tq:(t + 1) * tq, :] = o.T.astype(BF16)

    _attend_keys_major([stream(t, hh) for t in range(ATTN_TILES) for hh in range(2)],
                       SEQ // kc, finish, width=2 * ATTN_TILES)


def _b_attn(q, k, vt):
    rows = ATTN_TQ * ATTN_TILES
    nq = SEQ // rows
    return pl.pallas_call(
        _b_attn_kernel,
        grid=(BATCH, B_HEADS // 2, nq),
        in_specs=[
            pl.BlockSpec((rows, 2 * LANES), lambda b, h, i: (b * nq + i, h)),
            pl.BlockSpec((SEQ, 2 * LANES), lambda b, h, i: (b, h)),
            pl.BlockSpec((2 * B_V, SEQ), lambda b, h, i: (h, b)),
        ],
        out_specs=pl.BlockSpec((rows, LANES), lambda b, h, i: (b * nq + i, h)),
        out_shape=jax.ShapeDtypeStruct((TOKENS, B_HEADS * B_V), BF16),
        compiler_params=_params("parallel", "parallel", "arbitrary"),
        name="b_attn",
    )(q, k, vt)


def _c_group_geometry(gi):
    first_row = gi * C_ROW_GROUP
    wrow = min(max(first_row - NA_ROWS // 2, 0), GRID_ROWS - C_WIN_ROWS)
    n_groups = GRID_ROWS // C_ROW_GROUP
    case = 0 if gi == 0 else (2 if gi == n_groups - 1 else 1)
    return first_row, wrow, case


def _c_row_offset_slot(gi, kr, i):
    first_row, wrow, _ = _c_group_geometry(gi)
    qrow, krow = first_row + i, wrow + kr
    rs = min(max(qrow - NA_ROWS // 2, 0), GRID_ROWS - NA_ROWS)
    if rs <= krow < rs + NA_ROWS:
        return krow - qrow + NA_ROWS - 1
    return 2 * NA_ROWS - 1


def _c_attn_kernel(q_ref, k_ref, vt_ref, left_ref, right_ref, o_ref, bias_ref):
    low = _lane_iota() < HEAD_DIM
    nq = C_ROW_GROUP * GRID_W
    nk = C_WIN_ROWS * GRID_W
    n_groups = GRID_ROWS // C_ROW_GROUP

    @pl.when(pl.program_id(1) == 0)
    def _():
        for case, gi in enumerate((0, 1, n_groups - 1)):
            for kr in range(C_WIN_ROWS):
                for hh in range(2):
                    for ip in range(C_ROW_GROUP // 2):
                        a0 = _c_row_offset_slot(gi, kr, 2 * ip)
                        a1 = _c_row_offset_slot(gi, kr, 2 * ip + 1)
                        col = (hh * C_ROW_GROUP // 2 + ip) * LANES
                        bias_ref[case, kr * GRID_W:(kr + 1) * GRID_W, col:col + LANES] = (
                            left_ref[0, hh, a0] + right_ref[0, hh, a1])

    kc = C_KEY_CHUNK

    def stream(gi):
        first_row, wrow, case = _c_group_geometry(gi)
        q0, ws = first_row * GRID_W, wrow * GRID_W
        qc = q_ref[q0:q0 + nq, :]
        zero = jnp.zeros_like(qc)
        qq = jnp.concatenate([jnp.where(low, qc, zero), jnp.where(low, zero, qc)], axis=0)
        return (lambda j: k_ref[ws + j * kc:ws + (j + 1) * kc, :],
                lambda j: vt_ref[:, ws + j * kc:ws + (j + 1) * kc],
                qq,
                lambda j: bias_ref[case, j * kc:(j + 1) * kc, :])

    def finish(gi, acc):
        q0 = _c_group_geometry(gi)[0] * GRID_W
        ot = acc[:LANES] * (1.0 / acc[LANES:LANES + 1])
        o = jnp.concatenate([ot[:HEAD_DIM, :nq], ot[HEAD_DIM:, nq:]], axis=0)
        o_ref[q0:q0 + nq, :] = o.T.astype(BF16)

    _attend_keys_major([stream(gi) for gi in range(n_groups)], nk // kc, finish, width=C_WIDTH)


def _c_attn(qk, vt, left, right):
    nq = C_ROW_GROUP * GRID_W
    nk = C_WIN_ROWS * GRID_W
    npair = C_HEADS // 2
    tab_spec = pl.BlockSpec((1, 2, 2 * NA_ROWS, GRID_W, LANES), lambda h, b: (h, 0, 0, 0, 0))
    return pl.pallas_call(
        _c_attn_kernel,
        grid=(npair, BATCH),
        in_specs=[
            pl.BlockSpec((SEQ, LANES), lambda h, b: (b, h)),
            pl.BlockSpec((SEQ, LANES), lambda h, b: (b, npair + h)),
            pl.BlockSpec((LANES, SEQ), lambda h, b: (h, b)),
            tab_spec, tab_spec,
        ],
        out_specs=pl.BlockSpec((SEQ, LANES), lambda h, b: (b, h)),
        out_shape=jax.ShapeDtypeStruct((TOKENS, C_HEADS * HEAD_DIM), BF16),
        scratch_shapes=[pltpu.VMEM((3, nk, 2 * nq), F32)],
        compiler_params=_params("arbitrary", "arbitrary"),
        name="c_attn",
    )(qk, qk, vt, left, right)


def _c_column_tables(rpb):
    kc, qc = np.arange(GRID_W)[:, None], np.arange(GRID_W)[None, :]
    win0 = np.clip(qc - NA_COLS // 2, 0, GRID_W - NA_COLS)
    col_ok = (kc >= win0) & (kc < win0 + NA_COLS)
    col_off = np.clip(kc - qc + NA_COLS - 1, 0, 2 * NA_COLS - 2)
    col_sel = (col_off[None] == np.arange(2 * NA_COLS - 1)[:, None, None]).astype(np.float32)
    cols = jnp.einsum("hab,bkq->hakq", rpb, col_sel, precision=lax.Precision.HIGHEST)
    cols = jnp.where(col_ok, cols * LOG2_E, NEG_INF)
    cols = jnp.concatenate([cols, jnp.full((C_HEADS, 1, GRID_W, GRID_W), NEG_INF, F32)], axis=1)
    cols = cols.reshape(C_HEADS // 2, 2, 2 * NA_ROWS, GRID_W, GRID_W)
    zeros = jnp.zeros_like(cols)
    return jnp.concatenate([cols, zeros], axis=-1), jnp.concatenate([zeros, cols], axis=-1)


def _d_attn_kernel(lq1, lk1, lq2, lk2, q_ref, k_ref, vt_ref, sub_ref, o_ref, *, lambda_init):
    lam = (jnp.exp(jnp.sum(lq1[...] * lk1[...], axis=-1, keepdims=True))
           - jnp.exp(jnp.sum(lq2[...] * lk2[...], axis=-1, keepdims=True)) + lambda_init)
    low = _lane_iota() < D_HEAD
    kc, tq, dv = ATTN_KC, ATTN_TQ, 2 * D_HEAD
    k_chunk = lambda j: k_ref[j * kc:(j + 1) * kc, :]
    vt_chunk = lambda j: vt_ref[:, j * kc:(j + 1) * kc]
    outs = {}

    def stream(t):
        q = q_ref[t * tq:(t + 1) * tq, :]
        zero = jnp.zeros_like(q)
        qw = jnp.concatenate([jnp.where(low, q, zero), jnp.where(low, zero, q)], axis=0)
        return (k_chunk, vt_chunk, qw, None)

    def finish(t, acc):
        on = acc[:dv] * (1.0 / acc[dv:dv + 1])
        o = (on[:, :tq] - lam * on[:, tq:]).T
        o_ref[t * tq:(t + 1) * tq, :] = (
            _rms_f32(o, sub_ref[...]) * (1.0 - lambda_init)).astype(BF16)

    _attend_keys_major([stream(t) for t in range(ATTN_TILES)], SEQ // kc, finish,
                       width=ATTN_TILES)


def _d_attn(qk, vt, lq1, lk1, lq2, lk2, subln, lambda_init):
    tq = ATTN_TQ * ATTN_TILES
    nq = SEQ // tq
    vec = pl.BlockSpec((1, D_HEAD), lambda b, h, i: (0, 0))
    return pl.pallas_call(
        functools.partial(_d_attn_kernel, lambda_init=lambda_init),
        grid=(BATCH, D_HEADS, nq),
        in_specs=[
            vec, vec, vec, vec,
            pl.BlockSpec((tq, LANES), lambda b, h, i: (b * nq + i, h)),
            pl.BlockSpec((SEQ, LANES), lambda b, h, i: (b, D_HEADS + h)),
            pl.BlockSpec((LANES, SEQ), lambda b, h, i: (h, b)),
            pl.BlockSpec((1, 2 * D_HEAD), lambda b, h, i: (0, 0)),
        ],
        out_specs=pl.BlockSpec((tq, LANES), lambda b, h, i: (b * nq + i, h)),
        out_shape=jax.ShapeDtypeStruct((TOKENS, 2 * D_HEADS * D_HEAD), BF16),
        compiler_params=_params("parallel", "parallel", "arbitrary"),
        name="d_attn",
    )(lq1, lk1, lq2, lk2, qk, qk, vt, subln)


def _rope_tables(rot_dim, period, lane0):
    r = rot_dim // 2
    inv = ROPE_THETA ** (-jnp.arange(0, rot_dim, 2, dtype=F32) / rot_dim)
    ang = jnp.arange(SEQ, dtype=F32)[:, None] * inv[None, :]
    cos, sin = jnp.cos(ang), jnp.sin(ang)
    lane = np.arange(LANES) % period - lane0
    first = (lane >= 0) & (lane < r)
    second = (lane >= r) & (lane < 2 * r)
    idx = np.where(first, lane, np.where(second, lane - r, 0))
    cg, sg = cos[:, idx], sin[:, idx]
    c = jnp.where(first | second, cg, 1.0)
    sa = jnp.where(first, -sg, 0.0)
    sb = jnp.where(second, sg, 0.0)
    return c, sa, sb


def _a_weights(w_qkv, w_o):
    nh = A_GROUP_HEADS * HEAD_DIM
    pad = jnp.zeros((D_MODEL, A_GROUP_LANES - nh), w_qkv.dtype)

    def cols(part, g):
        base = part * A_HEADS * HEAD_DIM + g * nh
        return [w_qkv[:, base:base + nh], pad]

    pieces = []
    for g in range(len(DIL_PAIRS)):
        pieces += cols(0, g) + cols(1, g)
    for g in range(len(DIL_PAIRS)):
        pieces += cols(2, g)
    w = jnp.concatenate(pieces, axis=1).astype(BF16)
    wo = w_o.reshape(len(DIL_PAIRS), nh, D_MODEL)
    wo = jnp.pad(wo, ((0, 0), (0, A_GROUP_LANES - nh), (0, 0))).astype(BF16)
    return w, wo.reshape(len(DIL_PAIRS) * A_GROUP_LANES, D_MODEL)


def _rotate_half_columns(w_rope):
    r = B_ROPE // 2
    return jnp.concatenate([-w_rope[..., r:], w_rope[..., :r]], axis=-1)


def _b_weights(w_in, w_uq, w_ukv):
    split = B_Q_RANK + B_KV_RANK
    pad_lo = jnp.zeros((D_MODEL, B_NOPE), w_in.dtype)
    pad_hi = jnp.zeros((D_MODEL, LANES - B_NOPE - B_ROPE), w_in.dtype)
    win = jnp.concatenate([
        w_in[:, :split], pad_lo, w_in[:, split:], pad_hi,
        pad_lo, _rotate_half_columns(w_in[:, split:]), pad_hi], axis=1).astype(BF16)
    wuq = w_uq.reshape(B_Q_RANK, B_HEADS, B_NOPE + B_ROPE)
    slot_pad = ((0, 0), (0, 0), (0, LANES - B_NOPE - B_ROPE))
    wuqr = jnp.concatenate([jnp.zeros_like(wuq[:, :, :B_NOPE]),
                            _rotate_half_columns(wuq[:, :, B_NOPE:])], axis=-1)
    wuqr = jnp.pad(wuqr, slot_pad).reshape(B_Q_RANK, B_HEADS * LANES).astype(BF16)
    wuq = jnp.pad(wuq, slot_pad)
    wuq = wuq.reshape(B_Q_RANK, B_HEADS * LANES).astype(BF16)
    wukv = w_ukv.reshape(B_KV_RANK, B_HEADS, B_NOPE + B_V)
    wuk = jnp.pad(wukv[:, :, :B_NOPE], ((0, 0), (0, 0), (0, LANES - B_NOPE)))
    wuk = wuk.reshape(B_KV_RANK, B_HEADS * LANES).astype(BF16)
    wuvt = wukv[:, :, B_NOPE:].reshape(B_KV_RANK, B_HEADS * B_V).T.astype(BF16)
    return win, wuq, wuqr, wuk, wuvt


def _split_qk_vt(w_qkv, n_qk):
    return w_qkv[:, :n_qk].astype(BF16), w_qkv[:, n_qk:].T.astype(BF16)


def kernel(x, p, a_norm, a_w_qkv, a_w_o, b_norm, b_w_in, b_q_norm, b_w_uq, b_kv_norm, b_w_ukv, b_w_o, c_norm, c_w_qkv, c_rpb, c_w_o, d_norm, d_w_qkv, d_lambda_q1, d_lambda_k1, d_lambda_q2, d_lambda_k2, d_subln, d_w_o, mlp_norm, w_up, w_down, ple_norm, w_ple_gate, w_ple_proj, final_norm):
    tabs_p = _rope_tables(ROT_DIM, HEAD_DIM, 0)
    tabs_l = _rope_tables(B_ROPE, LANES, B_NOPE)
    h = x.reshape(TOKENS, D_MODEL)
    row = lambda v: v.reshape(1, -1)

    stacks = (mlp_norm.reshape(DEPTH, 1, D_MODEL), w_up.astype(BF16), w_down.astype(BF16),
              ple_norm.reshape(DEPTH, 1, D_MODEL), w_ple_gate.astype(BF16),
              p.reshape(DEPTH * TOKENS, PLE_DIM), w_ple_proj.astype(BF16), row(final_norm))
    for i in range(DEPTH):
        rest = (i,) + stacks
        if i == 0:
            w, wo = _a_weights(a_w_qkv[0], a_w_o[0])
            parts = _a_qkv(h, row(a_norm[0]), w, tabs_p)
            os_, ls_ = [], []
            for g, (_, dil) in enumerate(DIL_PAIRS):
                o, lse = _a_attn(parts[3 * g], parts[3 * g + 1], parts[3 * g + 2], dil,
                                 token_order=g < len(DIL_PAIRS) - 1)
                os_.append(o)
                ls_.append(lse)
            h = _tail_plain(h, _a_combine(os_, ls_), wo, *rest, name="a_tail")
        elif i == 1:
            b_w = _b_weights(b_w_in[0], b_w_uq[0], b_w_ukv[0])
            q, k, vt = _b_proj(h, row(b_norm[0]), b_w[0], row(b_q_norm[0]), row(b_kv_norm[0]),
                               *b_w[1:], tabs_l)
            h = _tail_plain(h, _b_attn(q, k, vt), b_w_o[0].astype(BF16), *rest, name="b_tail")
        elif i == 2:
            w, wvt = _split_qk_vt(c_w_qkv[0], 2 * C_HEADS * HEAD_DIM)
            qk, vt = _norm_proj(h, row(c_norm[0]), w, wvt, tabs_p, False, "c_qkv")
            o = _c_attn(qk, vt, *_c_column_tables(c_rpb[0]))
            h = _tail_plain(h, o, c_w_o[0].astype(BF16), *rest, name="c_tail")
        else:
            lambda_init = 0.8 - 0.6 * math.exp(-0.3 * i)
            w, wvt = _split_qk_vt(d_w_qkv[0], 2 * 2 * D_HEADS * D_HEAD)
            qk, vt = _norm_proj(h, row(d_norm[0]), w, wvt, tabs_p, True, "d_qkv")
            o = _d_attn(qk, vt, row(d_lambda_q1[0]), row(d_lambda_k1[0]), row(d_lambda_q2[0]),
                        row(d_lambda_k2[0]), row(d_subln[0]), lambda_init)
            h = _tail_plain(h, o, d_w_o[0].astype(BF16), *rest, name="d_tail")
    return h.reshape(BATCH, SEQ, D_MODEL)
```

```python
import functools
import math

import numpy as np
import jax
import jax.numpy as jnp
from jax import lax
from jax.experimental import pallas as pl
from jax.experimental.pallas import tpu as pltpu

F32 = jnp.float32
BF16 = jnp.bfloat16

D_MODEL = 1024
BATCH = 8
SEQ = 2048
DEPTH = 4
TOKENS = BATCH * SEQ
HEAD_DIM = 64
ROPE_THETA = 500000.0
ROT_DIM = HEAD_DIM // 4
NEG_INF = -1e30
RMS_EPS = 1e-6
LOG2_E = math.log2(math.e)

DIL_PAIRS = ((128, 1), (512, 4), (2048, 16))
A_GROUP_HEADS = 5
A_HEADS = A_GROUP_HEADS * len(DIL_PAIRS)
A_BAND_HALF = 64
A_GROUP_LANES = 384
A_GROUP_CHUNKS = A_GROUP_LANES // 128

B_HEADS = 16
B_Q_RANK = 256
B_KV_RANK = 128
B_NOPE = 64
B_ROPE = 32
B_V = 64

C_HEADS = 16
GRID_W = 64
GRID_ROWS = SEQ // GRID_W
NA_ROWS = 8
NA_COLS = 16
C_ROW_GROUP = 4
C_WIN_ROWS = 12
C_KEY_CHUNK = 768
C_WIDTH = 4

D_HEADS = 8
D_HEAD = 64

MLP_HIDDEN = 4 * D_MODEL
PLE_DIM = 256

LANES = 128
VMEM_LIMIT = 48 * 1024 * 1024

PROJ_TM = 512
MLP_TH = 1024
ATTN_TQ = 512
ATTN_KC = 512
ATTN_TILES = 4
ATTN_LOOKAHEAD = 5

B_Q_SCALE = (B_NOPE + B_ROPE) ** -0.5 * LOG2_E
CD_Q_SCALE = HEAD_DIM ** -0.5 * LOG2_E
BAND_TQ = 128
A_QKV_TN = 1024
BAND_UNROLL = 8


def _params(*sem):
    return pltpu.CompilerParams(dimension_semantics=sem, vmem_limit_bytes=VMEM_LIMIT)


def _rms_f32(x, g):
    ms = jnp.mean(x * x, axis=-1, keepdims=True)
    return x * lax.rsqrt(ms + RMS_EPS) * g


def _rope_chunk(y, c, sa, sb, shift):
    return (y * c + pltpu.roll(y, LANES - shift, 1) * sa + pltpu.roll(y, shift, 1) * sb)


def _softmax_parts(s):
    m = jnp.max(s, axis=-1, keepdims=True)
    e = jnp.exp(s - m)
    l = jnp.sum(e, axis=-1, keepdims=True)
    return m, e, l


def _dot(a, b):
    return jnp.dot(a, b, preferred_element_type=F32)


def _dot_nt(a, b):
    return lax.dot_general(a, b, (((1,), (1,)), ((), ())), preferred_element_type=F32)


def _lane_iota():
    return lax.broadcasted_iota(jnp.int32, (1, LANES), 1)


def _norm_proj_kernel(x_ref, g_ref, w_ref, wvt_ref, c_ref, sa_ref, sb_ref, qk_ref, vt_ref,
                      *, rope, tn):
    xn = _rms_f32(x_ref[...], g_ref[...]).astype(BF16)
    n = qk_ref.shape[1]
    c, sa, sb = c_ref[...], sa_ref[...], sb_ref[...]
    for j in range(n // tn):
        y = _dot(xn, w_ref[:, j * tn:(j + 1) * tn])
        for i in range(tn // LANES):
            col = j * tn + i * LANES
            chunk = y[:, i * LANES:(i + 1) * LANES]
            if rope:
                chunk = _rope_chunk(chunk, c, sa, sb, ROT_DIM // 2)
            if col < n // 2:
                chunk = chunk * CD_Q_SCALE
            qk_ref[:, col:col + LANES] = chunk.astype(BF16)
    vt_ref[...] = _dot_nt(wvt_ref[...], xn).astype(BF16)


def _norm_proj(x, g, w, wvt, tabs, rope, name):
    n = w.shape[1]
    nv = wvt.shape[0]
    tm = PROJ_TM
    nb = SEQ // tm
    tab_spec = pl.BlockSpec((tm, LANES), lambda i: (i % nb, 0))
    return pl.pallas_call(
        functools.partial(_norm_proj_kernel, rope=rope, tn=512),
        grid=(TOKENS // tm,),
        in_specs=[
            pl.BlockSpec((tm, D_MODEL), lambda i: (i, 0)),
            pl.BlockSpec((1, D_MODEL), lambda i: (0, 0)),
            pl.BlockSpec((D_MODEL, n), lambda i: (0, 0)),
            pl.BlockSpec((nv, D_MODEL), lambda i: (0, 0)),
            tab_spec, tab_spec, tab_spec,
        ],
        out_specs=[pl.BlockSpec((tm, n), lambda i: (i, 0)),
                   pl.BlockSpec((nv, tm), lambda i: (0, i))],
        out_shape=[jax.ShapeDtypeStruct((TOKENS, n), BF16),
                   jax.ShapeDtypeStruct((nv, TOKENS), BF16)],
        compiler_params=_params("parallel"),
        name=name,
    )(x, g, w, wvt, *tabs)


def _mixer_residual(x_ref, o_ref, w_ref):
    return x_ref[...] + _dot(o_ref[...], w_ref[...])


def _tail_kernel(*refs, n_front, front, final):
    front_refs = refs[:n_front]
    gm_ref, wu_ref, wd_ref, gp_ref, wg_ref, p_ref, wp_ref, fg_ref, out_ref = refs[n_front:]
    x1 = front(*front_refs)
    xn = _rms_f32(x1, gm_ref[...]).astype(BF16)
    acc = x1
    for j in range(MLP_HIDDEN // MLP_TH):
        h = _dot(xn, wu_ref[:, j * MLP_TH:(j + 1) * MLP_TH])
        h = jnp.square(jnp.maximum(h, 0.0)).astype(BF16)
        acc = acc + _dot(h, wd_ref[j * MLP_TH:(j + 1) * MLP_TH, :])
    gate = jax.nn.sigmoid(_dot(_rms_f32(acc, gp_ref[...]).astype(BF16), wg_ref[...]))
    y = acc + gate * _dot(p_ref[...].astype(BF16), wp_ref[...])
    if final:
        y = _rms_f32(y, fg_ref[...])
    out_ref[...] = y


def _resident(a):
    return pl.BlockSpec(a.shape, lambda i: (0,) * a.ndim, pipeline_mode=pl.Buffered(1))


def _resident_layer(a, layer):
    return pl.BlockSpec((None,) + a.shape[1:], lambda i: (layer,) + (0,) * (a.ndim - 1),
                        pipeline_mode=pl.Buffered(1))


def _tail(front, front_args, front_specs, layer, gm, wu, wd, gp, wg, p, wp, fg, name):
    tm = PROJ_TM
    nt = TOKENS // tm
    shared = (gm, wu, wd, gp, wg)
    return pl.pallas_call(
        functools.partial(_tail_kernel, n_front=len(front_args), front=front,
                          final=layer == DEPTH - 1),
        grid=(nt,),
        in_specs=list(front_specs) + [_resident_layer(a, layer) for a in shared] + [
            pl.BlockSpec((tm, PLE_DIM), lambda i: (layer * nt + i, 0)),
            _resident_layer(wp, layer), _resident(fg)],
        out_specs=pl.BlockSpec((tm, D_MODEL), lambda i: (i, 0)),
        out_shape=jax.ShapeDtypeStruct((TOKENS, D_MODEL), F32),
        compiler_params=_params("parallel"),
        name=name,
    )(*front_args, *shared, p, wp, fg)


def _tail_plain(x, o, w_o, *rest, name):
    tm = PROJ_TM
    specs = [pl.BlockSpec((tm, D_MODEL), lambda i: (i, 0)),
             pl.BlockSpec((tm, o.shape[1]), lambda i: (i, 0)), _resident(w_o)]
    return _tail(_mixer_residual, (x, o, w_o), specs, *rest, name=name)


def _a_qkv_kernel(x_ref, g_ref, w_ref, c_ref, sa_ref, sb_ref, *refs):
    outs, stage = refs[:9], refs[9]
    tm = x_ref.shape[0]
    nc = A_GROUP_CHUNKS
    n_groups = len(DIL_PAIRS)
    xn = _rms_f32(x_ref[...], g_ref[...]).astype(BF16)
    c, sa, sb = c_ref[...], sa_ref[...], sb_ref[...]
    n_chunks = 3 * n_groups * nc
    per_dot = A_QKV_TN // LANES
    for first in range(0, n_chunks, per_dot):
        last = min(first + per_dot, n_chunks)
        y = _dot(xn, w_ref[:, first * LANES:last * LANES])
        for idx in range(first, last):
            chunk = y[:, (idx - first) * LANES:(idx - first + 1) * LANES]
            if idx < 2 * n_groups * nc:
                g, part, ci = idx // (2 * nc), (idx // nc) % 2, idx % nc
                chunk = _rope_chunk(chunk, c, sa, sb, ROT_DIM // 2)
            else:
                g, part, ci = (idx - 2 * n_groups * nc) // nc, 2, idx % nc
            dil = DIL_PAIRS[g][1]
            dst = outs[3 * g + part]
            if dil == 1:
                dst[0, ci, 0] = chunk.astype(BF16)
            else:
                slot = idx % 2
                stage[slot] = chunk
                for r in range(dil):
                    dst[0, ci, r] = stage[slot, pl.ds(r, tm // dil, stride=dil), :].astype(BF16)


def _a_qkv(x, g, w, tabs):
    tm = PROJ_TM
    nb = SEQ // tm
    gl = A_GROUP_LANES
    tab_spec = pl.BlockSpec((tm, LANES), lambda i: (i % nb, 0))
    out_shapes, out_specs = [], []
    nc = A_GROUP_CHUNKS
    for _, dil in DIL_PAIRS:
        for _ in range(3):
            out_shapes.append(jax.ShapeDtypeStruct((BATCH, nc, dil, SEQ // dil, LANES), BF16))
            out_specs.append(pl.BlockSpec((1, nc, dil, tm // dil, LANES),
                                          lambda i: (i // nb, 0, 0, i % nb, 0)))
    return pl.pallas_call(
        _a_qkv_kernel,
        grid=(TOKENS // tm,),
        in_specs=[
            pl.BlockSpec((tm, D_MODEL), lambda i: (i, 0)),
            pl.BlockSpec((1, D_MODEL), lambda i: (0, 0)),
            pl.BlockSpec((D_MODEL, 9 * gl), lambda i: (0, 0)),
            tab_spec, tab_spec, tab_spec,
        ],
        out_specs=out_specs,
        out_shape=out_shapes,
        scratch_shapes=[pltpu.VMEM((2, tm, LANES), F32)],
        compiler_params=_params("parallel"),
        name="a_qkv",
    )(x, g, w, *tabs)


def _a_attn_kernel(q_ref, k_ref, v_ref, o_ref, lse_ref, *, seg, dil, token_order):
    tq = BAND_TQ
    kw = tq + 2 * A_BAND_HALF
    low = _lane_iota() < HEAD_DIM
    ones = jnp.ones((kw, LANES), BF16)

    def tile(u):
        p0 = pl.multiple_of(u * tq, tq)
        if seg >= kw:
            seg0 = (p0 // seg) * seg
            ws = jnp.clip(p0 - A_BAND_HALF, seg0, seg0 + seg - kw)
        else:
            ws = (p0 // kw) * kw
        ws = pl.multiple_of(ws, A_BAND_HALF)
        rows = p0 + lax.broadcasted_iota(jnp.int32, (tq, 1), 0)
        cols = ws + lax.broadcasted_iota(jnp.int32, (1, kw), 1)
        valid = jnp.abs(cols - rows) <= A_BAND_HALF
        if seg < kw:
            seg_of = lambda p: lax.shift_right_logical(p, seg.bit_length() - 1)
            valid = jnp.logical_and(valid, seg_of(cols) == seg_of(rows))
        r = p0 // seg
        q0 = p0 - r * seg
        for c in range(A_GROUP_CHUNKS):
            qc = q_ref[0, c, pl.ds(p0, tq), :] * jnp.asarray(HEAD_DIM ** -0.5, BF16)
            kc = k_ref[0, c, pl.ds(ws, kw), :]
            v1 = jnp.concatenate([v_ref[0, c, pl.ds(ws, kw), :], ones], axis=1)
            outs, lses = [], []
            n_heads = 2 if 2 * c + 1 < A_GROUP_HEADS else 1
            for hh in range(n_heads):
                qm = jnp.where(low if hh == 0 else jnp.logical_not(low), qc, jnp.zeros_like(qc))
                s = jnp.where(valid, _dot_nt(qm, kc), NEG_INF)
                m = jnp.max(s, axis=-1, keepdims=True)
                ol = _dot(jnp.exp(s - m).astype(BF16), v1)
                l = ol[:, LANES:]
                outs.append(ol[:, :LANES] * (1.0 / l))
                lses.append(m + jnp.log(l))
            if n_heads == 2:
                o = jnp.where(low, outs[0], outs[1])
                ls = jnp.where(low, lses[0], lses[1])
            else:
                o = jnp.where(low, outs[0], 0.0)
                ls = jnp.where(low, lses[0], 0.0)
            if not token_order or dil == 1:
                dst = pl.ds(p0, tq)
            else:
                dst = pl.ds(r + dil * q0, tq, stride=dil)
            o_ref[0, c, dst, :] = o
            lse_ref[0, c, dst, :] = ls

    def body(i, carry):
        for j in range(BAND_UNROLL):
            tile(i * BAND_UNROLL + j)
        return carry

    lax.fori_loop(0, SEQ // tq // BAND_UNROLL, body, 0)


def _a_attn(q, k, v, dil, token_order):
    seg = SEQ // dil
    nc = A_GROUP_CHUNKS
    q, k, v = (a.reshape(BATCH, nc, SEQ, LANES) for a in (q, k, v))
    in_spec = pl.BlockSpec((1, nc, SEQ, LANES), lambda b: (b, 0, 0, 0))
    out_spec = pl.BlockSpec((1, nc, SEQ, LANES), lambda b: (b, 0, 0, 0))
    out_shape = jax.ShapeDtypeStruct((BATCH, nc, SEQ, LANES), F32)
    return pl.pallas_call(
        functools.partial(_a_attn_kernel, seg=seg, dil=dil, token_order=token_order),
        grid=(BATCH,),
        in_specs=[in_spec, in_spec, in_spec],
        out_specs=[out_spec, out_spec],
        out_shape=[out_shape, out_shape],
        compiler_params=_params("parallel"),
        name=f"a_attn_d{dil}",
    )(q, k, v)


def _a_combine_kernel(o0, o1, o2, l0, l1, l2, out_ref):
    tm = out_ref.shape[0]
    dil = DIL_PAIRS[-1][1]
    seg = SEQ // dil
    pos0 = (pl.program_id(0) % (SEQ // tm)) * (tm // dil)

    def token_rows(ref, c):
        return jnp.concatenate(
            [ref[0, c, pl.ds(pos0 + p, dil, stride=seg), :] for p in range(tm // dil)], axis=0)

    for c in range(A_GROUP_CHUNKS):
        ls = [l0[0, c], l1[0, c], token_rows(l2, c)]
        os_ = [o0[0, c], o1[0, c], token_rows(o2, c)]
        m = jnp.maximum(jnp.maximum(ls[0], ls[1]), ls[2])
        es = [jnp.exp(l - m) for l in ls]
        inv = 1.0 / (es[0] + es[1] + es[2])
        for g in range(len(DIL_PAIRS)):
            col = g * A_GROUP_LANES + c * LANES
            out_ref[:, col:col + LANES] = (os_[g] * (es[g] * inv)).astype(BF16)


def _a_combine(os_, ls_):
    tm = PROJ_TM
    nb = SEQ // tm
    n = len(DIL_PAIRS) * A_GROUP_LANES
    part = pl.BlockSpec((1, A_GROUP_CHUNKS, tm, LANES), lambda i: (i // nb, 0, i % nb, 0))
    whole = pl.BlockSpec((1, A_GROUP_CHUNKS, SEQ, LANES), lambda i: (i // nb, 0, 0, 0))
    return pl.pallas_call(
        _a_combine_kernel,
        grid=(TOKENS // tm,),
        in_specs=[part, part, whole, part, part, whole],
        out_specs=pl.BlockSpec((tm, n), lambda i: (i, 0)),
        out_shape=jax.ShapeDtypeStruct((TOKENS, n), BF16),
        compiler_params=_params("parallel"),
        name="a_combine",
    )(*os_, *ls_)


def _b_proj_kernel(x_ref, g_ref, win_ref, qn_ref, kvn_ref, wuq_ref, wuqr_ref, wuk_ref, wuvt_ref,
                   c_ref, sa_ref, sb_ref, q_out, k_out, vt_out):
    c = c_ref[...]
    s = sb_ref[...] - sa_ref[...]
    slot = B_Q_RANK + B_KV_RANK
    xn = _rms_f32(x_ref[...], g_ref[...]).astype(BF16)
    z = _dot(xn, win_ref[...])
    cq = _rms_f32(z[:, :B_Q_RANK], qn_ref[...]).astype(BF16)
    ckv = _rms_f32(z[:, B_Q_RANK:slot], kvn_ref[...]).astype(BF16)
    k_rope = z[:, slot:slot + LANES] * c + z[:, slot + LANES:] * s
    q = _dot(cq, wuq_ref[...])
    qr = _dot(cq, wuqr_ref[...])
    k = _dot(ckv, wuk_ref[...])
    cq_tab, sq_tab = c * B_Q_SCALE, s * B_Q_SCALE
    for h in range(B_HEADS):
        sl = slice(h * LANES, (h + 1) * LANES)
        q_out[:, sl] = (q[:, sl] * cq_tab + qr[:, sl] * sq_tab).astype(BF16)
        k_out[:, sl] = (k[:, sl] + k_rope).astype(BF16)
    vt_out[...] = _dot_nt(wuvt_ref[...], ckv).astype(BF16)


def _b_proj(x, g, win, qn, kvn, wuq, wuqr, wuk, wuvt, tabs):
    tm = PROJ_TM
    nb = SEQ // tm
    tab_spec = pl.BlockSpec((tm, LANES), lambda i: (i % nb, 0))

    def full(a):
        return pl.BlockSpec(a.shape, lambda i: (0,) * a.ndim)

    nqk = B_HEADS * LANES
    nv = B_HEADS * B_V
    return pl.pallas_call(
        _b_proj_kernel,
        grid=(TOKENS // tm,),
        in_specs=[pl.BlockSpec((tm, D_MODEL), lambda i: (i, 0)), full(g), full(win), full(qn),
                  full(kvn), full(wuq), full(wuqr), full(wuk), full(wuvt),
                  tab_spec, tab_spec, tab_spec],
        out_specs=[pl.BlockSpec((tm, nqk), lambda i: (i, 0)),
                   pl.BlockSpec((tm, nqk), lambda i: (i, 0)),
                   pl.BlockSpec((nv, tm), lambda i: (0, i))],
        out_shape=[jax.ShapeDtypeStruct((TOKENS, nqk), BF16),
                   jax.ShapeDtypeStruct((TOKENS, nqk), BF16),
                   jax.ShapeDtypeStruct((nv, TOKENS), BF16)],
        compiler_params=_params("parallel"),
        name="b_proj",
    )(x, g, win, qn, kvn, wuq, wuqr, wuk, wuvt, *tabs)


def _with_ones_rows(vt):
    return jnp.concatenate([vt, jnp.ones((16, vt.shape[1]), vt.dtype)], axis=0)


def _attend_keys_major(streams, n_chunks, finish, width):
    items = []
    for s0 in range(0, len(streams), width):
        items += [(s, j) for j in range(n_chunks) for s in range(s0, min(s0 + width, len(streams)))]
    state = [None] * len(streams)

    def scores(item):
        s, j = item
        k_chunk, _, qw, _ = streams[s]
        return _dot_nt(k_chunk(j), qw)

    def absorb(item, st):
        s, j = item
        vt1 = _with_ones_rows(streams[s][1](j))
        bias_chunk = streams[s][3]
        if bias_chunk is not None:
            st = st + bias_chunk(j)
        mj = jnp.max(st, axis=0, keepdims=True)
        if j == 0:
            state[s] = (mj, _dot(vt1, jnp.exp2(st - mj).astype(BF16)))
        else:
            m, acc = state[s]
            m_new = jnp.maximum(m, mj)
            e = jnp.exp2(st - m_new).astype(BF16)
            state[s] = (m_new, acc * jnp.exp2(m - m_new) + _dot(vt1, e))
        if j == n_chunks - 1:
            finish(s, state[s][1])
            state[s] = None

    ahead = ATTN_LOOKAHEAD
    pending = [scores(item) for item in items[:ahead]]
    for idx, item in enumerate(items):
        if idx + ahead < len(items):
            pending.append(scores(items[idx + ahead]))
        absorb(item, pending.pop(0))


def _b_attn_kernel(q0_ref, q1_ref, k0_ref, k1_ref, vt_ref, o_ref):
    kc, tq = ATTN_KC, ATTN_TQ
    outs = {}

    def stream(t, hh):
        q_ref, k_ref = (q0_ref, k0_ref) if hh == 0 else (q1_ref, k1_ref)
        rows = slice(hh * B_V, (hh + 1) * B_V)
        return (lambda j: k_ref[j * kc:(j + 1) * kc, :],
                lambda j: vt_ref[rows, j * kc:(j + 1) * kc],
                q_ref[t * tq:(t + 1) * tq, :], None)

    def finish(s, acc):
        t, hh = divmod(s, 2)
        outs[hh] = acc[:B_V] * (1.0 / acc[B_V:B_V + 1])
        if hh == 1:
            o = jnp.concatenate([outs[0], outs[1]], axis=0)
            o_ref[t * tq:(t + 1) * tq, :] = o.T.astype(BF16)

    _attend_keys_major([stream(t, hh) for t in range(ATTN_TILES) for hh in range(2)],
                       SEQ // kc, finish, width=2 * ATTN_TILES)


def _b_attn(q, k, vt):
    rows = ATTN_TQ * ATTN_TILES
    nq = SEQ // rows
    return pl.pallas_call(
        _b_attn_kernel,
        grid=(BATCH, B_HEADS // 2, nq),
        in_specs=[
            pl.BlockSpec((rows, LANES), lambda b, h, i: (b * nq + i, 2 * h)),
            pl.BlockSpec((rows, LANES), lambda b, h, i: (b * nq + i, 2 * h + 1)),
            pl.BlockSpec((SEQ, LANES), lambda b, h, i: (b, 2 * h)),
            pl.BlockSpec((SEQ, LANES), lambda b, h, i: (b, 2 * h + 1)),
            pl.BlockSpec((2 * B_V, SEQ), lambda b, h, i: (h, b)),
        ],
        out_specs=pl.BlockSpec((rows, LANES), lambda b, h, i: (b * nq + i, h)),
        out_shape=jax.ShapeDtypeStruct((TOKENS, B_HEADS * B_V), BF16),
        compiler_params=_params("parallel", "parallel", "arbitrary"),
        name="b_attn",
    )(q, q, k, k, vt)


def _c_group_geometry(gi):
    first_row = gi * C_ROW_GROUP
    wrow = min(max(first_row - NA_ROWS // 2, 0), GRID_ROWS - C_WIN_ROWS)
    n_groups = GRID_ROWS // C_ROW_GROUP
    case = 0 if gi == 0 else (2 if gi == n_groups - 1 else 1)
    return first_row, wrow, case


def _c_row_offset_slot(gi, kr, i):
    first_row, wrow, _ = _c_group_geometry(gi)
    qrow, krow = first_row + i, wrow + kr
    rs = min(max(qrow - NA_ROWS // 2, 0), GRID_ROWS - NA_ROWS)
    if rs <= krow < rs + NA_ROWS:
        return krow - qrow + NA_ROWS - 1
    return 2 * NA_ROWS - 1


def _c_attn_kernel(q_ref, k_ref, vt_ref, left_ref, right_ref, o_ref, bias_ref):
    low = _lane_iota() < HEAD_DIM
    nq = C_ROW_GROUP * GRID_W
    nk = C_WIN_ROWS * GRID_W
    n_groups = GRID_ROWS // C_ROW_GROUP

    @pl.when(pl.program_id(1) == 0)
    def _():
        for case, gi in enumerate((0, 1, n_groups - 1)):
            for kr in range(C_WIN_ROWS):
                for hh in range(2):
                    for ip in range(C_ROW_GROUP // 2):
                        a0 = _c_row_offset_slot(gi, kr, 2 * ip)
                        a1 = _c_row_offset_slot(gi, kr, 2 * ip + 1)
                        col = (hh * C_ROW_GROUP // 2 + ip) * LANES
                        bias_ref[case, kr * GRID_W:(kr + 1) * GRID_W, col:col + LANES] = (
                            left_ref[0, hh, a0] + right_ref[0, hh, a1])

    kc = C_KEY_CHUNK

    def stream(gi):
        first_row, wrow, case = _c_group_geometry(gi)
        q0, ws = first_row * GRID_W, wrow * GRID_W
        qc = q_ref[q0:q0 + nq, :]
        zero = jnp.zeros_like(qc)
        qq = jnp.concatenate([jnp.where(low, qc, zero), jnp.where(low, zero, qc)], axis=0)
        return (lambda j: k_ref[ws + j * kc:ws + (j + 1) * kc, :],
                lambda j: vt_ref[:, ws + j * kc:ws + (j + 1) * kc],
                qq,
                lambda j: bias_ref[case, j * kc:(j + 1) * kc, :])

    def finish(gi, acc):
        q0 = _c_group_geometry(gi)[0] * GRID_W
        ot = acc[:LANES] * (1.0 / acc[LANES:LANES + 1])
        o = jnp.concatenate([ot[:HEAD_DIM, :nq], ot[HEAD_DIM:, nq:]], axis=0)
        o_ref[q0:q0 + nq, :] = o.T.astype(BF16)

    _attend_keys_major([stream(gi) for gi in range(n_groups)], nk // kc, finish, width=C_WIDTH)


def _c_attn(qk, vt, left, right):
    nq = C_ROW_GROUP * GRID_W
    nk = C_WIN_ROWS * GRID_W
    npair = C_HEADS // 2
    tab_spec = pl.BlockSpec((1, 2, 2 * NA_ROWS, GRID_W, LANES), lambda h, b: (h, 0, 0, 0, 0))
    return pl.pallas_call(
        _c_attn_kernel,
        grid=(npair, BATCH),
        in_specs=[
            pl.BlockSpec((SEQ, LANES), lambda h, b: (b, h)),
            pl.BlockSpec((SEQ, LANES), lambda h, b: (b, npair + h)),
            pl.BlockSpec((LANES, SEQ), lambda h, b: (h, b)),
            tab_spec, tab_spec,
        ],
        out_specs=pl.BlockSpec((SEQ, LANES), lambda h, b: (b, h)),
        out_shape=jax.ShapeDtypeStruct((TOKENS, C_HEADS * HEAD_DIM), BF16),
        scratch_shapes=[pltpu.VMEM((3, nk, 2 * nq), F32)],
        compiler_params=_params("arbitrary", "arbitrary"),
        name="c_attn",
    )(qk, qk, vt, left, right)


def _c_column_tables(rpb):
    kc, qc = np.arange(GRID_W)[:, None], np.arange(GRID_W)[None, :]
    win0 = np.clip(qc - NA_COLS // 2, 0, GRID_W - NA_COLS)
    col_ok = (kc >= win0) & (kc < win0 + NA_COLS)
    col_off = np.clip(kc - qc + NA_COLS - 1, 0, 2 * NA_COLS - 2)
    col_sel = (col_off[None] == np.arange(2 * NA_COLS - 1)[:, None, None]).astype(np.float32)
    cols = jnp.einsum("hab,bkq->hakq", rpb, col_sel, precision=lax.Precision.HIGHEST)
    cols = jnp.where(col_ok, cols * LOG2_E, NEG_INF)
    cols = jnp.concatenate([cols, jnp.full((C_HEADS, 1, GRID_W, GRID_W), NEG_INF, F32)], axis=1)
    cols = cols.reshape(C_HEADS // 2, 2, 2 * NA_ROWS, GRID_W, GRID_W)
    zeros = jnp.zeros_like(cols)
    return jnp.concatenate([cols, zeros], axis=-1), jnp.concatenate([zeros, cols], axis=-1)


def _d_attn_kernel(lq1, lk1, lq2, lk2, q_ref, k_ref, vt_ref, sub_ref, o_ref, *, lambda_init):
    lam = (jnp.exp(jnp.sum(lq1[...] * lk1[...], axis=-1, keepdims=True))
           - jnp.exp(jnp.sum(lq2[...] * lk2[...], axis=-1, keepdims=True)) + lambda_init)
    low = _lane_iota() < D_HEAD
    kc, tq, dv = ATTN_KC, ATTN_TQ, 2 * D_HEAD
    k_chunk = lambda j: k_ref[j * kc:(j + 1) * kc, :]
    vt_chunk = lambda j: vt_ref[:, j * kc:(j + 1) * kc]
    outs = {}

    def stream(t):
        q = q_ref[t * tq:(t + 1) * tq, :]
        zero = jnp.zeros_like(q)
        qw = jnp.concatenate([jnp.where(low, q, zero), jnp.where(low, zero, q)], axis=0)
        return (k_chunk, vt_chunk, qw, None)

    def finish(t, acc):
        on = acc[:dv] * (1.0 / acc[dv:dv + 1])
        o = (on[:, :tq] - lam * on[:, tq:]).T
        o_ref[t * tq:(t + 1) * tq, :] = (
            _rms_f32(o, sub_ref[...]) * (1.0 - lambda_init)).astype(BF16)

    _attend_keys_major([stream(t) for t in range(ATTN_TILES)], SEQ // kc, finish,
                       width=ATTN_TILES)


def _d_attn(qk, vt, lq1, lk1, lq2, lk2, subln, lambda_init):
    tq = ATTN_TQ * ATTN_TILES
    nq = SEQ // tq
    vec = pl.BlockSpec((1, D_HEAD), lambda b, h, i: (0, 0))
    return pl.pallas_call(
        functools.partial(_d_attn_kernel, lambda_init=lambda_init),
        grid=(BATCH, D_HEADS, nq),
        in_specs=[
            vec, vec, vec, vec,
            pl.BlockSpec((tq, LANES), lambda b, h, i: (b * nq + i, h)),
            pl.BlockSpec((SEQ, LANES), lambda b, h, i: (b, D_HEADS + h)),
            pl.BlockSpec((LANES, SEQ), lambda b, h, i: (h, b)),
            pl.BlockSpec((1, 2 * D_HEAD), lambda b, h, i: (0, 0)),
        ],
        out_specs=pl.BlockSpec((tq, LANES), lambda b, h, i: (b * nq + i, h)),
        out_shape=jax.ShapeDtypeStruct((TOKENS, 2 * D_HEADS * D_HEAD), BF16),
        compiler_params=_params("parallel", "parallel", "arbitrary"),
        name="d_attn",
    )(lq1, lk1, lq2, lk2, qk, qk, vt, subln)


def _rope_tables(rot_dim, period, lane0):
    r = rot_dim // 2
    inv = ROPE_THETA ** (-jnp.arange(0, rot_dim, 2, dtype=F32) / rot_dim)
    ang = jnp.arange(SEQ, dtype=F32)[:, None] * inv[None, :]
    cos, sin = jnp.cos(ang), jnp.sin(ang)
    lane = np.arange(LANES) % period - lane0
    first = (lane >= 0) & (lane < r)
    second = (lane >= r) & (lane < 2 * r)
    idx = np.where(first, lane, np.where(second, lane - r, 0))
    cg, sg = cos[:, idx], sin[:, idx]
    c = jnp.where(first | second, cg, 1.0)
    sa = jnp.where(first, -sg, 0.0)
    sb = jnp.where(second, sg, 0.0)
    return c, sa, sb


def _a_weights(w_qkv, w_o):
    nh = A_GROUP_HEADS * HEAD_DIM
    pad = jnp.zeros((D_MODEL, A_GROUP_LANES - nh), w_qkv.dtype)

    def cols(part, g):
        base = part * A_HEADS * HEAD_DIM + g * nh
        return [w_qkv[:, base:base + nh], pad]

    pieces = []
    for g in range(len(DIL_PAIRS)):
        pieces += cols(0, g) + cols(1, g)
    for g in range(len(DIL_PAIRS)):
        pieces += cols(2, g)
    w = jnp.concatenate(pieces, axis=1).astype(BF16)
    wo = w_o.reshape(len(DIL_PAIRS), nh, D_MODEL)
    wo = jnp.pad(wo, ((0, 0), (0, A_GROUP_LANES - nh), (0, 0))).astype(BF16)
    return w, wo.reshape(len(DIL_PAIRS) * A_GROUP_LANES, D_MODEL)


def _rotate_half_columns(w_rope):
    r = B_ROPE // 2
    return jnp.concatenate([-w_rope[..., r:], w_rope[..., :r]], axis=-1)


def _b_weights(w_in, w_uq, w_ukv):
    split = B_Q_RANK + B_KV_RANK
    pad_lo = jnp.zeros((D_MODEL, B_NOPE), w_in.dtype)
    pad_hi = jnp.zeros((D_MODEL, LANES - B_NOPE - B_ROPE), w_in.dtype)
    win = jnp.concatenate([
        w_in[:, :split], pad_lo, w_in[:, split:], pad_hi,
        pad_lo, _rotate_half_columns(w_in[:, split:]), pad_hi], axis=1).astype(BF16)
    wuq = w_uq.reshape(B_Q_RANK, B_HEADS, B_NOPE + B_ROPE)
    slot_pad = ((0, 0), (0, 0), (0, LANES - B_NOPE - B_ROPE))
    wuqr = jnp.concatenate([jnp.zeros_like(wuq[:, :, :B_NOPE]),
                            _rotate_half_columns(wuq[:, :, B_NOPE:])], axis=-1)
    wuqr = jnp.pad(wuqr, slot_pad).reshape(B_Q_RANK, B_HEADS * LANES).astype(BF16)
    wuq = jnp.pad(wuq, slot_pad)
    wuq = wuq.reshape(B_Q_RANK, B_HEADS * LANES).astype(BF16)
    wukv = w_ukv.reshape(B_KV_RANK, B_HEADS, B_NOPE + B_V)
    wuk = jnp.pad(wukv[:, :, :B_NOPE], ((0, 0), (0, 0), (0, LANES - B_NOPE)))
    wuk = wuk.reshape(B_KV_RANK, B_HEADS * LANES).astype(BF16)
    wuvt = wukv[:, :, B_NOPE:].reshape(B_KV_RANK, B_HEADS * B_V).T.astype(BF16)
    return win, wuq, wuqr, wuk, wuvt


def _split_qk_vt(w_qkv, n_qk):
    return w_qkv[:, :n_qk].astype(BF16), w_qkv[:, n_qk:].T.astype(BF16)


def kernel(x, p, a_norm, a_w_qkv, a_w_o, b_norm, b_w_in, b_q_norm, b_w_uq, b_kv_norm, b_w_ukv, b_w_o, c_norm, c_w_qkv, c_rpb, c_w_o, d_norm, d_w_qkv, d_lambda_q1, d_lambda_k1, d_lambda_q2, d_lambda_k2, d_subln, d_w_o, mlp_norm, w_up, w_down, ple_norm, w_ple_gate, w_ple_proj, final_norm):
    tabs_p = _rope_tables(ROT_DIM, HEAD_DIM, 0)
    tabs_l = _rope_tables(B_ROPE, LANES, B_NOPE)
    h = x.reshape(TOKENS, D_MODEL)
    row = lambda v: v.reshape(1, -1)

    stacks = (mlp_norm.reshape(DEPTH, 1, D_MODEL), w_up.astype(BF16), w_down.astype(BF16),
              ple_norm.reshape(DEPTH, 1, D_MODEL), w_ple_gate.astype(BF16),
              p.reshape(DEPTH * TOKENS, PLE_DIM), w_ple_proj.astype(BF16), row(final_norm))
    for i in range(DEPTH):
        rest = (i,) + stacks
        if i == 0:
            w, wo = _a_weights(a_w_qkv[0], a_w_o[0])
            parts = _a_qkv(h, row(a_norm[0]), w, tabs_p)
            os_, ls_ = [], []
            for g, (_, dil) in enumerate(DIL_PAIRS):
                o, lse = _a_attn(parts[3 * g], parts[3 * g + 1], parts[3 * g + 2], dil,
                                 token_order=g < len(DIL_PAIRS) - 1)
                os_.append(o)
                ls_.append(lse)
            h = _tail_plain(h, _a_combine(os_, ls_), wo, *rest, name="a_tail")
        elif i == 1:
            b_w = _b_weights(b_w_in[0], b_w_uq[0], b_w_ukv[0])
            q, k, vt = _b_proj(h, row(b_norm[0]), b_w[0], row(b_q_norm[0]), row(b_kv_norm[0]),
                               *b_w[1:], tabs_l)
            h = _tail_plain(h, _b_attn(q, k, vt), b_w_o[0].astype(BF16), *rest, name="b_tail")
        elif i == 2:
            w, wvt = _split_qk_vt(c_w_qkv[0], 2 * C_HEADS * HEAD_DIM)
            qk, vt = _norm_proj(h, row(c_norm[0]), w, wvt, tabs_p, False, "c_qkv")
            o = _c_attn(qk, vt, *_c_column_tables(c_rpb[0]))
            h = _tail_plain(h, o, c_w_o[0].astype(BF16), *rest, name="c_tail")
        else:
            lambda_init = 0.8 - 0.6 * math.exp(-0.3 * i)
            w, wvt = _split_qk_vt(d_w_qkv[0], 2 * 2 * D_HEADS * D_HEAD)
            qk, vt = _norm_proj(h, row(d_norm[0]), w, wvt, tabs_p, True, "d_qkv")
            o = _d_attn(qk, vt, row(d_lambda_q1[0]), row(d_lambda_k1[0]), row(d_lambda_q2[0]),
                        row(d_lambda_k2[0]), row(d_subln[0]), lambda_init)
            h = _tail_plain(h, o, d_w_o[0].astype(BF16), *rest, name="d_tail")
    return h.reshape(BATCH, SEQ, D_MODEL)
```

```python
import functools
import math

import numpy as np
import jax
import jax.numpy as jnp
from jax import lax
from jax.experimental import pallas as pl
from jax.experimental.pallas import tpu as pltpu

F32 = jnp.float32
BF16 = jnp.bfloat16

D_MODEL = 1024
BATCH = 8
SEQ = 2048
DEPTH = 4
TOKENS = BATCH * SEQ
HEAD_DIM = 64
ROPE_THETA = 500000.0
ROT_DIM = HEAD_DIM // 4
NEG_INF = -1e30
RMS_EPS = 1e-6
LOG2_E = math.log2(math.e)

DIL_PAIRS = ((128, 1), (512, 4), (2048, 16))
A_GROUP_HEADS = 5
A_HEADS = A_GROUP_HEADS * len(DIL_PAIRS)
A_BAND_HALF = 64
A_GROUP_LANES = 384
A_GROUP_CHUNKS = A_GROUP_LANES // 128

B_HEADS = 16
B_Q_RANK = 256
B_KV_RANK = 128
B_NOPE = 64
B_ROPE = 32
B_V = 64

C_HEADS = 16
GRID_W = 64
GRID_ROWS = SEQ // GRID_W
NA_ROWS = 8
NA_COLS = 16
C_ROW_GROUP = 4
C_WIN_ROWS = 12
C_KEY_CHUNK = 768
C_WIDTH = 4

D_HEADS = 8
D_HEAD = 64

MLP_HIDDEN = 4 * D_MODEL
PLE_DIM = 256

LANES = 128
BF16_SUBLANES = 16
VMEM_LIMIT = 48 * 1024 * 1024

PROJ_TM = 512
QK_TN = 512
A_QKV_TN = 1024
MLP_TH = 1024
ATTN_TQ = 512
ATTN_KC = 512
ATTN_TILES = 4
ATTN_LOOKAHEAD = 5
BAND_TQ = 128
BAND_UNROLL = 8

B_Q_SCALE = (B_NOPE + B_ROPE) ** -0.5 * LOG2_E
CD_Q_SCALE = HEAD_DIM ** -0.5 * LOG2_E


def _params(*sem):
    return pltpu.CompilerParams(dimension_semantics=sem, vmem_limit_bytes=VMEM_LIMIT)


def _rms_f32(x, g):
    ms = jnp.mean(x * x, axis=-1, keepdims=True)
    return x * lax.rsqrt(ms + RMS_EPS) * g


def _rope_chunk(y, c, sa, sb, shift):
    return (y * c + pltpu.roll(y, LANES - shift, 1) * sa + pltpu.roll(y, shift, 1) * sb)


def _dot(a, b):
    return jnp.dot(a, b, preferred_element_type=F32)


def _dot_nt(a, b):
    return lax.dot_general(a, b, (((1,), (1,)), ((), ())), preferred_element_type=F32)


def _lane_iota():
    return lax.broadcasted_iota(jnp.int32, (1, LANES), 1)


def _norm_proj_kernel(x_ref, g_ref, w_ref, wvt_ref, c_ref, sa_ref, sb_ref, qk_ref, vt_ref,
                      *, rope, tn):
    xn = _rms_f32(x_ref[...], g_ref[...]).astype(BF16)
    n = qk_ref.shape[1]
    c, sa, sb = c_ref[...], sa_ref[...], sb_ref[...]
    for j in range(n // tn):
        y = _dot(xn, w_ref[:, j * tn:(j + 1) * tn])
        for i in range(tn // LANES):
            col = j * tn + i * LANES
            chunk = y[:, i * LANES:(i + 1) * LANES]
            if rope:
                chunk = _rope_chunk(chunk, c, sa, sb, ROT_DIM // 2)
            if col < n // 2:
                chunk = chunk * CD_Q_SCALE
            qk_ref[:, col:col + LANES] = chunk.astype(BF16)
    vt_ref[...] = _dot_nt(wvt_ref[...], xn).astype(BF16)


def _norm_proj(x, g, w, wvt, tabs, rope, name):
    n = w.shape[1]
    nv = wvt.shape[0]
    tm = PROJ_TM
    nb = SEQ // tm
    tab_spec = pl.BlockSpec((tm, LANES), lambda i: (i % nb, 0))
    return pl.pallas_call(
        functools.partial(_norm_proj_kernel, rope=rope, tn=QK_TN),
        grid=(TOKENS // tm,),
        in_specs=[
            pl.BlockSpec((tm, D_MODEL), lambda i: (i, 0)),
            pl.BlockSpec((1, D_MODEL), lambda i: (0, 0)),
            pl.BlockSpec((D_MODEL, n), lambda i: (0, 0)),
            pl.BlockSpec((nv, D_MODEL), lambda i: (0, 0)),
            tab_spec, tab_spec, tab_spec,
        ],
        out_specs=[pl.BlockSpec((tm, n), lambda i: (i, 0)),
                   pl.BlockSpec((nv, tm), lambda i: (0, i))],
        out_shape=[jax.ShapeDtypeStruct((TOKENS, n), BF16),
                   jax.ShapeDtypeStruct((nv, TOKENS), BF16)],
        compiler_params=_params("parallel"),
        name=name,
    )(x, g, w, wvt, *tabs)


def _mixer_residual(x_ref, o_ref, w_ref):
    return x_ref[...] + _dot(o_ref[...], w_ref[...])


def _tail_kernel(*refs, n_front, front, final):
    front_refs = refs[:n_front]
    gm_ref, wu_ref, wd_ref, gp_ref, wg_ref, p_ref, wp_ref, fg_ref, out_ref = refs[n_front:]
    x1 = front(*front_refs)
    xn = _rms_f32(x1, gm_ref[...]).astype(BF16)
    acc = x1
    for j in range(MLP_HIDDEN // MLP_TH):
        h = _dot(xn, wu_ref[:, j * MLP_TH:(j + 1) * MLP_TH])
        h = jnp.square(jnp.maximum(h, 0.0)).astype(BF16)
        acc = acc + _dot(h, wd_ref[j * MLP_TH:(j + 1) * MLP_TH, :])
    gate = jax.nn.sigmoid(_dot(_rms_f32(acc, gp_ref[...]).astype(BF16), wg_ref[...]))
    y = acc + gate * _dot(p_ref[...].astype(BF16), wp_ref[...])
    if final:
        y = _rms_f32(y, fg_ref[...])
    out_ref[...] = y


def _resident(a):
    return pl.BlockSpec(a.shape, lambda i: (0,) * a.ndim, pipeline_mode=pl.Buffered(1))


def _resident_layer(a, layer):
    return pl.BlockSpec((None,) + a.shape[1:], lambda i: (layer,) + (0,) * (a.ndim - 1),
                        pipeline_mode=pl.Buffered(1))


def _tail(front, front_args, front_specs, layer, gm, wu, wd, gp, wg, p, wp, fg, name):
    tm = PROJ_TM
    nt = TOKENS // tm
    shared = (gm, wu, wd, gp, wg)
    return pl.pallas_call(
        functools.partial(_tail_kernel, n_front=len(front_args), front=front,
                          final=layer == DEPTH - 1),
        grid=(nt,),
        in_specs=list(front_specs) + [_resident_layer(a, layer) for a in shared] + [
            pl.BlockSpec((tm, PLE_DIM), lambda i: (layer * nt + i, 0)),
            _resident_layer(wp, layer), _resident(fg)],
        out_specs=pl.BlockSpec((tm, D_MODEL), lambda i: (i, 0)),
        out_shape=jax.ShapeDtypeStruct((TOKENS, D_MODEL), F32),
        compiler_params=_params("parallel"),
        name=name,
    )(*front_args, *shared, p, wp, fg)


def _tail_plain(x, o, w_o, *rest, name):
    tm = PROJ_TM
    specs = [pl.BlockSpec((tm, D_MODEL), lambda i: (i, 0)),
             pl.BlockSpec((tm, o.shape[1]), lambda i: (i, 0)), _resident(w_o)]
    return _tail(_mixer_residual, (x, o, w_o), specs, *rest, name=name)


def _a_qkv_kernel(x_ref, g_ref, w_ref, c_ref, sa_ref, sb_ref, *refs):
    outs, stage = refs[:9], refs[9]
    tm = x_ref.shape[0]
    nc = A_GROUP_CHUNKS
    n_groups = len(DIL_PAIRS)
    xn = _rms_f32(x_ref[...], g_ref[...]).astype(BF16)
    c, sa, sb = c_ref[...], sa_ref[...], sb_ref[...]
    n_chunks = 3 * n_groups * nc
    per_dot = A_QKV_TN // LANES
    for first in range(0, n_chunks, per_dot):
        last = min(first + per_dot, n_chunks)
        y = _dot(xn, w_ref[:, first * LANES:last * LANES])
        for idx in range(first, last):
            chunk = y[:, (idx - first) * LANES:(idx - first + 1) * LANES]
            if idx < 2 * n_groups * nc:
                g, part, ci = idx // (2 * nc), (idx // nc) % 2, idx % nc
                chunk = _rope_chunk(chunk, c, sa, sb, ROT_DIM // 2)
            else:
                g, part, ci = (idx - 2 * n_groups * nc) // nc, 2, idx % nc
            dil = DIL_PAIRS[g][1]
            dst = outs[3 * g + part]
            if dil == 1:
                dst[0, ci, 0] = chunk.astype(BF16)
            else:
                slot = idx % 2
                stage[slot] = chunk
                for r in range(dil):
                    dst[0, ci, r] = stage[slot, pl.ds(r, tm // dil, stride=dil), :].astype(BF16)


def _a_qkv(x, g, w, tabs):
    tm = PROJ_TM
    nb = SEQ // tm
    gl = A_GROUP_LANES
    tab_spec = pl.BlockSpec((tm, LANES), lambda i: (i % nb, 0))
    out_shapes, out_specs = [], []
    nc = A_GROUP_CHUNKS
    for _, dil in DIL_PAIRS:
        for _ in range(3):
            out_shapes.append(jax.ShapeDtypeStruct((BATCH, nc, dil, SEQ // dil, LANES), BF16))
            out_specs.append(pl.BlockSpec((1, nc, dil, tm // dil, LANES),
                                          lambda i: (i // nb, 0, 0, i % nb, 0)))
    return pl.pallas_call(
        _a_qkv_kernel,
        grid=(TOKENS // tm,),
        in_specs=[
            pl.BlockSpec((tm, D_MODEL), lambda i: (i, 0)),
            pl.BlockSpec((1, D_MODEL), lambda i: (0, 0)),
            pl.BlockSpec((D_MODEL, 9 * gl), lambda i: (0, 0)),
            tab_spec, tab_spec, tab_spec,
        ],
        out_specs=out_specs,
        out_shape=out_shapes,
        scratch_shapes=[pltpu.VMEM((2, tm, LANES), F32)],
        compiler_params=_params("parallel"),
        name="a_qkv",
    )(x, g, w, *tabs)


def _a_attn_kernel(q_ref, k_ref, v_ref, o_ref, lse_ref, *, seg, dil, token_order):
    tq = BAND_TQ
    kw = tq + 2 * A_BAND_HALF
    low = _lane_iota() < HEAD_DIM
    ones = jnp.ones((kw, LANES), BF16)

    def tile(u):
        p0 = pl.multiple_of(u * tq, tq)
        if seg >= kw:
            seg0 = (p0 // seg) * seg
            ws = jnp.clip(p0 - A_BAND_HALF, seg0, seg0 + seg - kw)
        else:
            ws = (p0 // kw) * kw
        ws = pl.multiple_of(ws, A_BAND_HALF)
        rows = p0 + lax.broadcasted_iota(jnp.int32, (tq, 1), 0)
        cols = ws + lax.broadcasted_iota(jnp.int32, (1, kw), 1)
        valid = jnp.abs(cols - rows) <= A_BAND_HALF
        if seg < kw:
            seg_of = lambda p: lax.shift_right_logical(p, seg.bit_length() - 1)
            valid = jnp.logical_and(valid, seg_of(cols) == seg_of(rows))
        r = p0 // seg
        q0 = p0 - r * seg
        for c in range(A_GROUP_CHUNKS):
            qc = q_ref[0, c, pl.ds(p0, tq), :] * jnp.asarray(HEAD_DIM ** -0.5, BF16)
            kc = k_ref[0, c, pl.ds(ws, kw), :]
            v1 = jnp.concatenate([v_ref[0, c, pl.ds(ws, kw), :], ones], axis=1)
            outs, lses = [], []
            n_heads = 2 if 2 * c + 1 < A_GROUP_HEADS else 1
            for hh in range(n_heads):
                qm = jnp.where(low if hh == 0 else jnp.logical_not(low), qc, jnp.zeros_like(qc))
                s = jnp.where(valid, _dot_nt(qm, kc), NEG_INF)
                m = jnp.max(s, axis=-1, keepdims=True)
                ol = _dot(jnp.exp(s - m).astype(BF16), v1)
                l = ol[:, LANES:]
                outs.append(ol[:, :LANES] * (1.0 / l))
                lses.append(m + jnp.log(l))
            if n_heads == 2:
                o = jnp.where(low, outs[0], outs[1])
                ls = jnp.where(low, lses[0], lses[1])
            else:
                o = jnp.where(low, outs[0], 0.0)
                ls = jnp.where(low, lses[0], 0.0)
            if not token_order or dil == 1:
                dst = pl.ds(p0, tq)
            else:
                dst = pl.ds(r + dil * q0, tq, stride=dil)
            o_ref[0, c, dst, :] = o
            lse_ref[0, c, dst, :] = ls

    def body(i, carry):
        for j in range(BAND_UNROLL):
            tile(i * BAND_UNROLL + j)
        return carry

    lax.fori_loop(0, SEQ // tq // BAND_UNROLL, body, 0)


def _a_attn(q, k, v, dil, token_order):
    seg = SEQ // dil
    nc = A_GROUP_CHUNKS
    q, k, v = (a.reshape(BATCH, nc, SEQ, LANES) for a in (q, k, v))
    in_spec = pl.BlockSpec((1, nc, SEQ, LANES), lambda b: (b, 0, 0, 0))
    out_spec = pl.BlockSpec((1, nc, SEQ, LANES), lambda b: (b, 0, 0, 0))
    out_shape = jax.ShapeDtypeStruct((BATCH, nc, SEQ, LANES), F32)
    return pl.pallas_call(
        functools.partial(_a_attn_kernel, seg=seg, dil=dil, token_order=token_order),
        grid=(BATCH,),
        in_specs=[in_spec, in_spec, in_spec],
        out_specs=[out_spec, out_spec],
        out_shape=[out_shape, out_shape],
        compiler_params=_params("parallel"),
        name=f"a_attn_d{dil}",
    )(q, k, v)


def _a_combine_kernel(o0, o1, o2, l0, l1, l2, out_ref):
    tm = out_ref.shape[0]
    dil = DIL_PAIRS[-1][1]
    seg = SEQ // dil
    pos0 = (pl.program_id(0) % (SEQ // tm)) * (tm // dil)

    def token_rows(ref, c):
        return jnp.concatenate(
            [ref[0, c, pl.ds(pos0 + p, dil, stride=seg), :] for p in range(tm // dil)], axis=0)

    for c in range(A_GROUP_CHUNKS):
        ls = [l0[0, c], l1[0, c], token_rows(l2, c)]
        os_ = [o0[0, c], o1[0, c], token_rows(o2, c)]
        m = jnp.maximum(jnp.maximum(ls[0], ls[1]), ls[2])
        es = [jnp.exp(l - m) for l in ls]
        inv = 1.0 / (es[0] + es[1] + es[2])
        for g in range(len(DIL_PAIRS)):
            col = g * A_GROUP_LANES + c * LANES
            out_ref[:, col:col + LANES] = (os_[g] * (es[g] * inv)).astype(BF16)


def _a_combine(os_, ls_):
    tm = PROJ_TM
    nb = SEQ // tm
    n = len(DIL_PAIRS) * A_GROUP_LANES
    part = pl.BlockSpec((1, A_GROUP_CHUNKS, tm, LANES), lambda i: (i // nb, 0, i % nb, 0))
    whole = pl.BlockSpec((1, A_GROUP_CHUNKS, SEQ, LANES), lambda i: (i // nb, 0, 0, 0))
    return pl.pallas_call(
        _a_combine_kernel,
        grid=(TOKENS // tm,),
        in_specs=[part, part, whole, part, part, whole],
        out_specs=pl.BlockSpec((tm, n), lambda i: (i, 0)),
        out_shape=jax.ShapeDtypeStruct((TOKENS, n), BF16),
        compiler_params=_params("parallel"),
        name="a_combine",
    )(*os_, *ls_)


def _b_proj_kernel(x_ref, g_ref, win_ref, qn_ref, kvn_ref, wuq_ref, wuqr_ref, wuk_ref, wuvt_ref,
                   c_ref, sa_ref, sb_ref, q_out, k_out, vt_out):
    c = c_ref[...]
    s = sb_ref[...] - sa_ref[...]
    slot = B_Q_RANK + B_KV_RANK
    xn = _rms_f32(x_ref[...], g_ref[...]).astype(BF16)
    z = _dot(xn, win_ref[...])
    cq = _rms_f32(z[:, :B_Q_RANK], qn_ref[...]).astype(BF16)
    ckv = _rms_f32(z[:, B_Q_RANK:slot], kvn_ref[...]).astype(BF16)
    k_rope = z[:, slot:slot + LANES] * c + z[:, slot + LANES:] * s
    q = _dot(cq, wuq_ref[...])
    qr = _dot(cq, wuqr_ref[...])
    k = _dot(ckv, wuk_ref[...])
    cq_tab, sq_tab = c * B_Q_SCALE, s * B_Q_SCALE
    for h in range(B_HEADS):
        sl = slice(h * LANES, (h + 1) * LANES)
        q_out[:, sl] = (q[:, sl] * cq_tab + qr[:, sl] * sq_tab).astype(BF16)
        k_out[:, sl] = (k[:, sl] + k_rope).astype(BF16)
    vt_out[...] = _dot_nt(wuvt_ref[...], ckv).astype(BF16)


def _b_proj(x, g, win, qn, kvn, wuq, wuqr, wuk, wuvt, tabs):
    tm = PROJ_TM
    nb = SEQ // tm
    tab_spec = pl.BlockSpec((tm, LANES), lambda i: (i % nb, 0))

    def full(a):
        return pl.BlockSpec(a.shape, lambda i: (0,) * a.ndim)

    nqk = B_HEADS * LANES
    nv = B_HEADS * B_V
    return pl.pallas_call(
        _b_proj_kernel,
        grid=(TOKENS // tm,),
        in_specs=[pl.BlockSpec((tm, D_MODEL), lambda i: (i, 0)), full(g), full(win), full(qn),
                  full(kvn), full(wuq), full(wuqr), full(wuk), full(wuvt),
                  tab_spec, tab_spec, tab_spec],
        out_specs=[pl.BlockSpec((tm, nqk), lambda i: (i, 0)),
                   pl.BlockSpec((tm, nqk), lambda i: (i, 0)),
                   pl.BlockSpec((nv, tm), lambda i: (0, i))],
        out_shape=[jax.ShapeDtypeStruct((TOKENS, nqk), BF16),
                   jax.ShapeDtypeStruct((TOKENS, nqk), BF16),
                   jax.ShapeDtypeStruct((nv, TOKENS), BF16)],
        compiler_params=_params("parallel"),
        name="b_proj",
    )(x, g, win, qn, kvn, wuq, wuqr, wuk, wuvt, *tabs)


def _with_ones_rows(vt):
    return jnp.concatenate([vt, jnp.ones((BF16_SUBLANES, vt.shape[1]), vt.dtype)], axis=0)


def _attend_keys_major(streams, n_chunks, finish, width):
    items = []
    for s0 in range(0, len(streams), width):
        items += [(s, j) for j in range(n_chunks) for s in range(s0, min(s0 + width, len(streams)))]
    state = [None] * len(streams)

    def scores(item):
        s, j = item
        k_chunk, _, qw, _ = streams[s]
        return _dot_nt(k_chunk(j), qw)

    def absorb(item, st):
        s, j = item
        vt1 = _with_ones_rows(streams[s][1](j))
        bias_chunk = streams[s][3]
        if bias_chunk is not None:
            st = st + bias_chunk(j)
        mj = jnp.max(st, axis=0, keepdims=True)
        if j == 0:
            state[s] = (mj, _dot(vt1, jnp.exp2(st - mj).astype(BF16)))
        else:
            m, acc = state[s]
            m_new = jnp.maximum(m, mj)
            e = jnp.exp2(st - m_new).astype(BF16)
            state[s] = (m_new, acc * jnp.exp2(m - m_new) + _dot(vt1, e))
        if j == n_chunks - 1:
            finish(s, state[s][1])
            state[s] = None

    ahead = ATTN_LOOKAHEAD
    pending = [scores(item) for item in items[:ahead]]
    for idx, item in enumerate(items):
        if idx + ahead < len(items):
            pending.append(scores(items[idx + ahead]))
        absorb(item, pending.pop(0))


def _b_attn_kernel(q_ref, k_ref, vt_ref, o_ref):
    kc, tq = ATTN_KC, ATTN_TQ
    outs = {}

    def stream(t, hh):
        sl = slice(hh * LANES, (hh + 1) * LANES)
        rows = slice(hh * B_V, (hh + 1) * B_V)
        return (lambda j: k_ref[j * kc:(j + 1) * kc, sl],
                lambda j: vt_ref[rows, j * kc:(j + 1) * kc],
                q_ref[t * tq:(t + 1) * tq, sl], None)

    def finish(s, acc):
        t, hh = divmod(s, 2)
        outs[hh] = acc[:B_V] * (1.0 / acc[B_V:B_V + 1])
        if hh == 1:
            o = jnp.concatenate([outs[0], outs[1]], axis=0)
            o_ref[t * tq:(t + 1) * tq, :] = o.T.astype(BF16)

    _attend_keys_major([stream(t, hh) for t in range(ATTN_TILES) for hh in range(2)],
                       SEQ // kc, finish, width=2 * ATTN_TILES)


def _b_attn(q, k, vt):
    rows = ATTN_TQ * ATTN_TILES
    nq = SEQ // rows
    return pl.pallas_call(
        _b_attn_kernel,
        grid=(BATCH, B_HEADS // 2, nq),
        in_specs=[
            pl.BlockSpec((rows, 2 * LANES), lambda b, h, i: (b * nq + i, h)),
            pl.BlockSpec((SEQ, 2 * LANES), lambda b, h, i: (b, h)),
            pl.BlockSpec((2 * B_V, SEQ), lambda b, h, i: (h, b)),
        ],
        out_specs=pl.BlockSpec((rows, LANES), lambda b, h, i: (b * nq + i, h)),
        out_shape=jax.ShapeDtypeStruct((TOKENS, B_HEADS * B_V), BF16),
        compiler_params=_params("parallel", "parallel", "arbitrary"),
        name="b_attn",
    )(q, k, vt)


def _c_group_geometry(gi):
    first_row = gi * C_ROW_GROUP
    wrow = min(max(first_row - NA_ROWS // 2, 0), GRID_ROWS - C_WIN_ROWS)
    n_groups = GRID_ROWS // C_ROW_GROUP
    case = 0 if gi == 0 else (2 if gi == n_groups - 1 else 1)
    return first_row, wrow, case


def _c_row_offset_slot(gi, kr, i):
    first_row, wrow, _ = _c_group_geometry(gi)
    qrow, krow = first_row + i, wrow + kr
    rs = min(max(qrow - NA_ROWS // 2, 0), GRID_ROWS - NA_ROWS)
    if rs <= krow < rs + NA_ROWS:
        return krow - qrow + NA_ROWS - 1
    return 2 * NA_ROWS - 1


def _c_attn_kernel(q_ref, k_ref, vt_ref, left_ref, right_ref, o_ref, bias_ref):
    low = _lane_iota() < HEAD_DIM
    nq = C_ROW_GROUP * GRID_W
    nk = C_WIN_ROWS * GRID_W
    n_groups = GRID_ROWS // C_ROW_GROUP

    @pl.when(pl.program_id(1) == 0)
    def _():
        for case, gi in enumerate((0, 1, n_groups - 1)):
            for kr in range(C_WIN_ROWS):
                for hh in range(2):
                    for ip in range(C_ROW_GROUP // 2):
                        a0 = _c_row_offset_slot(gi, kr, 2 * ip)
                        a1 = _c_row_offset_slot(gi, kr, 2 * ip + 1)
                        col = (hh * C_ROW_GROUP // 2 + ip) * LANES
                        bias_ref[case, kr * GRID_W:(kr + 1) * GRID_W, col:col + LANES] = (
                            left_ref[0, hh, a0] + right_ref[0, hh, a1])

    kc = C_KEY_CHUNK

    def stream(gi):
        first_row, wrow, case = _c_group_geometry(gi)
        q0, ws = first_row * GRID_W, wrow * GRID_W
        qc = q_ref[q0:q0 + nq, :]
        zero = jnp.zeros_like(qc)
        qq = jnp.concatenate([jnp.where(low, qc, zero), jnp.where(low, zero, qc)], axis=0)
        return (lambda j: k_ref[ws + j * kc:ws + (j + 1) * kc, :],
                lambda j: vt_ref[:, ws + j * kc:ws + (j + 1) * kc],
                qq,
                lambda j: bias_ref[case, j * kc:(j + 1) * kc, :])

    def finish(gi, acc):
        q0 = _c_group_geometry(gi)[0] * GRID_W
        ot = acc[:LANES] * (1.0 / acc[LANES:LANES + 1])
        o = jnp.concatenate([ot[:HEAD_DIM, :nq], ot[HEAD_DIM:, nq:]], axis=0)
        o_ref[q0:q0 + nq, :] = o.T.astype(BF16)

    _attend_keys_major([stream(gi) for gi in range(n_groups)], nk // kc, finish, width=C_WIDTH)


def _c_attn(qk, vt, left, right):
    nq = C_ROW_GROUP * GRID_W
    nk = C_WIN_ROWS * GRID_W
    npair = C_HEADS // 2
    tab_spec = pl.BlockSpec((1, 2, 2 * NA_ROWS, GRID_W, LANES), lambda h, b: (h, 0, 0, 0, 0))
    return pl.pallas_call(
        _c_attn_kernel,
        grid=(npair, BATCH),
        in_specs=[
            pl.BlockSpec((SEQ, LANES), lambda h, b: (b, h)),
            pl.BlockSpec((SEQ, LANES), lambda h, b: (b, npair + h)),
            pl.BlockSpec((LANES, SEQ), lambda h, b: (h, b)),
            tab_spec, tab_spec,
        ],
        out_specs=pl.BlockSpec((SEQ, LANES), lambda h, b: (b, h)),
        out_shape=jax.ShapeDtypeStruct((TOKENS, C_HEADS * HEAD_DIM), BF16),
        scratch_shapes=[pltpu.VMEM((3, nk, 2 * nq), F32)],
        compiler_params=_params("arbitrary", "arbitrary"),
        name="c_attn",
    )(qk, qk, vt, left, right)


def _c_column_tables(rpb):
    kc, qc = np.arange(GRID_W)[:, None], np.arange(GRID_W)[None, :]
    win0 = np.clip(qc - NA_COLS // 2, 0, GRID_W - NA_COLS)
    col_ok = (kc >= win0) & (kc < win0 + NA_COLS)
    col_off = np.clip(kc - qc + NA_COLS - 1, 0, 2 * NA_COLS - 2)
    col_sel = (col_off[None] == np.arange(2 * NA_COLS - 1)[:, None, None]).astype(np.float32)
    cols = jnp.einsum("hab,bkq->hakq", rpb, col_sel, precision=lax.Precision.HIGHEST)
    cols = jnp.where(col_ok, cols * LOG2_E, NEG_INF)
    cols = jnp.concatenate([cols, jnp.full((C_HEADS, 1, GRID_W, GRID_W), NEG_INF, F32)], axis=1)
    cols = cols.reshape(C_HEADS // 2, 2, 2 * NA_ROWS, GRID_W, GRID_W)
    zeros = jnp.zeros_like(cols)
    return jnp.concatenate([cols, zeros], axis=-1), jnp.concatenate([zeros, cols], axis=-1)


def _d_attn_kernel(lq1, lk1, lq2, lk2, q_ref, k_ref, vt_ref, sub_ref, o_ref, *, lambda_init):
    lam = (jnp.exp(jnp.sum(lq1[...] * lk1[...], axis=-1, keepdims=True))
           - jnp.exp(jnp.sum(lq2[...] * lk2[...], axis=-1, keepdims=True)) + lambda_init)
    low = _lane_iota() < D_HEAD
    kc, tq, dv = ATTN_KC, ATTN_TQ, 2 * D_HEAD
    k_chunk = lambda j: k_ref[j * kc:(j + 1) * kc, :]
    vt_chunk = lambda j: vt_ref[:, j * kc:(j + 1) * kc]
    outs = {}

    def stream(t):
        q = q_ref[t * tq:(t + 1) * tq, :]
        zero = jnp.zeros_like(q)
        qw = jnp.concatenate([jnp.where(low, q, zero), jnp.where(low, zero, q)], axis=0)
        return (k_chunk, vt_chunk, qw, None)

    def finish(t, acc):
        on = acc[:dv] * (1.0 / acc[dv:dv + 1])
        o = (on[:, :tq] - lam * on[:, tq:]).T
        o_ref[t * tq:(t + 1) * tq, :] = (
            _rms_f32(o, sub_ref[...]) * (1.0 - lambda_init)).astype(BF16)

    _attend_keys_major([stream(t) for t in range(ATTN_TILES)], SEQ // kc, finish,
                       width=ATTN_TILES)


def _d_attn(qk, vt, lq1, lk1, lq2, lk2, subln, lambda_init):
    tq = ATTN_TQ * ATTN_TILES
    nq = SEQ // tq
    vec = pl.BlockSpec((1, D_HEAD), lambda b, h, i: (0, 0))
    return pl.pallas_call(
        functools.partial(_d_attn_kernel, lambda_init=lambda_init),
        grid=(BATCH, D_HEADS, nq),
        in_specs=[
            vec, vec, vec, vec,
            pl.BlockSpec((tq, LANES), lambda b, h, i: (b * nq + i, h)),
            pl.BlockSpec((SEQ, LANES), lambda b, h, i: (b, D_HEADS + h)),
            pl.BlockSpec((LANES, SEQ), lambda b, h, i: (h, b)),
            pl.BlockSpec((1, 2 * D_HEAD), lambda b, h, i: (0, 0)),
        ],
        out_specs=pl.BlockSpec((tq, LANES), lambda b, h, i: (b * nq + i, h)),
        out_shape=jax.ShapeDtypeStruct((TOKENS, 2 * D_HEADS * D_HEAD), BF16),
        compiler_params=_params("parallel", "parallel", "arbitrary"),
        name="d_attn",
    )(lq1, lk1, lq2, lk2, qk, qk, vt, subln)


def _rope_tables(rot_dim, period, lane0):
    r = rot_dim // 2
    inv = ROPE_THETA ** (-jnp.arange(0, rot_dim, 2, dtype=F32) / rot_dim)
    ang = jnp.arange(SEQ, dtype=F32)[:, None] * inv[None, :]
    cos, sin = jnp.cos(ang), jnp.sin(ang)
    lane = np.arange(LANES) % period - lane0
    first = (lane >= 0) & (lane < r)
    second = (lane >= r) & (lane < 2 * r)
    idx = np.where(first, lane, np.where(second, lane - r, 0))
    cg, sg = cos[:, idx], sin[:, idx]
    c = jnp.where(first | second, cg, 1.0)
    sa = jnp.where(first, -sg, 0.0)
    sb = jnp.where(second, sg, 0.0)
    return c, sa, sb


def _a_weights(w_qkv, w_o):
    nh = A_GROUP_HEADS * HEAD_DIM
    pad = jnp.zeros((D_MODEL, A_GROUP_LANES - nh), w_qkv.dtype)

    def cols(part, g):
        base = part * A_HEADS * HEAD_DIM + g * nh
        return [w_qkv[:, base:base + nh], pad]

    pieces = []
    for g in range(len(DIL_PAIRS)):
        pieces += cols(0, g) + cols(1, g)
    for g in range(len(DIL_PAIRS)):
        pieces += cols(2, g)
    w = jnp.concatenate(pieces, axis=1).astype(BF16)
    wo = w_o.reshape(len(DIL_PAIRS), nh, D_MODEL)
    wo = jnp.pad(wo, ((0, 0), (0, A_GROUP_LANES - nh), (0, 0))).astype(BF16)
    return w, wo.reshape(len(DIL_PAIRS) * A_GROUP_LANES, D_MODEL)


def _rotate_half_columns(w_rope):
    r = B_ROPE // 2
    return jnp.concatenate([-w_rope[..., r:], w_rope[..., :r]], axis=-1)


def _b_weights(w_in, w_uq, w_ukv):
    split = B_Q_RANK + B_KV_RANK
    pad_lo = jnp.zeros((D_MODEL, B_NOPE), w_in.dtype)
    pad_hi = jnp.zeros((D_MODEL, LANES - B_NOPE - B_ROPE), w_in.dtype)
    win = jnp.concatenate([
        w_in[:, :split], pad_lo, w_in[:, split:], pad_hi,
        pad_lo, _rotate_half_columns(w_in[:, split:]), pad_hi], axis=1).astype(BF16)
    wuq = w_uq.reshape(B_Q_RANK, B_HEADS, B_NOPE + B_ROPE)
    slot_pad = ((0, 0), (0, 0), (0, LANES - B_NOPE - B_ROPE))
    wuqr = jnp.concatenate([jnp.zeros_like(wuq[:, :, :B_NOPE]),
                            _rotate_half_columns(wuq[:, :, B_NOPE:])], axis=-1)
    wuqr = jnp.pad(wuqr, slot_pad).reshape(B_Q_RANK, B_HEADS * LANES).astype(BF16)
    wuq = jnp.pad(wuq, slot_pad)
    wuq = wuq.reshape(B_Q_RANK, B_HEADS * LANES).astype(BF16)
    wukv = w_ukv.reshape(B_KV_RANK, B_HEADS, B_NOPE + B_V)
    wuk = jnp.pad(wukv[:, :, :B_NOPE], ((0, 0), (0, 0), (0, LANES - B_NOPE)))
    wuk = wuk.reshape(B_KV_RANK, B_HEADS * LANES).astype(BF16)
    wuvt = wukv[:, :, B_NOPE:].reshape(B_KV_RANK, B_HEADS * B_V).T.astype(BF16)
    return win, wuq, wuqr, wuk, wuvt


def _split_qk_vt(w_qkv, n_qk):
    return w_qkv[:, :n_qk].astype(BF16), w_qkv[:, n_qk:].T.astype(BF16)


def kernel(x, p, a_norm, a_w_qkv, a_w_o, b_norm, b_w_in, b_q_norm, b_w_uq, b_kv_norm, b_w_ukv, b_w_o, c_norm, c_w_qkv, c_rpb, c_w_o, d_norm, d_w_qkv, d_lambda_q1, d_lambda_k1, d_lambda_q2, d_lambda_k2, d_subln, d_w_o, mlp_norm, w_up, w_down, ple_norm, w_ple_gate, w_ple_proj, final_norm):
    tabs_p = _rope_tables(ROT_DIM, HEAD_DIM, 0)
    tabs_l = _rope_tables(B_ROPE, LANES, B_NOPE)
    h = x.reshape(TOKENS, D_MODEL)
    row = lambda v: v.reshape(1, -1)

    stacks = (mlp_norm.reshape(DEPTH, 1, D_MODEL), w_up.astype(BF16), w_down.astype(BF16),
              ple_norm.reshape(DEPTH, 1, D_MODEL), w_ple_gate.astype(BF16),
              p.reshape(DEPTH * TOKENS, PLE_DIM), w_ple_proj.astype(BF16), row(final_norm))
    for i in range(DEPTH):
        rest = (i,) + stacks
        if i == 0:
            w, wo = _a_weights(a_w_qkv[0], a_w_o[0])
            parts = _a_qkv(h, row(a_norm[0]), w, tabs_p)
            os_, ls_ = [], []
            for g, (_, dil) in enumerate(DIL_PAIRS):
                o, lse = _a_attn(parts[3 * g], parts[3 * g + 1], parts[3 * g + 2], dil,
                                 token_order=g < len(DIL_PAIRS) - 1)
                os_.append(o)
                ls_.append(lse)
            h = _tail_plain(h, _a_combine(os_, ls_), wo, *rest, name="a_tail")
        elif i == 1:
            b_w = _b_weights(b_w_in[0], b_w_uq[0], b_w_ukv[0])
            q, k, vt = _b_proj(h, row(b_norm[0]), b_w[0], row(b_q_norm[0]), row(b_kv_norm[0]),
                               *b_w[1:], tabs_l)
            h = _tail_plain(h, _b_attn(q, k, vt), b_w_o[0].astype(BF16), *rest, name="b_tail")
        elif i == 2:
            w, wvt = _split_qk_vt(c_w_qkv[0], 2 * C_HEADS * HEAD_DIM)
            qk, vt = _norm_proj(h, row(c_norm[0]), w, wvt, tabs_p, False, "c_qkv")
            o = _c_attn(qk, vt, *_c_column_tables(c_rpb[0]))
            h = _tail_plain(h, o, c_w_o[0].astype(BF16), *rest, name="c_tail")
        else:
            lambda_init = 0.8 - 0.6 * math.exp(-0.3 * i)
            w, wvt = _split_qk_vt(d_w_qkv[0], 2 * 2 * D_HEADS * D_HEAD)
            qk, vt = _norm_proj(h, row(d_norm[0]), w, wvt, tabs_p, True, "d_qkv")
            o = _d_attn(qk, vt, row(d_lambda_q1[0]), row(d_lambda_k1[0]), row(d_lambda_q2[0]),
                        row(d_lambda_k2[0]), row(d_subln[0]), lambda_init)
            h = _tail_plain(h, o, d_w_o[0].astype(BF16), *rest, name="d_tail")
    return h.reshape(BATCH, SEQ, D_MODEL)
```
